```python
import jax
import jax.numpy as jnp
from jax import lax
import numpy as np

D_MODEL = 1024
BATCH = 2
SEQ = 8192
DEPTH = 2

CTX_LEN = 256
GRID_W = 64
HEAD_DIM = 64
FOURIER_W = 256
FOURIER_GROUPS = 4
ATTN_HEADS = 6
ATTN_KV_HEADS = 2
ATTN_GROUP = ATTN_HEADS // ATTN_KV_HEADS
ATTN_W = ATTN_HEADS * HEAD_DIM
KV_W = ATTN_KV_HEADS * HEAD_DIM
RWKV_HEADS = 6
RWKV_W = RWKV_HEADS * HEAD_DIM
MIX_W = FOURIER_W + ATTN_W + RWKV_W
WINDOW = 128
BLOCK = 128
ROPE_BASE = 10000.0
DECAY_LORA = 64
ICLR_LORA = 64
GATE_LORA = 128
CONV_W = 3
FFN_DIM = 2752
N_MOD = 9
NORM_EPS = 1e-6
GN_EPS = 64e-5
IN_WIDTHS = (FOURIER_W, ATTN_W, KV_W, KV_W, RWKV_W, RWKV_W, RWKV_W, DECAY_LORA, DECAY_LORA, ICLR_LORA, ICLR_LORA, GATE_LORA)
IN_W = FOURIER_W + ATTN_W + 2 * KV_W + 3 * RWKV_W + 2 * DECAY_LORA + 2 * ICLR_LORA + GATE_LORA

kernel_name = 'hybrid_fourier_swa_rwkv7_prefix_dit'


def rms_norm(x, g):
    xf = x.astype(jnp.float32)
    y = xf * lax.rsqrt(jnp.mean(xf * xf, axis=-1, keepdims=True) + NORM_EPS)
    return (y * g.astype(jnp.float32)).astype(x.dtype)


def ada_mod(cond, w, b):
    m = jax.nn.silu(cond) @ w + b
    return m.reshape(cond.shape[:-1] + (N_MOD, D_MODEL))


def swiglu(h, wi, wo):
    gate, up = jnp.split(h @ wi, 2, axis=-1)
    return (jax.nn.silu(gate) * up) @ wo


def ffn_half(x, m, g_pre, g_post, wi, wo):
    h = rms_norm(x, g_pre) * (1.0 + m[:, 1][:, None]) + m[:, 0][:, None]
    return x + 0.5 * m[:, 2][:, None] * rms_norm(swiglu(h, wi, wo), g_post)


def split_cols(z):
    parts = []
    off = 0
    for w in IN_WIDTHS:
        parts.append(z[..., off:off + w])
        off += w
    return parts


def axial_rope(n_tokens):
    rows_n = n_tokens // GRID_W
    row = jnp.repeat(jnp.arange(rows_n), GRID_W).astype(jnp.float32)
    col = jnp.tile(jnp.arange(GRID_W), rows_n).astype(jnp.float32)
    n_freq = HEAD_DIM // 4
    inv = ROPE_BASE ** (-jnp.arange(n_freq, dtype=jnp.float32) / n_freq)
    ang = jnp.concatenate([row[:, None] * inv, col[:, None] * inv], axis=-1)
    return jnp.cos(ang), jnp.sin(ang)


def apply_rope(x, cos, sin):
    xf = x.astype(jnp.float32)
    half = HEAD_DIM // 2
    x1, x2 = xf[..., :half], xf[..., half:]
    c = cos[None, :, None, :]
    s = sin[None, :, None, :]
    return jnp.concatenate([x1 * c - x2 * s, x2 * c + x1 * s], axis=-1).astype(x.dtype)


def fourier_mix(z):
    b, n = z.shape[:2]
    zf = z.astype(jnp.float32).reshape(b, n, FOURIER_GROUPS, FOURIER_W // FOURIER_GROUPS)
    y = jnp.fft.fft2(zf, axes=(1, 3), norm='ortho').real
    return y.reshape(b, n, FOURIER_W).astype(z.dtype)


def window_attention(q, k, v, kc, vc, sink):
    b, s = q.shape[:2]
    nb = s // BLOCK
    f32 = jnp.float32
    qb = q.astype(f32).reshape(b, nb, BLOCK, ATTN_KV_HEADS, ATTN_GROUP, HEAD_DIM) * (HEAD_DIM ** -0.5)
    pad = ((0, 0), (BLOCK, BLOCK), (0, 0), (0, 0))
    kp = jnp.pad(k.astype(f32), pad).reshape(b, nb + 2, BLOCK, ATTN_KV_HEADS, HEAD_DIM)
    vp = jnp.pad(v.astype(f32), pad).reshape(b, nb + 2, BLOCK, ATTN_KV_HEADS, HEAD_DIM)
    kw = jnp.concatenate([kp[:, :-2], kp[:, 1:-1], kp[:, 2:]], axis=2)
    vw = jnp.concatenate([vp[:, :-2], vp[:, 1:-1], vp[:, 2:]], axis=2)
    s_w = jnp.einsum('bnqhgd,bnkhd->bnhgqk', qb, kw)
    s_c = jnp.einsum('bnqhgd,bchd->bnhgqc', qb, kc.astype(f32))
    qpos = jnp.arange(BLOCK)[:, None]
    kpos = jnp.arange(3 * BLOCK)[None, :] - BLOCK
    abs_k = jnp.arange(nb)[:, None, None] * BLOCK + kpos[None]
    valid = (jnp.abs(kpos - qpos) <= WINDOW)[None] & (abs_k >= 0) & (abs_k < s)
    s_w = jnp.where(valid[None, :, None, None], s_w, -jnp.inf)
    sk = sink.astype(f32).reshape(ATTN_KV_HEADS, ATTN_GROUP)[None, None, :, :, None]
    m = jnp.maximum(jnp.maximum(s_w.max(-1), s_c.max(-1)), sk)
    e_w = jnp.exp(s_w - m[..., None])
    e_c = jnp.exp(s_c - m[..., None])
    den = e_w.sum(-1) + e_c.sum(-1) + jnp.exp(sk - m)
    o = jnp.einsum('bnhgqk,bnkhd->bnqhgd', e_w, vw) + jnp.einsum('bnhgqc,bchd->bnqhgd', e_c, vc.astype(f32))
    o = o / jnp.moveaxis(den, -1, 2)[..., None]
    return o.reshape(b, s, ATTN_W).astype(q.dtype)


def context_attention(qc, kc, vc, sink):
    b, n = qc.shape[:2]
    f32 = jnp.float32
    q = qc.astype(f32).reshape(b, n, ATTN_KV_HEADS, ATTN_GROUP, HEAD_DIM) * (HEAD_DIM ** -0.5)
    s = jnp.einsum('bqhgd,bkhd->bhgqk', q, kc.astype(f32))
    sk = sink.astype(f32).reshape(ATTN_KV_HEADS, ATTN_GROUP)[None, :, :, None]
    m = jnp.maximum(s.max(-1), sk)
    e = jnp.exp(s - m[..., None])
    den = e.sum(-1) + jnp.exp(sk - m)
    o = jnp.einsum('bhgqk,bkhd->bqhgd', e, vc.astype(f32)) / jnp.moveaxis(den, -1, 1)[..., None]
    return o.reshape(b, n, ATTN_W).astype(qc.dtype)


def short_conv(x, w):
    n = x.shape[1]
    half = CONV_W // 2
    xp = jnp.pad(x, ((0, 0), (half, half), (0, 0)))
    y = xp[:, 0:n] * w[0]
    for j in range(1, CONV_W):
        y = y + xp[:, j:j + n] * w[j]
    return y


def wkv_step(state, inp):
    r, w, k, v, a, b = inp
    sa = jnp.einsum('zbhij,zbhj->zbhi', state, a)
    state = state * w[..., None, :] + sa[..., None] * b[..., None, :] + v[..., None] * k[..., None, :]
    y = jnp.einsum('zbhij,zbhj->zbhi', state, r)
    return state, y


def rwkv_scan(r, k, v, zw, za, state0, conv_w, w0, w2, a0, a2, k_k, k_a):
    b, n = r.shape[:2]
    f32 = jnp.float32
    rkv = short_conv(jnp.concatenate([r, k, v], axis=-1), conv_w).astype(f32)
    r, k, v = jnp.split(rkv, 3, axis=-1)
    w = -jax.nn.softplus(-(w0.astype(f32)[:, None, None, :] + jnp.einsum('zbnr,zrc->zbnc', jnp.tanh(zw.astype(f32)), w2.astype(f32)))) - 0.5
    decay = jnp.exp(-jnp.exp(w))
    a = jax.nn.sigmoid(a0.astype(f32)[:, None, None, :] + jnp.einsum('zbnr,zrc->zbnc', za.astype(f32), a2.astype(f32)))
    kk = (k * k_k.astype(f32)).reshape(b, n, RWKV_HEADS, HEAD_DIM)
    kk = (kk * lax.rsqrt(jnp.maximum(jnp.sum(kk * kk, -1, keepdims=True), 1e-24))).reshape(b, n, RWKV_W)
    k_dir = k * (1.0 + (a - 1.0) * k_a.astype(f32))
    seqs = [jnp.broadcast_to(r, a.shape), decay, k_dir, jnp.broadcast_to(v, a.shape), jnp.broadcast_to(-kk, a.shape), kk * a]

    def time_major(t):
        t = jnp.stack([t[0], jnp.flip(t[1], axis=1)])
        return jnp.moveaxis(t.reshape(2, b, n, RWKV_HEADS, HEAD_DIM), 2, 0)

    s_fin, y = lax.scan(wkv_step, state0, [time_major(t) for t in seqs])
    y = jnp.moveaxis(y, 0, 2)
    y = y[0] + jnp.flip(y[1], axis=1)
    return y.reshape(b, n, RWKV_W), s_fin, r, k, v


def rwkv_readout(y, r, k, v, zg, g2, r_k, ln_g, ln_b, out_dtype):
    b, n = y.shape[:2]
    f32 = jnp.float32
    yh = y.reshape(b, n, RWKV_HEADS, HEAD_DIM)
    mu = yh.mean(-1, keepdims=True)
    var = jnp.mean(jnp.square(yh - mu), -1, keepdims=True)
    yh = (yh - mu) * lax.rsqrt(var + GN_EPS)
    yh = yh * ln_g.astype(f32).reshape(RWKV_HEADS, HEAD_DIM) + ln_b.astype(f32).reshape(RWKV_HEADS, HEAD_DIM)
    rh = r.reshape(b, n, RWKV_HEADS, HEAD_DIM)
    kh = k.reshape(b, n, RWKV_HEADS, HEAD_DIM)
    bonus = jnp.sum(rh * kh * r_k.astype(f32), -1, keepdims=True) * v.reshape(b, n, RWKV_HEADS, HEAD_DIM)
    gate = jax.nn.sigmoid(zg.astype(f32)) @ g2.astype(f32)
    return ((yh + bonus).reshape(b, n, RWKV_W) * gate).astype(out_dtype)


def token_mixer(hl, hc, w_in, w_out, sink, conv_w, w0, w2, a0, a2, g2, k_k, k_a, r_k, ln_g, ln_b, need_ctx_out):
    b, s = hl.shape[:2]
    n_c = hc.shape[1]
    f_l, q_l, k_l, v_l, rr_l, rk_l, rv_l, wf_l, wb_l, af_l, ab_l, g_l = split_cols(hl @ w_in)
    f_c, q_c, k_c, v_c, rr_c, rk_c, rv_c, wf_c, wb_c, af_c, ab_c, g_c = split_cols(hc @ w_in)
    cos, sin = axial_rope(s)
    q = apply_rope(q_l.reshape(b, s, ATTN_HEADS, HEAD_DIM), cos, sin)
    k = apply_rope(k_l.reshape(b, s, ATTN_KV_HEADS, HEAD_DIM), cos, sin)
    v = v_l.reshape(b, s, ATTN_KV_HEADS, HEAD_DIM)
    kc = k_c.reshape(b, n_c, ATTN_KV_HEADS, HEAD_DIM)
    vc = v_c.reshape(b, n_c, ATTN_KV_HEADS, HEAD_DIM)
    attn_l = window_attention(q, k, v, kc, vc, sink)
    state0 = jnp.zeros((2, b, RWKV_HEADS, HEAD_DIM, HEAD_DIM), jnp.float32)
    y_c, state_c, r_c, k_c2, v_c2 = rwkv_scan(rr_c, rk_c, rv_c, jnp.stack([wf_c, wb_c]), jnp.stack([af_c, ab_c]), state0, conv_w, w0, w2, a0, a2, k_k, k_a)
    y_l, _, r_l, k_l2, v_l2 = rwkv_scan(rr_l, rk_l, rv_l, jnp.stack([wf_l, wb_l]), jnp.stack([af_l, ab_l]), state_c, conv_w, w0, w2, a0, a2, k_k, k_a)
    rwkv_l = rwkv_readout(y_l, r_l, k_l2, v_l2, g_l, g2, r_k, ln_g, ln_b, hl.dtype)
    out_l = jnp.concatenate([fourier_mix(f_l), attn_l, rwkv_l], axis=-1) @ w_out
    if not need_ctx_out:
        return out_l, None
    attn_c = context_attention(q_c.reshape(b, n_c, ATTN_HEADS, HEAD_DIM), kc, vc, sink)
    rwkv_c = rwkv_readout(y_c, r_c, k_c2, v_c2, g_c, g2, r_k, ln_g, ln_b, hc.dtype)
    out_c = jnp.concatenate([fourier_mix(f_c), attn_c, rwkv_c], axis=-1) @ w_out
    return out_l, out_c


def setup_inputs(seed: int = 0) -> dict:
    key = jax.random.key(seed)
    ks = jax.random.split(key, 32)
    L, D, F = DEPTH, D_MODEL, FFN_DIM
    f32 = jnp.float32

    def nrm(i, shape, scale):
        return jax.random.normal(ks[i], shape, f32) * scale

    return {
        'x': nrm(0, (BATCH, SEQ, D), 1.0),
        'c': nrm(1, (BATCH, D), 1.0),
        'ctx': nrm(2, (BATCH, CTX_LEN, D), 1.0),
        'c_ctx': nrm(3, (D,), 1.0),
        'mod_w': nrm(4, (L, D, N_MOD * D), 0.5 * D ** -0.5),
        'mod_b': nrm(5, (L, N_MOD * D), 0.02),
        'norm_g': 1.0 + nrm(6, (L, 6, D), 0.05),
        'ffn1_wi': nrm(7, (L, D, 2 * F), D ** -0.5),
        'ffn1_wo': nrm(8, (L, F, D), F ** -0.5),
        'mix_w_in': nrm(9, (L, D, IN_W), D ** -0.5),
        'mix_w_out': nrm(10, (L, MIX_W, D), MIX_W ** -0.5),
        'attn_sink': nrm(11, (L, ATTN_HEADS), 0.5),
        'rwkv_conv': nrm(12, (L, CONV_W, 3 * RWKV_W), 0.2).at[:, CONV_W // 2].add(1.0),
        'rwkv_w0': jax.random.uniform(ks[13], (L, 2, RWKV_W), f32, -6.0, 1.0),
        'rwkv_w2': nrm(14, (L, 2, DECAY_LORA, RWKV_W), 0.1),
        'rwkv_a0': nrm(15, (L, 2, RWKV_W), 0.5),
        'rwkv_a2': nrm(16, (L, 2, ICLR_LORA, RWKV_W), 0.5 * ICLR_LORA ** -0.5),
        'rwkv_g2': nrm(17, (L, GATE_LORA, RWKV_W), GATE_LORA ** -0.5),
        'rwkv_k_k': 0.85 + nrm(18, (L, RWKV_W), 0.05),
        'rwkv_k_a': 1.0 + nrm(19, (L, RWKV_W), 0.05),
        'rwkv_r_k': nrm(20, (L, RWKV_HEADS, HEAD_DIM), 0.1),
        'rwkv_ln_g': 1.0 + nrm(21, (L, RWKV_W), 0.05),
        'rwkv_ln_b': nrm(22, (L, RWKV_W), 0.02),
        'ffn2_wi': nrm(23, (L, D, 2 * F), D ** -0.5),
        'ffn2_wo': nrm(24, (L, F, D), F ** -0.5),
    }


def reference(x, c, ctx, c_ctx, mod_w, mod_b, norm_g, ffn1_wi, ffn1_wo, mix_w_in, mix_w_out, attn_sink,
              rwkv_conv, rwkv_w0, rwkv_w2, rwkv_a0, rwkv_a2, rwkv_g2, rwkv_k_k, rwkv_k_a, rwkv_r_k,
              rwkv_ln_g, rwkv_ln_b, ffn2_wi, ffn2_wo):
    xl, xc = x, ctx
    for li in range(DEPTH):
        need_ctx_out = li < DEPTH - 1
        ml = ada_mod(c, mod_w[li], mod_b[li])
        mc = ada_mod(c_ctx, mod_w[li], mod_b[li])[None]
        g = norm_g[li]
        xl = ffn_half(xl, ml[:, 0:3], g[0], g[1], ffn1_wi[li], ffn1_wo[li])
        xc = ffn_half(xc, mc[:, 0:3], g[0], g[1], ffn1_wi[li], ffn1_wo[li])
        hl = rms_norm(xl, g[2]) * (1.0 + ml[:, 4][:, None]) + ml[:, 3][:, None]
        hc = rms_norm(xc, g[2]) * (1.0 + mc[:, 4][:, None]) + mc[:, 3][:, None]
        out_l, out_c = token_mixer(hl, hc, mix_w_in[li], mix_w_out[li], attn_sink[li], rwkv_conv[li],
                                   rwkv_w0[li], rwkv_w2[li], rwkv_a0[li], rwkv_a2[li], rwkv_g2[li],
                                   rwkv_k_k[li], rwkv_k_a[li], rwkv_r_k[li], rwkv_ln_g[li], rwkv_ln_b[li],
                                   need_ctx_out)
        xl = xl + ml[:, 5][:, None] * rms_norm(out_l, g[3])
        if need_ctx_out:
            xc = xc + mc[:, 5][:, None] * rms_norm(out_c, g[3])
            xc = ffn_half(xc, mc[:, 6:9], g[4], g[5], ffn2_wi[li], ffn2_wo[li])
        xl = ffn_half(xl, ml[:, 6:9], g[4], g[5], ffn2_wi[li], ffn2_wo[li])
    return xl
```

```python
import functools

import numpy as np
import jax
import jax.numpy as jnp
from jax import lax
from jax.experimental import pallas as pl
from jax.experimental.pallas import tpu as pltpu

F32 = jnp.float32
BF16 = jnp.bfloat16

HEAD_DIM = 64
GRID_W = 64
FOURIER_W = 256
FOURIER_GROUPS = 4
ATTN_HEADS = 6
ATTN_KV_HEADS = 2
ATTN_GROUP = ATTN_HEADS // ATTN_KV_HEADS
ATTN_W = ATTN_HEADS * HEAD_DIM
KV_W = ATTN_KV_HEADS * HEAD_DIM
RWKV_HEADS = 6
RWKV_W = RWKV_HEADS * HEAD_DIM
WINDOW = 128
ROPE_BASE = 10000.0
DECAY_LORA = 64
ICLR_LORA = 64
GATE_LORA = 128
CONV_W = 3
N_MOD = 9
NORM_EPS = 1e-6
GN_EPS = 64e-5

LANES = 128
SUBLANES = 8
VMEM_LIMIT = 56 * 1024 * 1024

HEAD_SLOTS = 8
JJ = HEAD_DIM // 2
J_W = 4 * LANES
I_W = SUBLANES * LANES
SCAN_T = 64
FOURIER_N1 = 64
FOURIER_DIRECT_MAX = 256


def _j_layout():
    idx = np.full((J_W,), -1, np.int64)
    for jh in range(4):
        for jl in range(8):
            for half in range(2):
                for h in range(RWKV_HEADS):
                    idx[jh * 128 + jl * 16 + half * 8 + h] = h * HEAD_DIM + half * 32 + jh * 8 + jl
    return idx


def _i_layout(dup):
    idx = np.full((I_W,), -1, np.int64)
    for isub in range(8):
        for half in range(2 if dup else 1):
            for h in range(RWKV_HEADS):
                for il in range(8):
                    idx[isub * 128 + half * 64 + h * 8 + il] = h * HEAD_DIM + isub * 8 + il
    return idx


J_IDX = _j_layout()
I_IDX = _i_layout(True)
I_IDX_ONCE = _i_layout(False)


def _take_cols(w, idx):
    g = jnp.take(w, jnp.asarray(np.maximum(idx, 0)), axis=-1)
    return g * jnp.asarray((idx >= 0).astype(np.float32))


def _take_rows(w, idx):
    g = jnp.take(w, jnp.asarray(np.maximum(idx, 0)), axis=0)
    return g * jnp.asarray((idx >= 0).astype(np.float32))[:, None]


def _expand_matrix():
    m = np.zeros((LANES, 8 * LANES), np.float32)
    for jl in range(8):
        for half in range(2):
            for h in range(HEAD_SLOTS):
                for rep in range(8):
                    m[jl * 16 + half * 8 + h, jl * 128 + half * 64 + h * 8 + rep] = 1.0
    return m


def _seg_matrices():
    hj = np.arange(LANES) % 8
    hi = (np.arange(LANES) // 8) % 8
    jj = (hj[:, None] == hj[None, :]).astype(np.float32)
    ji = (hj[:, None] == hi[None, :]).astype(np.float32)
    ii = (hi[:, None] == hi[None, :]).astype(np.float32)
    return jj, ji, ii


def _cparams(sem, vmem=VMEM_LIMIT):
    return pltpu.CompilerParams(dimension_semantics=sem, vmem_limit_bytes=vmem)


def _const_spec(shape):
    nd = len(shape)
    return pl.BlockSpec(shape, lambda *_: (0,) * nd)


def _sigmoid(x):
    return 1.0 / (1.0 + jnp.exp(-x))


def _rms(x, g):
    return x * lax.rsqrt(jnp.mean(x * x, axis=-1, keepdims=True) + NORM_EPS) * g


def _dot(a, b):
    return jnp.dot(a, b, preferred_element_type=F32)


def _split_dot(x, m, passes):
    acc = None
    rem = x
    for p in range(passes):
        piece = rem.astype(BF16)
        d = _dot(piece, m)
        acc = d if acc is None else acc + d
        if p + 1 < passes:
            rem = rem - piece.astype(F32)
    return acc


def _ada_kernel(c_ref, w_ref, b_ref, o_ref):
    c = c_ref[...]
    s = c * _sigmoid(c)
    o_ref[0] = _dot(s.astype(BF16), w_ref[0].astype(BF16)) + b_ref[0]


def _ada_mod(cond8, mod_w, mod_b):
    nl, d, nw = mod_w.shape
    tn = nw // 8
    return pl.pallas_call(
        _ada_kernel,
        out_shape=jax.ShapeDtypeStruct((nl, 8, nw), F32),
        grid=(nl, nw // tn),
        in_specs=[pl.BlockSpec((8, d), lambda l, j: (0, 0)),
                  pl.BlockSpec((1, d, tn), lambda l, j: (l, 0, j)),
                  pl.BlockSpec((1, 1, tn), lambda l, j: (l, 0, j))],
        out_specs=pl.BlockSpec((1, 8, tn), lambda l, j: (l, 0, j)),
        compiler_params=_cparams(("arbitrary", "arbitrary")),
        name="ada_mod",
    )(cond8, mod_w, mod_b.reshape(nl, 1, nw))


def _ffn_kernel(x_ref, m_ref, g_ref, wg_ref, wu_ref, wo_ref, o_ref):
    x = x_ref[0]
    m = m_ref[0]
    g = g_ref[...]
    h = _rms(x, g[0:1]) * (1.0 + m[1:2]) + m[0:1]
    hb = h.astype(BF16)
    gate = _dot(hb, wg_ref[...])
    up = _dot(hb, wu_ref[...])
    act = (gate * _sigmoid(gate) * up).astype(BF16)
    y = _dot(act, wo_ref[...])
    o_ref[0] = x + 0.5 * m[2:3] * _rms(y, g[1:2])


def _ffn_half(x, m3, g2, wg, wu, wo, tm):
    b, n, d = x.shape
    fp = wg.shape[1]
    return pl.pallas_call(
        _ffn_kernel,
        out_shape=jax.ShapeDtypeStruct(x.shape, F32),
        grid=(b, n // tm),
        in_specs=[pl.BlockSpec((1, tm, d), lambda i, j: (i, j, 0)),
                  pl.BlockSpec((1, 3, d), lambda i, j: (i, 0, 0)),
                  _const_spec((2, d)),
                  _const_spec((d, fp)), _const_spec((d, fp)), _const_spec((fp, d))],
        out_specs=pl.BlockSpec((1, tm, d), lambda i, j: (i, j, 0)),
        compiler_params=_cparams(("arbitrary", "arbitrary")),
        name="ffn_half",
    )(x, m3, g2, wg, wu, wo)


MIX_COLS = (("f", FOURIER_W), ("q", ATTN_W), ("qr", ATTN_W), ("k", 2 * KV_W), ("kr", 2 * KV_W),
            ("v", 2 * KV_W), ("r", J_W), ("rk", J_W), ("rv", I_W), ("lora", 4 * DECAY_LORA), ("g", GATE_LORA))
MIX_OFF = {}
_o = 0
for _n, _w in MIX_COLS:
    MIX_OFF[_n] = (_o, _o + _w)
    _o += _w
MIX_TOTAL = _o


def _mix_in_kernel(x_ref, m_ref, g_ref, w_ref, cos_ref, sin_ref,
                   f_ref, q_ref, k_ref, v_ref, r_ref, rk_ref, rv_ref, lora_ref, zg_ref):
    x = x_ref[0]
    m = m_ref[0]
    h = _rms(x, g_ref[...]) * (1.0 + m[1:2]) + m[0:1]
    z = _dot(h.astype(BF16), w_ref[...])

    def col(name):
        lo, hi = MIX_OFF[name]
        return z[:, lo:hi]

    cos = cos_ref[...]
    sin = sin_ref[...]
    cos3 = jnp.concatenate([cos, cos, cos], axis=1)
    sin3 = jnp.concatenate([sin, sin, sin], axis=1)
    cos2 = jnp.concatenate([cos, cos], axis=1)
    sin2 = jnp.concatenate([sin, sin], axis=1)
    f_ref[0] = col("f")
    q_ref[0] = ((col("q") * cos3 + col("qr") * sin3) * (HEAD_DIM ** -0.5)).astype(BF16)
    k_ref[0] = (col("k") * cos2 + col("kr") * sin2).astype(BF16)
    v_ref[0] = col("v").astype(BF16)
    r_ref[0] = col("r")
    rk_ref[0] = col("rk")
    rv_ref[0] = col("rv")
    lora_ref[0] = col("lora")
    zg_ref[0] = col("g")


def _mix_in(x, m3, g1, w_all, cos, sin, tm):
    b, n, d = x.shape
    widths = (FOURIER_W, ATTN_W, 2 * KV_W, 2 * KV_W, J_W, J_W, I_W, 4 * DECAY_LORA, GATE_LORA)
    dtypes = (F32, BF16, BF16, BF16, F32, F32, F32, F32, F32)
    return pl.pallas_call(
        _mix_in_kernel,
        out_shape=[jax.ShapeDtypeStruct((b, n, w), dt) for w, dt in zip(widths, dtypes)],
        grid=(b, n // tm),
        in_specs=[pl.BlockSpec((1, tm, d), lambda i, j: (i, j, 0)),
                  pl.BlockSpec((1, 3, d), lambda i, j: (i, 0, 0)),
                  _const_spec((1, d)),
                  _const_spec((d, MIX_TOTAL)),
                  pl.BlockSpec((tm, LANES), lambda i, j: (j, 0)),
                  pl.BlockSpec((tm, LANES), lambda i, j: (j, 0))],
        out_specs=[pl.BlockSpec((1, tm, w), lambda i, j: (i, j, 0)) for w in widths],
        compiler_params=_cparams(("arbitrary", "arbitrary")),
        name="mix_in",
    )(x, m3, g1, w_all, cos, sin)


def _fourier_stage1_kernel(l_ref, x_ref, o_ref):
    o_ref[0] = _dot(l_ref[...], x_ref[0].astype(BF16))


def _fourier_stage2_kernel(a_ref, mc_ref, ms_ref, cc_ref, sc_ref, o_ref, *, scale):
    ar = a_ref[0, 0, 0].astype(BF16)
    ai = a_ref[0, 1, 0].astype(BF16)
    mc = mc_ref[0]
    ms = ms_ref[0]
    gr = _dot(mc, ar) + _dot(ms, ai)
    gi = _dot(mc, ai) - _dot(ms, ar)
    o_ref[0] = (_dot(gr.astype(BF16), cc_ref[...]) + _dot(gi.astype(BF16), sc_ref[...])) * scale


def _fourier_small_kernel(z_ref, cn_ref, sn_ref, cc_ref, sc_ref, o_ref, *, scale):
    z = z_ref[0].astype(BF16)
    t1 = _dot(z, cc_ref[...]).astype(BF16)
    t2 = _dot(z, sc_ref[...]).astype(BF16)
    o_ref[0] = (_dot(cn_ref[...], t1) - _dot(sn_ref[...], t2)) * scale


def _channel_dft():
    gw = FOURIER_W // FOURIER_GROUPS
    c = np.arange(FOURIER_W)
    same = (c[:, None] // gw) == (c[None, :] // gw)
    ang = 2.0 * np.pi * ((c[:, None] % gw) * (c[None, :] % gw) % gw) / gw
    return (np.cos(ang) * same).astype(np.float32), (np.sin(ang) * same).astype(np.float32)


def _fourier_mix(z):
    b, n, w = z.shape
    gw = w // FOURIER_GROUPS
    scale = float(1.0 / np.sqrt(n * gw))
    cc, sc = _channel_dft()
    cc = jnp.asarray(cc, BF16)
    sc = jnp.asarray(sc, BF16)
    if n <= FOURIER_DIRECT_MAX:
        p = np.arange(n)
        ang = 2.0 * np.pi * ((p[:, None] * p[None, :]) % n) / n
        return pl.pallas_call(
            functools.partial(_fourier_small_kernel, scale=scale),
            out_shape=jax.ShapeDtypeStruct((b, n, w), F32),
            grid=(b,),
            in_specs=[pl.BlockSpec((1, n, w), lambda i: (i, 0, 0)),
                      _const_spec((n, n)), _const_spec((n, n)), _const_spec((w, w)), _const_spec((w, w))],
            out_specs=pl.BlockSpec((1, n, w), lambda i: (i, 0, 0)),
            compiler_params=_cparams(("arbitrary",)),
            name="fourier_small",
        )(z, jnp.asarray(np.cos(ang), BF16), jnp.asarray(np.sin(ang), BF16), cc, sc)
    n1 = FOURIER_N1
    n2 = n // n1
    k1 = np.arange(n1)
    ang1 = 2.0 * np.pi * ((k1[:, None] * k1[None, :]) % n1) / n1
    lhs1 = jnp.asarray(np.concatenate([np.cos(ang1), -np.sin(ang1)], axis=0), BF16)
    tc = min(n2 * w, 4096)
    a = pl.pallas_call(
        _fourier_stage1_kernel,
        out_shape=jax.ShapeDtypeStruct((b, 2 * n1, n2 * w), F32),
        grid=(b, (n2 * w) // tc),
        in_specs=[_const_spec((2 * n1, n1)),
                  pl.BlockSpec((1, n1, tc), lambda i, j: (i, 0, j))],
        out_specs=pl.BlockSpec((1, 2 * n1, tc), lambda i, j: (i, 0, j)),
        compiler_params=_cparams(("arbitrary", "arbitrary")),
        name="fourier_stage1",
    )(lhs1, z.reshape(b, n1, n2 * w))
    k2 = np.arange(n2)
    freq = (k1[:, None, None] + n1 * k2[None, :, None]) * k2[None, None, :]
    ang2 = 2.0 * np.pi * (freq % n) / n
    out = pl.pallas_call(
        functools.partial(_fourier_stage2_kernel, scale=scale),
        out_shape=jax.ShapeDtypeStruct((b, n2, n1 * w), F32),
        grid=(b, n1),
        in_specs=[pl.BlockSpec((1, 2, 1, n2, w), lambda i, j: (i, 0, j, 0, 0)),
                  pl.BlockSpec((1, n2, n2), lambda i, j: (j, 0, 0)),
                  pl.BlockSpec((1, n2, n2), lambda i, j: (j, 0, 0)),
                  _const_spec((w, w)), _const_spec((w, w))],
        out_specs=pl.BlockSpec((1, n2, w), lambda i, j: (i, 0, j)),
        compiler_params=_cparams(("arbitrary", "arbitrary")),
        name="fourier_stage2",
    )(a.reshape(b, 2, n1, n2, w), jnp.asarray(np.cos(ang2), BF16), jnp.asarray(np.sin(ang2), BF16), cc, sc)
    return out.reshape(b, n, w)


NEG = -1e30


def _attn_kernel(*refs, window):
    if window:
        (sink_ref, q_ref, kp_ref, kc_ref, kn_ref, vp_ref, vc_ref, vn_ref, kx_ref, vx_ref, o_ref) = refs
    else:
        (sink_ref, q_ref, kx_ref, vx_ref, o_ref) = refs
    i = pl.program_id(1)
    nb = pl.num_programs(1)
    tq = q_ref.shape[1]
    lane = lax.broadcasted_iota(jnp.int32, (1, KV_W), 1)
    row = lax.broadcasted_iota(jnp.int32, (tq, WINDOW), 0)
    col = lax.broadcasted_iota(jnp.int32, (tq, WINDOW), 1)
    segs = []
    if window:
        prev_ok = jnp.logical_and(col >= row, i > 0)
        next_ok = jnp.logical_and(col <= row, i < nb - 1)
        segs.append((kp_ref[0], vp_ref[0], jnp.where(prev_ok, 0.0, NEG)))
        segs.append((kc_ref[0], vc_ref[0], None))
        segs.append((kn_ref[0], vn_ref[0], jnp.where(next_ok, 0.0, NEG)))
    segs.append((kx_ref[0], vx_ref[0], None))
    zero = jnp.zeros((), BF16)
    for pair in range(ATTN_HEADS // 2):
        qp = q_ref[0, :, pair * LANES:(pair + 1) * LANES]
        outs = []
        for slot in range(2):
            h = 2 * pair + slot
            kv = h // ATTN_GROUP
            swap = slot != kv
            keep = (lane >= HEAD_DIM) if slot else (lane < HEAD_DIM)
            s_list = []
            for kk, vv, msk in segs:
                kh = kk[:, KV_W:] if swap else kk[:, :KV_W]
                kh = jnp.where(keep, kh, zero)
                s = lax.dot_general(qp, kh, (((1,), (1,)), ((), ())), preferred_element_type=F32)
                s_list.append(s if msk is None else s + msk)
            sk = sink_ref[h]
            mx = s_list[0].max(axis=-1, keepdims=True)
            for s in s_list[1:]:
                mx = jnp.maximum(mx, s.max(axis=-1, keepdims=True))
            mx = jnp.maximum(mx, sk)
            den = jnp.exp(sk - mx)
            acc = None
            for s, (kk, vv, msk) in zip(s_list, segs):
                e = jnp.exp(s - mx)
                den = den + e.sum(axis=-1, keepdims=True)
                vh = vv[:, KV_W:] if swap else vv[:, :KV_W]
                pv = _dot(e.astype(BF16), vh)
                acc = pv if acc is None else acc + pv
            outs.append(acc / den)
        o_ref[0, :, pair * LANES:(pair + 1) * LANES] = jnp.where(lane < HEAD_DIM, outs[0], outs[1])


def _attention(q, k, v, kx, vx, sink8, window):
    b, n, _ = q.shape
    c = kx.shape[1]
    tq = WINDOW
    nb = n // tq
    smem = pl.BlockSpec(memory_space=pltpu.SMEM)
    qspec = pl.BlockSpec((1, tq, ATTN_W), lambda i, j: (i, j, 0))
    xspec = pl.BlockSpec((1, c, 2 * KV_W), lambda i, j: (i, 0, 0))
    if window:
        prev = pl.BlockSpec((1, tq, 2 * KV_W), lambda i, j: (i, jnp.maximum(j - 1, 0), 0))
        cur = pl.BlockSpec((1, tq, 2 * KV_W), lambda i, j: (i, j, 0))
        nxt = pl.BlockSpec((1, tq, 2 * KV_W), lambda i, j: (i, jnp.minimum(j + 1, nb - 1), 0))
        in_specs = [smem, qspec, prev, cur, nxt, prev, cur, nxt, xspec, xspec]
        args = (sink8, q, k, k, k, v, v, v, kx, vx)
    else:
        in_specs = [smem, qspec, xspec, xspec]
        args = (sink8, q, kx, vx)
    return pl.pallas_call(
        functools.partial(_attn_kernel, window=window),
        out_shape=jax.ShapeDtypeStruct((b, n, ATTN_W), F32),
        grid=(b, nb),
        in_specs=in_specs,
        out_specs=pl.BlockSpec((1, tq, ATTN_W), lambda i, j: (i, j, 0)),
        compiler_params=_cparams(("arbitrary", "arbitrary")),
        name="window_attention" if window else "context_attention",
    )(*args)


def _shift_rows(x, prev_row, next_row):
    n = x.shape[0]
    row = lax.broadcasted_iota(jnp.int32, (n, 1), 0)
    xp = jnp.where(row == 0, prev_row, pltpu.roll(x, 1, 0))
    xn = jnp.where(row == n - 1, next_row, pltpu.roll(x, n - 1, 0))
    return xp, xn


def _rwkv_prep_kernel(r_ref, rp_ref, rn_ref, k_ref, kp_ref, kn_ref, v_ref, vp_ref, vn_ref, lora_ref,
                      cwr_ref, cwk_ref, cwv_ref, w0_ref, w2_ref, a0_ref, a2_ref, vec_ref, sjj_ref, sji_ref,
                      rc_ref, nkk_ref, vc_ref, bonus_ref, dec_ref, kd_ref, bb_ref):
    i = pl.program_id(1)
    first = (i > 0).astype(F32)
    last = (i < pl.num_programs(1) - 1).astype(F32)

    def conv(x_ref, p_ref, n_ref, cw_ref):
        x = x_ref[0]
        xp, xn = _shift_rows(x, p_ref[0, SUBLANES - 1:SUBLANES, :] * first, n_ref[0, 0:1, :] * last)
        cw = cw_ref[...]
        return xp * cw[0:1] + x * cw[1:2] + xn * cw[2:3]

    r = conv(r_ref, rp_ref, rn_ref, cwr_ref)
    k = conv(k_ref, kp_ref, kn_ref, cwk_ref)
    v = conv(v_ref, vp_ref, vn_ref, cwv_ref)
    vec = vec_ref[...]
    sjj = sjj_ref[...]
    sji = sji_ref[...]

    def fold(x):
        return (x[:, 0:LANES] + x[:, LANES:2 * LANES]) + (x[:, 2 * LANES:3 * LANES] + x[:, 3 * LANES:4 * LANES])

    kk = k * vec[0:1]
    ss = _split_dot(fold(kk * kk), sjj, 2)
    inv = lax.rsqrt(jnp.maximum(ss, 1e-24))
    kk = kk * jnp.concatenate([inv, inv, inv, inv], axis=1)
    rk = _split_dot(fold(r * k * vec[2:3]), sji, 2)
    rc_ref[0] = r
    nkk_ref[0] = -kk
    vc_ref[0] = v
    bonus_ref[0] = v * jnp.concatenate([rk] * SUBLANES, axis=1)
    lora = lora_ref[0]
    zw = jnp.tanh(lora[:, :2 * DECAY_LORA])
    za = lora[:, 2 * DECAY_LORA:]
    for d in range(2):
        wpre = w0_ref[d:d + 1, :] + jnp.dot(zw, w2_ref[d], preferred_element_type=F32,
                                            precision=lax.Precision.HIGHEST)
        apre = a0_ref[d:d + 1, :] + jnp.dot(za, a2_ref[d], preferred_element_type=F32,
                                            precision=lax.Precision.HIGHEST)
        nw = -wpre
        softplus = jnp.maximum(nw, 0.0) + jnp.log(1.0 + jnp.exp(-jnp.abs(nw)))
        w = -softplus - 0.5
        a = _sigmoid(apre)
        dec_ref[d, 0] = jnp.exp(-jnp.exp(w))
        kd_ref[d, 0] = k * (1.0 + (a - 1.0) * vec[1:2])
        bb_ref[d, 0] = kk * a


def _rwkv_prep(r, k, v, lora, cwr, cwk, cwv, w0, w2, a0, a2, vec, sjj, sji, tm):
    b, n, _ = r.shape
    nh = n // SUBLANES

    def tile(w):
        return pl.BlockSpec((1, tm, w), lambda i, j: (i, j, 0))

    def halo_prev(w):
        return pl.BlockSpec((1, SUBLANES, w), lambda i, j: (i, jnp.maximum(j * (tm // SUBLANES) - 1, 0), 0))

    def halo_next(w):
        return pl.BlockSpec((1, SUBLANES, w), lambda i, j: (i, jnp.minimum((j + 1) * (tm // SUBLANES), nh - 1), 0))

    def dir_tile(w):
        return pl.BlockSpec((2, 1, tm, w), lambda i, j: (0, i, j, 0))

    out_shape = [jax.ShapeDtypeStruct((b, n, J_W), F32), jax.ShapeDtypeStruct((b, n, J_W), F32),
                 jax.ShapeDtypeStruct((b, n, I_W), F32), jax.ShapeDtypeStruct((b, n, I_W), F32),
                 jax.ShapeDtypeStruct((2, b, n, J_W), F32), jax.ShapeDtypeStruct((2, b, n, J_W), F32),
                 jax.ShapeDtypeStruct((2, b, n, J_W), F32)]
    return pl.pallas_call(
        _rwkv_prep_kernel,
        out_shape=out_shape,
        grid=(b, n // tm),
        in_specs=[tile(J_W), halo_prev(J_W), halo_next(J_W),
                  tile(J_W), halo_prev(J_W), halo_next(J_W),
                  tile(I_W), halo_prev(I_W), halo_next(I_W),
                  tile(4 * DECAY_LORA),
                  _const_spec((CONV_W, J_W)), _const_spec((CONV_W, J_W)), _const_spec((CONV_W, I_W)),
                  _const_spec((2, J_W)), _const_spec((2, 2 * DECAY_LORA, J_W)),
                  _const_spec((2, J_W)), _const_spec((2, 2 * ICLR_LORA, J_W)),
                  _const_spec((3, J_W)), _const_spec((LANES, LANES)), _const_spec((LANES, LANES))],
        out_specs=[tile(J_W), tile(J_W), tile(I_W), tile(I_W), dir_tile(J_W), dir_tile(J_W), dir_tile(J_W)],
        compiler_params=_cparams(("arbitrary", "arbitrary")),
        name="rwkv_prep",
    )(r, r, r, k, k, k, v, v, v, lora, cwr, cwk, cwv, w0, w2, a0, a2, vec, sjj, sji)


N_ROWVEC = 5
N_DIR_REFS = N_ROWVEC + 1


def _scan_kernel(*refs, t_chunk, n_batch):
    ins = refs[:2 * N_DIR_REFS]
    rexp_ref, s0_ref = refs[2 * N_DIR_REFS:2 * N_DIR_REFS + 2]
    y_refs = refs[2 * N_DIR_REFS + 2:2 * N_DIR_REFS + 4]
    sout_ref = refs[2 * N_DIR_REFS + 4]
    e_ref, s_ref = refs[2 * N_DIR_REFS + 5:]
    step_id = pl.program_id(0)

    @pl.when(step_id == 0)
    def _():
        s_ref[...] = s0_ref[...]

    rexp = rexp_ref[...]
    for d in range(2):
        for bi in range(n_batch):
            for x in range(N_ROWVEC):
                ref = ins[d * N_DIR_REFS + x]
                val = ref[bi] if len(ref.shape) == 3 else ref[0, bi]
                e_ref[d * n_batch + bi, x] = _split_dot(val, rexp, 3).reshape(t_chunk // 2, SUBLANES, 8 * LANES)

    def step(s2, par, d, bi):
        t2, tp = (s2, par) if d == 0 else (t_chunk // 2 - 1 - s2, 1 - par)
        t = 2 * t2 + tp
        g = d * n_batch + bi
        v = ins[d * N_DIR_REFS + N_ROWVEC][bi, t]

        def row(x, jj):
            r0 = tp * 4 + jj // 8
            return e_ref[g, x, t2, r0:r0 + 1, (jj % 8) * LANES:(jj % 8 + 1) * LANES]

        parts = [None] * 4
        for jj in range(JJ):
            term = s_ref[g, jj] * row(0, jj)
            parts[jj % 4] = term if parts[jj % 4] is None else parts[jj % 4] + term
        p = (parts[0] + parts[1]) + (parts[2] + parts[3])
        sa = p + pltpu.roll(p, LANES // 2, 1)
        parts = [None] * 4
        for jj in range(JJ):
            new = s_ref[g, jj] * row(1, jj) + sa * row(2, jj) + v * row(3, jj)
            s_ref[g, jj] = new
            term = new * row(4, jj)
            parts[jj % 4] = term if parts[jj % 4] is None else parts[jj % 4] + term
        q = (parts[0] + parts[1]) + (parts[2] + parts[3])
        y_refs[d][bi, t] = q + pltpu.roll(q, LANES // 2, 1)

    def pair_of_steps(s2, carry):
        for par in range(2):
            for d in range(2):
                for bi in range(n_batch):
                    step(s2, par, d, bi)
        return carry

    lax.fori_loop(0, t_chunk // 2, pair_of_steps, 0)

    @pl.when(step_id == pl.num_programs(0) - 1)
    def _():
        sout_ref[...] = s_ref[...]


def _rwkv_scan(rc, nkk, vc, dec, kd, bb, state0, rexp):
    b, n, _ = rc.shape
    t = min(SCAN_T, n)
    nc = n // t
    rc4 = rc.reshape(b, n * 4, LANES)
    nkk4 = nkk.reshape(b, n * 4, LANES)
    vc5 = vc.reshape(b, n, SUBLANES, LANES)
    dec4 = dec.reshape(2, b, n * 4, LANES)
    kd4 = kd.reshape(2, b, n * 4, LANES)
    bb4 = bb.reshape(2, b, n * 4, LANES)
    in_specs = []
    args = []
    out_specs = []
    for d in range(2):
        def chunk(i, d=d):
            return i if d == 0 else nc - 1 - i

        shared = pl.BlockSpec((b, t * 4, LANES), lambda i, chunk=chunk: (0, chunk(i), 0))
        perdir = pl.BlockSpec((1, b, t * 4, LANES), lambda i, d=d, chunk=chunk: (d, 0, chunk(i), 0))
        vspec = pl.BlockSpec((b, t, SUBLANES, LANES), lambda i, chunk=chunk: (0, chunk(i), 0, 0))
        in_specs += [shared, perdir, perdir, perdir, shared, vspec]
        args += [nkk4, dec4, bb4, kd4, rc4, vc5]
        out_specs.append(pl.BlockSpec((b, t, SUBLANES, LANES), lambda i, chunk=chunk: (0, chunk(i), 0, 0)))
    state_shape = (2 * b, JJ, SUBLANES, LANES)
    in_specs += [_const_spec((LANES, 8 * LANES)), _const_spec(state_shape)]
    args += [rexp, state0]
    out_specs.append(_const_spec(state_shape))
    y_shape = jax.ShapeDtypeStruct((b, n, SUBLANES, LANES), F32)
    y_f, y_b, state = pl.pallas_call(
        functools.partial(_scan_kernel, t_chunk=t, n_batch=b),
        out_shape=[y_shape, y_shape, jax.ShapeDtypeStruct(state_shape, F32)],
        grid=(nc,),
        in_specs=in_specs,
        out_specs=out_specs,
        scratch_shapes=[pltpu.VMEM((2 * b, N_ROWVEC, t // 2, SUBLANES, 8 * LANES), F32),
                        pltpu.VMEM(state_shape, F32)],
        compiler_params=_cparams(("arbitrary",)),
        name="rwkv_scan",
    )(*args)
    return y_f.reshape(b, n, I_W), y_b.reshape(b, n, I_W), state


def _mix_out_kernel(x_ref, m_ref, g_ref, f_ref, a_ref, yf_ref, yb_ref, bonus_ref, zg_ref,
                    g2_ref, ln_ref, sii_ref, wf_ref, wa_ref, wr_ref, o_ref):
    x = x_ref[0]
    y = yf_ref[0] + yb_ref[0]
    sii = sii_ref[...]
    nblk = I_W // LANES
    cnt = float(2 * HEAD_DIM)

    def fold(z):
        acc = z[:, 0:LANES]
        for c in range(1, nblk):
            acc = acc + z[:, c * LANES:(c + 1) * LANES]
        return acc

    def spread(z):
        return jnp.concatenate([z] * nblk, axis=1)

    mu = _split_dot(fold(y), sii, 2) * (1.0 / cnt)
    dlt = y - spread(mu)
    var = _split_dot(fold(dlt * dlt), sii, 2) * (1.0 / cnt)
    ln = ln_ref[...]
    yn = dlt * spread(lax.rsqrt(var + GN_EPS)) * ln[0:1] + ln[1:2]
    gate = _dot(_sigmoid(zg_ref[0]).astype(BF16), g2_ref[...])
    rw = ((yn + bonus_ref[0]) * gate).astype(BF16)
    o = _dot(f_ref[0].astype(BF16), wf_ref[...]) + _dot(a_ref[0].astype(BF16), wa_ref[...]) + _dot(rw, wr_ref[...])
    o_ref[0] = x + m_ref[0] * _rms(o, g_ref[...])


def _mix_out(x, gate_row, g3, fo, ao, y_f, y_b, bonus, zg, g2p, ln, sii, wf, wa, wr, tm):
    b, n, d = x.shape

    def tile(w):
        return pl.BlockSpec((1, tm, w), lambda i, j: (i, j, 0))

    return pl.pallas_call(
        _mix_out_kernel,
        out_shape=jax.ShapeDtypeStruct(x.shape, F32),
        grid=(b, n // tm),
        in_specs=[tile(d), pl.BlockSpec((1, 1, d), lambda i, j: (i, 0, 0)), _const_spec((1, d)),
                  tile(FOURIER_W), tile(ATTN_W),
                  tile(I_W), tile(I_W), tile(I_W), tile(GATE_LORA),
                  _const_spec((GATE_LORA, I_W)), _const_spec((2, I_W)), _const_spec((LANES, LANES)),
                  _const_spec((FOURIER_W, d)), _const_spec((ATTN_W, d)), _const_spec((I_W, d))],
        out_specs=tile(d),
        compiler_params=_cparams(("arbitrary", "arbitrary")),
        name="mix_out",
    )(x, gate_row, g3, fo, ao, y_f, y_b, bonus, zg, g2p, ln, sii, wf, wa, wr)


def _rope_tables(n_tokens):
    rows_n = n_tokens // GRID_W
    row = jnp.repeat(jnp.arange(rows_n), GRID_W).astype(F32)
    colp = jnp.tile(jnp.arange(GRID_W), rows_n).astype(F32)
    n_freq = HEAD_DIM // 4
    inv = ROPE_BASE ** (-jnp.arange(n_freq, dtype=F32) / n_freq)
    ang = jnp.concatenate([row[:, None] * inv, colp[:, None] * inv], axis=-1)
    cos = jnp.tile(jnp.cos(ang), (1, 4))
    sin = jnp.tile(jnp.sin(ang), (1, 4))
    return cos, sin


def _rot_cols(w):
    d, n = w.shape
    w4 = w.reshape(d, n // HEAD_DIM, 2, HEAD_DIM // 2)
    return jnp.stack([-w4[:, :, 1], w4[:, :, 0]], axis=2).reshape(d, n)


def _layer_weights(li, p):
    f = p["ffn1_wo"].shape[1]
    fp = -(-f // (2 * LANES)) * (2 * LANES)
    out = {}
    for name in ("ffn1", "ffn2"):
        wi = p[name + "_wi"][li]
        wo = p[name + "_wo"][li]
        out[name] = (jnp.pad(wi[:, :f], ((0, 0), (0, fp - f))).astype(BF16),
                     jnp.pad(wi[:, f:], ((0, 0), (0, fp - f))).astype(BF16),
                     jnp.pad(wo, ((0, fp - f), (0, 0))).astype(BF16))
    w_in = p["mix_w_in"][li]
    offs = np.cumsum([0, FOURIER_W, ATTN_W, KV_W, KV_W, RWKV_W, RWKV_W, RWKV_W,
                      DECAY_LORA, DECAY_LORA, ICLR_LORA, ICLR_LORA, GATE_LORA])
    part = [w_in[:, offs[i]:offs[i + 1]] for i in range(12)]
    wf, wq, wk, wv, wrr, wrk, wrv = part[:7]
    swap = lambda w: jnp.concatenate([w[:, HEAD_DIM:], w[:, :HEAD_DIM]], axis=1)
    cols = {"f": wf, "q": wq, "qr": _rot_cols(wq),
            "k": jnp.concatenate([wk, swap(wk)], axis=1),
            "kr": jnp.concatenate([_rot_cols(wk), swap(_rot_cols(wk))], axis=1),
            "v": jnp.concatenate([wv, swap(wv)], axis=1),
            "r": _take_cols(wrr, J_IDX), "rk": _take_cols(wrk, J_IDX), "rv": _take_cols(wrv, I_IDX),
            "lora": jnp.concatenate(part[7:11], axis=1), "g": part[11]}
    out["w_all"] = jnp.concatenate([cols[n] for n, _ in MIX_COLS], axis=1).astype(BF16)
    conv = p["rwkv_conv"][li]
    out["cwr"] = _take_cols(conv[:, :RWKV_W], J_IDX)
    out["cwk"] = _take_cols(conv[:, RWKV_W:2 * RWKV_W], J_IDX)
    out["cwv"] = _take_cols(conv[:, 2 * RWKV_W:], I_IDX)
    def per_direction(w2):
        w2 = _take_cols(w2, J_IDX)
        z = jnp.zeros_like(w2[0])
        return jnp.stack([jnp.concatenate([w2[0], z], axis=0), jnp.concatenate([z, w2[1]], axis=0)])

    out["w0"] = _take_cols(p["rwkv_w0"][li], J_IDX)
    out["w2"] = per_direction(p["rwkv_w2"][li])
    out["a0"] = _take_cols(p["rwkv_a0"][li], J_IDX)
    out["a2"] = per_direction(p["rwkv_a2"][li])
    out["vec"] = _take_cols(jnp.stack([p["rwkv_k_k"][li], p["rwkv_k_a"][li], p["rwkv_r_k"][li].reshape(-1)]), J_IDX)
    out["g2"] = _take_cols(p["rwkv_g2"][li], I_IDX).astype(BF16)
    out["ln"] = _take_cols(jnp.stack([p["rwkv_ln_g"][li], p["rwkv_ln_b"][li]]), I_IDX)
    w_out = p["mix_w_out"][li]
    out["wo_f"] = w_out[:FOURIER_W].astype(BF16)
    out["wo_a"] = w_out[FOURIER_W:FOURIER_W + ATTN_W].astype(BF16)
    out["wo_r"] = _take_rows(w_out[FOURIER_W + ATTN_W:], I_IDX_ONCE).astype(BF16)
    sink = p["attn_sink"][li]
    out["sink"] = jnp.concatenate([sink, jnp.zeros((8 - ATTN_HEADS,), F32)])
    return out


def kernel(x, c, ctx, c_ctx, mod_w, mod_b, norm_g, ffn1_wi, ffn1_wo, mix_w_in, mix_w_out, attn_sink,
           rwkv_conv, rwkv_w0, rwkv_w2, rwkv_a0, rwkv_a2, rwkv_g2, rwkv_k_k, rwkv_k_a, rwkv_r_k,
           rwkv_ln_g, rwkv_ln_b, ffn2_wi, ffn2_wo):
    p = dict(ffn1_wi=ffn1_wi, ffn1_wo=ffn1_wo, ffn2_wi=ffn2_wi, ffn2_wo=ffn2_wo, mix_w_in=mix_w_in,
             mix_w_out=mix_w_out, attn_sink=attn_sink, rwkv_conv=rwkv_conv, rwkv_w0=rwkv_w0, rwkv_w2=rwkv_w2,
             rwkv_a0=rwkv_a0, rwkv_a2=rwkv_a2, rwkv_g2=rwkv_g2, rwkv_k_k=rwkv_k_k, rwkv_k_a=rwkv_k_a,
             rwkv_r_k=rwkv_r_k, rwkv_ln_g=rwkv_ln_g, rwkv_ln_b=rwkv_ln_b)
    b, s, d = x.shape
    n_c = ctx.shape[1]
    depth = mod_w.shape[0]
    assert b + 1 <= 8 and s % WINDOW == 0 and n_c % WINDOW == 0
    tm_l = 256
    tm_c = min(256, n_c)

    cond8 = jnp.zeros((8, d), F32).at[:b].set(c).at[b].set(c_ctx)
    mod = _ada_mod(cond8, mod_w, mod_b).reshape(depth, 8, N_MOD, d)
    cos_l, sin_l = _rope_tables(s)
    cos_c = jnp.ones((n_c, LANES), F32)
    sin_c = jnp.zeros((n_c, LANES), F32)
    rexp = jnp.asarray(_expand_matrix(), BF16)
    sjj, sji, sii = (jnp.asarray(m, BF16) for m in _seg_matrices())
    zero_state = jnp.zeros((2 * b, JJ, SUBLANES, LANES), F32)

    xl, xc = x, ctx
    for li in range(depth):
        need_ctx_out = li < depth - 1
        w = _layer_weights(li, p)
        ml = mod[li, :b]
        mc = jnp.broadcast_to(mod[li, b:b + 1], (b, N_MOD, d))
        g = norm_g[li]
        xl = _ffn_half(xl, ml[:, 0:3], g[0:2], *w["ffn1"], tm_l)
        xc = _ffn_half(xc, mc[:, 0:3], g[0:2], *w["ffn1"], tm_c)

        def mixer_in(xx, mm, cos, sin, tm):
            return _mix_in(xx, mm[:, 3:6], g[2:3], w["w_all"], cos, sin, tm)

        fl, ql, kl, vl, rl, rkl, rvl, loral, zgl = mixer_in(xl, ml, cos_l, sin_l, tm_l)
        fc, qc, kc, vc, rc_, rkc, rvc, lorac, zgc = mixer_in(xc, mc, cos_c, sin_c, tm_c)

        attn_l = _attention(ql, kl, vl, kc, vc, w["sink"], True)

        def prep(r_, k_, v_, lora_, tm):
            return _rwkv_prep(r_, k_, v_, lora_, w["cwr"], w["cwk"], w["cwv"], w["w0"], w["w2"],
                              w["a0"], w["a2"], w["vec"], sjj, sji, tm)

        pc = prep(rc_, rkc, rvc, lorac, tm_c)
        plat = prep(rl, rkl, rvl, loral, tm_l)
        ycf, ycb, state_c = _rwkv_scan(pc[0], pc[1], pc[2], pc[4], pc[5], pc[6], zero_state, rexp)
        ylf, ylb, _ = _rwkv_scan(plat[0], plat[1], plat[2], plat[4], plat[5], plat[6], state_c, rexp)

        def mixer_out(xx, mm, fo, ao, y_f, y_b, bonus, zg, tm):
            return _mix_out(xx, mm[:, 5:6], g[3:4], fo, ao, y_f, y_b, bonus, zg, w["g2"], w["ln"], sii,
                            w["wo_f"], w["wo_a"], w["wo_r"], tm)

        xl = mixer_out(xl, ml, _fourier_mix(fl), attn_l, ylf, ylb, plat[3], zgl, tm_l)
        if need_ctx_out:
            attn_c = _attention(qc, None, None, kc, vc, w["sink"], False)
            xc = mixer_out(xc, mc, _fourier_mix(fc), attn_c, ycf, ycb, pc[3], zgc, tm_c)
            xc = _ffn_half(xc, mc[:, 6:9], g[4:6], *w["ffn2"], tm_c)
        xl = _ffn_half(xl, ml[:, 6:9], g[4:6], *w["ffn2"], tm_l)
    return xl
```

```python
import functools

import numpy as np
import jax
import jax.numpy as jnp
from jax import lax
from jax.experimental import pallas as pl
from jax.experimental.pallas import tpu as pltpu

F32 = jnp.float32
BF16 = jnp.bfloat16

HEAD_DIM = 64
GRID_W = 64
FOURIER_W = 256
FOURIER_GROUPS = 4
ATTN_HEADS = 6
ATTN_KV_HEADS = 2
ATTN_GROUP = ATTN_HEADS // ATTN_KV_HEADS
ATTN_W = ATTN_HEADS * HEAD_DIM
KV_W = ATTN_KV_HEADS * HEAD_DIM
RWKV_HEADS = 6
RWKV_W = RWKV_HEADS * HEAD_DIM
WINDOW = 128
ROPE_BASE = 10000.0
DECAY_LORA = 64
ICLR_LORA = 64
GATE_LORA = 128
CONV_W = 3
N_MOD = 9
NORM_EPS = 1e-6
GN_EPS = 64e-5

LANES = 128
SUBLANES = 8
VMEM_LIMIT = 56 * 1024 * 1024

HEAD_SLOTS = 8
J_PARTS = 4
I_REP = 4
JJ = HEAD_DIM // J_PARTS
I_HI = HEAD_DIM // (SUBLANES * I_REP)
J_W = 4 * LANES
E_W = 4 * LANES
I_W = SUBLANES * LANES
SCAN_T = 64
SCAN_UNROLL = 4
FOURIER_N1 = 64
FOURIER_DIRECT_MAX = 256


def _j_layout():
    idx = np.full((J_W,), -1, np.int64)
    for jh in range(4):
        for jl in range(4):
            for part in range(J_PARTS):
                for h in range(RWKV_HEADS):
                    idx[jh * 128 + jl * 32 + part * 8 + h] = h * HEAD_DIM + part * JJ + jh * 4 + jl
    return idx


def _i_layout(dup):
    idx = np.full((I_W,), -1, np.int64)
    for isub in range(8):
        for q in range(4 if dup else 2):
            for h in range(RWKV_HEADS):
                for il in range(I_REP):
                    idx[isub * 128 + q * 32 + h * 4 + il] = h * HEAD_DIM + (q % 2) * 32 + isub * 4 + il
    return idx


J_IDX = _j_layout()
I_IDX = _i_layout(True)
I_IDX_ONCE = _i_layout(False)


def _take_cols(w, idx):
    g = jnp.take(w, jnp.asarray(np.maximum(idx, 0)), axis=-1)
    return g * jnp.asarray((idx >= 0).astype(np.float32))


def _take_rows(w, idx):
    g = jnp.take(w, jnp.asarray(np.maximum(idx, 0)), axis=0)
    return g * jnp.asarray((idx >= 0).astype(np.float32))[:, None]


def _expand_matrix():
    m = np.zeros((LANES, E_W), np.float32)
    for jl in range(4):
        for part in range(J_PARTS):
            for h in range(HEAD_SLOTS):
                for rep in range(I_REP):
                    m[jl * 32 + part * 8 + h, jl * 128 + part * 32 + h * I_REP + rep] = 1.0
    return m


def _seg_matrices():
    hj = np.arange(LANES) % 8
    hi = (np.arange(LANES) // I_REP) % 8
    jj = (hj[:, None] == hj[None, :]).astype(np.float32)
    ji = (hj[:, None] == hi[None, :]).astype(np.float32)
    ii = (hi[:, None] == hi[None, :]).astype(np.float32)
    return jj, ji, ii


def _cparams(sem, vmem=VMEM_LIMIT):
    return pltpu.CompilerParams(dimension_semantics=sem, vmem_limit_bytes=vmem)


def _const_spec(shape):
    nd = len(shape)
    return pl.BlockSpec(shape, lambda *_: (0,) * nd)


def _sigmoid(x):
    return 1.0 / (1.0 + jnp.exp(-x))


def _rms(x, g):
    return x * lax.rsqrt(jnp.mean(x * x, axis=-1, keepdims=True) + NORM_EPS) * g


def _dot(a, b):
    return jnp.dot(a, b, preferred_element_type=F32)


def _split_dot(x, m, passes):
    acc = None
    rem = x
    for p in range(passes):
        piece = rem.astype(BF16)
        d = _dot(piece, m)
        acc = d if acc is None else acc + d
        if p + 1 < passes:
            rem = rem - piece.astype(F32)
    return acc


def _ada_kernel(c_ref, w_ref, b_ref, o_ref):
    c = c_ref[...]
    s = c * _sigmoid(c)
    o_ref[0] = _dot(s.astype(BF16), w_ref[0].astype(BF16)) + b_ref[0]


def _ada_mod(cond8, mod_w, mod_b):
    nl, d, nw = mod_w.shape
    tn = nw // 8
    return pl.pallas_call(
        _ada_kernel,
        out_shape=jax.ShapeDtypeStruct((nl, 8, nw), F32),
        grid=(nl, nw // tn),
        in_specs=[pl.BlockSpec((8, d), lambda l, j: (0, 0)),
                  pl.BlockSpec((1, d, tn), lambda l, j: (l, 0, j)),
                  pl.BlockSpec((1, 1, tn), lambda l, j: (l, 0, j))],
        out_specs=pl.BlockSpec((1, 8, tn), lambda l, j: (l, 0, j)),
        compiler_params=_cparams(("arbitrary", "arbitrary")),
        name="ada_mod",
    )(cond8, mod_w, mod_b.reshape(nl, 1, nw))


def _ffn_kernel(x_ref, m_ref, g_ref, wg_ref, wu_ref, wo_ref, o_ref):
    x = x_ref[0]
    m = m_ref[0]
    g = g_ref[...]
    h = _rms(x, g[0:1]) * (1.0 + m[1:2]) + m[0:1]
    hb = h.astype(BF16)
    gate = _dot(hb, wg_ref[...])
    up = _dot(hb, wu_ref[...])
    act = (gate * _sigmoid(gate) * up).astype(BF16)
    y = _dot(act, wo_ref[...])
    o_ref[0] = x + 0.5 * m[2:3] * _rms(y, g[1:2])


def _ffn_half(x, m3, g2, wg, wu, wo, tm):
    b, n, d = x.shape
    fp = wg.shape[1]
    return pl.pallas_call(
        _ffn_kernel,
        out_shape=jax.ShapeDtypeStruct(x.shape, F32),
        grid=(b, n // tm),
        in_specs=[pl.BlockSpec((1, tm, d), lambda i, j: (i, j, 0)),
                  pl.BlockSpec((1, 3, d), lambda i, j: (i, 0, 0)),
                  _const_spec((2, d)),
                  _const_spec((d, fp)), _const_spec((d, fp)), _const_spec((fp, d))],
        out_specs=pl.BlockSpec((1, tm, d), lambda i, j: (i, j, 0)),
        compiler_params=_cparams(("arbitrary", "arbitrary")),
        name="ffn_half",
    )(x, m3, g2, wg, wu, wo)


MIX_COLS = (("f", FOURIER_W), ("q", ATTN_W), ("qr", ATTN_W), ("k", 2 * KV_W), ("kr", 2 * KV_W),
            ("v", 2 * KV_W), ("r", J_W), ("rk", J_W), ("rv", I_W), ("lora", 4 * DECAY_LORA), ("g", GATE_LORA))
MIX_OFF = {}
_o = 0
for _n, _w in MIX_COLS:
    MIX_OFF[_n] = (_o, _o + _w)
    _o += _w
MIX_TOTAL = _o


def _mix_in_kernel(x_ref, m_ref, g_ref, w_ref, cos_ref, sin_ref,
                   f_ref, q_ref, k_ref, v_ref, r_ref, rk_ref, rv_ref, lora_ref, zg_ref):
    x = x_ref[0]
    m = m_ref[0]
    h = _rms(x, g_ref[...]) * (1.0 + m[1:2]) + m[0:1]
    z = _dot(h.astype(BF16), w_ref[...])

    def col(name):
        lo, hi = MIX_OFF[name]
        return z[:, lo:hi]

    cos = cos_ref[...]
    sin = sin_ref[...]
    cos3 = jnp.concatenate([cos, cos, cos], axis=1)
    sin3 = jnp.concatenate([sin, sin, sin], axis=1)
    cos2 = jnp.concatenate([cos, cos], axis=1)
    sin2 = jnp.concatenate([sin, sin], axis=1)
    f_ref[0] = col("f")
    q_ref[0] = ((col("q") * cos3 + col("qr") * sin3) * (HEAD_DIM ** -0.5)).astype(BF16)
    k_ref[0] = (col("k") * cos2 + col("kr") * sin2).astype(BF16)
    v_ref[0] = col("v").astype(BF16)
    r_ref[0] = col("r")
    rk_ref[0] = col("rk")
    rv_ref[0] = col("rv")
    lora_ref[0] = col("lora")
    zg_ref[0] = col("g")


def _mix_in(x, m3, g1, w_all, cos, sin, tm):
    b, n, d = x.shape
    widths = (FOURIER_W, ATTN_W, 2 * KV_W, 2 * KV_W, J_W, J_W, I_W, 4 * DECAY_LORA, GATE_LORA)
    dtypes = (F32, BF16, BF16, BF16, F32, F32, F32, F32, F32)
    return pl.pallas_call(
        _mix_in_kernel,
        out_shape=[jax.ShapeDtypeStruct((b, n, w), dt) for w, dt in zip(widths, dtypes)],
        grid=(b, n // tm),
        in_specs=[pl.BlockSpec((1, tm, d), lambda i, j: (i, j, 0)),
                  pl.BlockSpec((1, 3, d), lambda i, j: (i, 0, 0)),
                  _const_spec((1, d)),
                  _const_spec((d, MIX_TOTAL)),
                  pl.BlockSpec((tm, LANES), lambda i, j: (j, 0)),
                  pl.BlockSpec((tm, LANES), lambda i, j: (j, 0))],
        out_specs=[pl.BlockSpec((1, tm, w), lambda i, j: (i, j, 0)) for w in widths],
        compiler_params=_cparams(("arbitrary", "arbitrary")),
        name="mix_in",
    )(x, m3, g1, w_all, cos, sin)


def _fourier_stage1_kernel(l_ref, x_ref, o_ref):
    o_ref[0] = _dot(l_ref[...], x_ref[0].astype(BF16))


def _fourier_stage2_kernel(a_ref, mc_ref, ms_ref, cc_ref, sc_ref, o_ref, *, scale):
    ar = a_ref[0, 0, 0].astype(BF16)
    ai = a_ref[0, 1, 0].astype(BF16)
    mc = mc_ref[0]
    ms = ms_ref[0]
    gr = _dot(mc, ar) + _dot(ms, ai)
    gi = _dot(mc, ai) - _dot(ms, ar)
    o_ref[0] = (_dot(gr.astype(BF16), cc_ref[...]) + _dot(gi.astype(BF16), sc_ref[...])) * scale


def _fourier_small_kernel(z_ref, cn_ref, sn_ref, cc_ref, sc_ref, o_ref, *, scale):
    z = z_ref[0].astype(BF16)
    t1 = _dot(z, cc_ref[...]).astype(BF16)
    t2 = _dot(z, sc_ref[...]).astype(BF16)
    o_ref[0] = (_dot(cn_ref[...], t1) - _dot(sn_ref[...], t2)) * scale


def _channel_dft():
    gw = FOURIER_W // FOURIER_GROUPS
    c = np.arange(FOURIER_W)
    same = (c[:, None] // gw) == (c[None, :] // gw)
    ang = 2.0 * np.pi * ((c[:, None] % gw) * (c[None, :] % gw) % gw) / gw
    return (np.cos(ang) * same).astype(np.float32), (np.sin(ang) * same).astype(np.float32)


def _fourier_mix(z):
    b, n, w = z.shape
    gw = w // FOURIER_GROUPS
    scale = float(1.0 / np.sqrt(n * gw))
    cc, sc = _channel_dft()
    cc = jnp.asarray(cc, BF16)
    sc = jnp.asarray(sc, BF16)
    if n <= FOURIER_DIRECT_MAX:
        p = np.arange(n)
        ang = 2.0 * np.pi * ((p[:, None] * p[None, :]) % n) / n
        return pl.pallas_call(
            functools.partial(_fourier_small_kernel, scale=scale),
            out_shape=jax.ShapeDtypeStruct((b, n, w), F32),
            grid=(b,),
            in_specs=[pl.BlockSpec((1, n, w), lambda i: (i, 0, 0)),
                      _const_spec((n, n)), _const_spec((n, n)), _const_spec((w, w)), _const_spec((w, w))],
            out_specs=pl.BlockSpec((1, n, w), lambda i: (i, 0, 0)),
            compiler_params=_cparams(("arbitrary",)),
            name="fourier_small",
        )(z, jnp.asarray(np.cos(ang), BF16), jnp.asarray(np.sin(ang), BF16), cc, sc)
    n1 = FOURIER_N1
    n2 = n // n1
    k1 = np.arange(n1)
    ang1 = 2.0 * np.pi * ((k1[:, None] * k1[None, :]) % n1) / n1
    lhs1 = jnp.asarray(np.concatenate([np.cos(ang1), -np.sin(ang1)], axis=0), BF16)
    tc = min(n2 * w, 4096)
    a = pl.pallas_call(
        _fourier_stage1_kernel,
        out_shape=jax.ShapeDtypeStruct((b, 2 * n1, n2 * w), F32),
        grid=(b, (n2 * w) // tc),
        in_specs=[_const_spec((2 * n1, n1)),
                  pl.BlockSpec((1, n1, tc), lambda i, j: (i, 0, j))],
        out_specs=pl.BlockSpec((1, 2 * n1, tc), lambda i, j: (i, 0, j)),
        compiler_params=_cparams(("arbitrary", "arbitrary")),
        name="fourier_stage1",
    )(lhs1, z.reshape(b, n1, n2 * w))
    k2 = np.arange(n2)
    freq = (k1[:, None, None] + n1 * k2[None, :, None]) * k2[None, None, :]
    ang2 = 2.0 * np.pi * (freq % n) / n
    out = pl.pallas_call(
        functools.partial(_fourier_stage2_kernel, scale=scale),
        out_shape=jax.ShapeDtypeStruct((b, n2, n1 * w), F32),
        grid=(b, n1),
        in_specs=[pl.BlockSpec((1, 2, 1, n2, w), lambda i, j: (i, 0, j, 0, 0)),
                  pl.BlockSpec((1, n2, n2), lambda i, j: (j, 0, 0)),
                  pl.BlockSpec((1, n2, n2), lambda i, j: (j, 0, 0)),
                  _const_spec((w, w)), _const_spec((w, w))],
        out_specs=pl.BlockSpec((1, n2, w), lambda i, j: (i, 0, j)),
        compiler_params=_cparams(("arbitrary", "arbitrary")),
        name="fourier_stage2",
    )(a.reshape(b, 2, n1, n2, w), jnp.asarray(np.cos(ang2), BF16), jnp.asarray(np.sin(ang2), BF16), cc, sc)
    return out.reshape(b, n, w)


NEG = -1e30


def _attn_kernel(*refs, window):
    if window:
        (sink_ref, q_ref, kp_ref, kc_ref, kn_ref, vp_ref, vc_ref, vn_ref, kx_ref, vx_ref, o_ref) = refs
    else:
        (sink_ref, q_ref, kx_ref, vx_ref, o_ref) = refs
    i = pl.program_id(1)
    nb = pl.num_programs(1)
    tq = q_ref.shape[1]
    lane = lax.broadcasted_iota(jnp.int32, (1, KV_W), 1)
    low = lane < HEAD_DIM
    if window:
        k_all = jnp.concatenate([kp_ref[0], kc_ref[0], kn_ref[0], kx_ref[0]], axis=0)
        v_all = jnp.concatenate([vp_ref[0], vc_ref[0], vn_ref[0], vx_ref[0]], axis=0)
        row = lax.broadcasted_iota(jnp.int32, (tq, WINDOW), 0)
        col = lax.broadcasted_iota(jnp.int32, (tq, WINDOW), 1)
        prev_ok = jnp.logical_and(col >= row, i > 0)
        next_ok = jnp.logical_and(col <= row, i < nb - 1)
        mask = jnp.concatenate([jnp.where(prev_ok, 0.0, NEG), jnp.zeros((tq, WINDOW), F32),
                                jnp.where(next_ok, 0.0, NEG), jnp.zeros((tq, kx_ref.shape[1]), F32)], axis=1)
    else:
        k_all = kx_ref[0]
        v_all = vx_ref[0]
        mask = None
    zero = jnp.zeros((), BF16)
    one = jnp.ones((), BF16)
    for pair in range(ATTN_HEADS // 2):
        qp = q_ref[0, :, pair * LANES:(pair + 1) * LANES]
        outs = []
        sinks = []
        for slot in range(2):
            h = 2 * pair + slot
            kv = h // ATTN_GROUP
            swap = slot != kv
            keep = (lane >= HEAD_DIM) if slot else low
            kh = jnp.where(keep, k_all[:, KV_W:] if swap else k_all[:, :KV_W], zero)
            vh = jnp.where(keep, v_all[:, KV_W:] if swap else v_all[:, :KV_W], one)
            s = lax.dot_general(qp, kh, (((1,), (1,)), ((), ())), preferred_element_type=F32)
            if mask is not None:
                s = s + mask
            sk = sink_ref[h]
            mx = jnp.maximum(s.max(axis=-1, keepdims=True), sk)
            outs.append(_dot(jnp.exp(s - mx).astype(BF16), vh))
            sinks.append(jnp.exp(sk - mx))
        num = jnp.where(low, outs[0], outs[1])
        den = pltpu.roll(jnp.where(low, outs[1], outs[0]), HEAD_DIM, 1) + jnp.where(low, sinks[0], sinks[1])
        o_ref[0, :, pair * LANES:(pair + 1) * LANES] = num / den


def _attention(q, k, v, kx, vx, sink8, window):
    b, n, _ = q.shape
    c = kx.shape[1]
    tq = WINDOW
    nb = n // tq
    smem = pl.BlockSpec(memory_space=pltpu.SMEM)
    qspec = pl.BlockSpec((1, tq, ATTN_W), lambda i, j: (i, j, 0))
    xspec = pl.BlockSpec((1, c, 2 * KV_W), lambda i, j: (i, 0, 0))
    if window:
        prev = pl.BlockSpec((1, tq, 2 * KV_W), lambda i, j: (i, jnp.maximum(j - 1, 0), 0))
        cur = pl.BlockSpec((1, tq, 2 * KV_W), lambda i, j: (i, j, 0))
        nxt = pl.BlockSpec((1, tq, 2 * KV_W), lambda i, j: (i, jnp.minimum(j + 1, nb - 1), 0))
        in_specs = [smem, qspec, prev, cur, nxt, prev, cur, nxt, xspec, xspec]
        args = (sink8, q, k, k, k, v, v, v, kx, vx)
    else:
        in_specs = [smem, qspec, xspec, xspec]
        args = (sink8, q, kx, vx)
    return pl.pallas_call(
        functools.partial(_attn_kernel, window=window),
        out_shape=jax.ShapeDtypeStruct((b, n, ATTN_W), F32),
        grid=(b, nb),
        in_specs=in_specs,
        out_specs=pl.BlockSpec((1, tq, ATTN_W), lambda i, j: (i, j, 0)),
        compiler_params=_cparams(("arbitrary", "arbitrary")),
        name="window_attention" if window else "context_attention",
    )(*args)


def _shift_rows(x, prev_row, next_row):
    n = x.shape[0]
    row = lax.broadcasted_iota(jnp.int32, (n, 1), 0)
    xp = jnp.where(row == 0, prev_row, pltpu.roll(x, 1, 0))
    xn = jnp.where(row == n - 1, next_row, pltpu.roll(x, n - 1, 0))
    return xp, xn


def _rwkv_prep_kernel(r_ref, rp_ref, rn_ref, k_ref, kp_ref, kn_ref, v_ref, vp_ref, vn_ref, lora_ref,
                      cwr_ref, cwk_ref, cwv_ref, w0_ref, w2_ref, a0_ref, a2_ref, vec_ref, sjj_ref, sji_ref,
                      rc_ref, nkk_ref, vc_ref, bonus_ref, dec_ref, kd_ref, bb_ref):
    i = pl.program_id(1)
    first = (i > 0).astype(F32)
    last = (i < pl.num_programs(1) - 1).astype(F32)

    def conv(x_ref, p_ref, n_ref, cw_ref):
        x = x_ref[0]
        xp, xn = _shift_rows(x, p_ref[0, SUBLANES - 1:SUBLANES, :] * first, n_ref[0, 0:1, :] * last)
        cw = cw_ref[...]
        return xp * cw[0:1] + x * cw[1:2] + xn * cw[2:3]

    r = conv(r_ref, rp_ref, rn_ref, cwr_ref)
    k = conv(k_ref, kp_ref, kn_ref, cwk_ref)
    v = conv(v_ref, vp_ref, vn_ref, cwv_ref)
    vec = vec_ref[...]
    sjj = sjj_ref[...]
    sji = sji_ref[...]

    def fold(x):
        return (x[:, 0:LANES] + x[:, LANES:2 * LANES]) + (x[:, 2 * LANES:3 * LANES] + x[:, 3 * LANES:4 * LANES])

    kk = k * vec[0:1]
    ss = _split_dot(fold(kk * kk), sjj, 2)
    inv = lax.rsqrt(jnp.maximum(ss, 1e-24))
    kk = kk * jnp.concatenate([inv, inv, inv, inv], axis=1)
    rk = _split_dot(fold(r * k * vec[2:3]), sji, 2)
    rc_ref[0] = r
    nkk_ref[0] = -kk
    vc_ref[0] = v
    bonus_ref[0] = v * jnp.concatenate([rk] * SUBLANES, axis=1)
    lora = lora_ref[0]
    zw = jnp.tanh(lora[:, :2 * DECAY_LORA])
    za = lora[:, 2 * DECAY_LORA:]
    for d in range(2):
        wpre = w0_ref[d:d + 1, :] + jnp.dot(zw, w2_ref[d], preferred_element_type=F32,
                                            precision=lax.Precision.HIGHEST)
        apre = a0_ref[d:d + 1, :] + jnp.dot(za, a2_ref[d], preferred_element_type=F32,
                                            precision=lax.Precision.HIGHEST)
        nw = -wpre
        softplus = jnp.maximum(nw, 0.0) + jnp.log(1.0 + jnp.exp(-jnp.abs(nw)))
        w = -softplus - 0.5
        a = _sigmoid(apre)
        dec_ref[d, 0] = jnp.exp(-jnp.exp(w))
        kd_ref[d, 0] = k * (1.0 + (a - 1.0) * vec[1:2])
        bb_ref[d, 0] = kk * a


def _rwkv_prep(r, k, v, lora, cwr, cwk, cwv, w0, w2, a0, a2, vec, sjj, sji, tm):
    b, n, _ = r.shape
    nh = n // SUBLANES

    def tile(w):
        return pl.BlockSpec((1, tm, w), lambda i, j: (i, j, 0))

    def halo_prev(w):
        return pl.BlockSpec((1, SUBLANES, w), lambda i, j: (i, jnp.maximum(j * (tm // SUBLANES) - 1, 0), 0))

    def halo_next(w):
        return pl.BlockSpec((1, SUBLANES, w), lambda i, j: (i, jnp.minimum((j + 1) * (tm // SUBLANES), nh - 1), 0))

    def dir_tile(w):
        return pl.BlockSpec((2, 1, tm, w), lambda i, j: (0, i, j, 0))

    out_shape = [jax.ShapeDtypeStruct((b, n, J_W), F32), jax.ShapeDtypeStruct((b, n, J_W), F32),
                 jax.ShapeDtypeStruct((b, n, I_W), F32), jax.ShapeDtypeStruct((b, n, I_W), F32),
                 jax.ShapeDtypeStruct((2, b, n, J_W), F32), jax.ShapeDtypeStruct((2, b, n, J_W), F32),
                 jax.ShapeDtypeStruct((2, b, n, J_W), F32)]
    return pl.pallas_call(
        _rwkv_prep_kernel,
        out_shape=out_shape,
        grid=(b, n // tm),
        in_specs=[tile(J_W), halo_prev(J_W), halo_next(J_W),
                  tile(J_W), halo_prev(J_W), halo_next(J_W),
                  tile(I_W), halo_prev(I_W), halo_next(I_W),
                  tile(4 * DECAY_LORA),
                  _const_spec((CONV_W, J_W)), _const_spec((CONV_W, J_W)), _const_spec((CONV_W, I_W)),
                  _const_spec((2, J_W)), _const_spec((2, 2 * DECAY_LORA, J_W)),
                  _const_spec((2, J_W)), _const_spec((2, 2 * ICLR_LORA, J_W)),
                  _const_spec((3, J_W)), _const_spec((LANES, LANES)), _const_spec((LANES, LANES))],
        out_specs=[tile(J_W), tile(J_W), tile(I_W), tile(I_W), dir_tile(J_W), dir_tile(J_W), dir_tile(J_W)],
        compiler_params=_cparams(("arbitrary", "arbitrary")),
        name="rwkv_prep",
    )(r, r, r, k, k, k, v, v, v, lora, cwr, cwk, cwv, w0, w2, a0, a2, vec, sjj, sji)


N_ROWVEC = 5
N_DIR_REFS = N_ROWVEC + 1
SCAN_PASSES = (2, 3, 2, 2, 2)


def _scan_kernel(*refs, t_chunk, n_batch):
    ins = refs[:2 * N_DIR_REFS]
    rexp_ref, s0_ref = refs[2 * N_DIR_REFS:2 * N_DIR_REFS + 2]
    y_refs = refs[2 * N_DIR_REFS + 2:2 * N_DIR_REFS + 4]
    sout_ref = refs[2 * N_DIR_REFS + 4]
    e_ref, s_ref, vbuf_ref, ybuf_ref = refs[2 * N_DIR_REFS + 5:]
    step_id = pl.program_id(0)
    assert n_batch == 2

    @pl.when(step_id == 0)
    def _():
        s_ref[...] = s0_ref[...]

    rexp = rexp_ref[...]
    for d in range(2):
        for x in range(N_ROWVEC):
            ref = ins[d * N_DIR_REFS + x]
            both = jnp.concatenate([ref[bi] if len(ref.shape) == 3 else ref[0, bi] for bi in range(n_batch)], axis=1)
            rows = pltpu.einshape("t(kl)->tkl", both, l=LANES).reshape(t_chunk * SUBLANES, LANES)
            e_ref[d, x] = _split_dot(rows, rexp, SCAN_PASSES[x]).reshape(t_chunk, SUBLANES, E_W)
        for bi in range(n_batch):
            vbuf_ref[d * n_batch + bi] = pltpu.einshape("t(kl)->tkl", ins[d * N_DIR_REFS + N_ROWVEC][bi], l=LANES)

    lane = lax.broadcasted_iota(jnp.int32, (SUBLANES, LANES), 1)
    q_even = (lane // 32) % 2 == 0

    def allparts(p):
        return (p + pltpu.roll(p, 64, 1)) + (pltpu.roll(p, 32, 1) + pltpu.roll(p, 96, 1))

    def step(t, d, bi):
        g = d * n_batch + bi
        vp = vbuf_ref[g, t]
        vr = pltpu.roll(vp, 32, 1)
        v = (jnp.where(q_even, vp, vr), jnp.where(q_even, vr, vp))

        def row(x, jj):
            r0 = bi * 4 + jj // 4
            return e_ref[d, x, t, r0:r0 + 1, (jj % 4) * LANES:(jj % 4 + 1) * LANES]

        def add(acc, ih, jj, term):
            k = (ih, jj % 4)
            acc[k] = term if k not in acc else acc[k] + term

        def total(acc, ih):
            return allparts((acc[ih, 0] + acc[ih, 1]) + (acc[ih, 2] + acc[ih, 3]))

        acc = {}
        for jj in range(JJ):
            a = row(0, jj)
            for ih in range(I_HI):
                add(acc, ih, jj, s_ref[g, ih, jj] * a)
        sa = [total(acc, ih) for ih in range(I_HI)]
        acc = {}
        for jj in range(JJ):
            w, b, k, rr = row(1, jj), row(2, jj), row(3, jj), row(4, jj)
            for ih in range(I_HI):
                new = s_ref[g, ih, jj] * w + sa[ih] * b + v[ih] * k
                s_ref[g, ih, jj] = new
                add(acc, ih, jj, new * rr)
        y = [total(acc, ih) for ih in range(I_HI)]
        ybuf_ref[g, t] = jnp.where(q_even, y[0], y[1])

    def one_token(s, carry):
        for d in range(2):
            for bi in range(n_batch):
                step(s if d == 0 else t_chunk - 1 - s, d, bi)
        return carry

    lax.fori_loop(0, t_chunk, one_token, 0, unroll=SCAN_UNROLL)

    for d in range(2):
        for bi in range(n_batch):
            y_refs[d][bi] = pltpu.einshape("tkl->t(kl)", ybuf_ref[d * n_batch + bi])

    @pl.when(step_id == pl.num_programs(0) - 1)
    def _():
        sout_ref[...] = s_ref[...]


def _rwkv_scan(rc, nkk, vc, dec, kd, bb, state0, rexp):
    b, n, _ = rc.shape
    t = min(SCAN_T, n)
    nc = n // t
    in_specs = []
    args = []
    out_specs = []
    for d in range(2):
        def chunk(i, d=d):
            return i if d == 0 else nc - 1 - i

        shared = pl.BlockSpec((b, t, J_W), lambda i, chunk=chunk: (0, chunk(i), 0))
        perdir = pl.BlockSpec((1, b, t, J_W), lambda i, d=d, chunk=chunk: (d, 0, chunk(i), 0))
        ispec = pl.BlockSpec((b, t, I_W), lambda i, chunk=chunk: (0, chunk(i), 0))
        in_specs += [shared, perdir, perdir, perdir, shared, ispec]
        args += [nkk, dec, bb, kd, rc, vc]
        out_specs.append(ispec)
    state_shape = (2 * b, I_HI, JJ, SUBLANES, LANES)
    in_specs += [_const_spec((LANES, E_W)), _const_spec(state_shape)]
    args += [rexp, state0]
    out_specs.append(_const_spec(state_shape))
    y_shape = jax.ShapeDtypeStruct((b, n, I_W), F32)
    return pl.pallas_call(
        functools.partial(_scan_kernel, t_chunk=t, n_batch=b),
        out_shape=[y_shape, y_shape, jax.ShapeDtypeStruct(state_shape, F32)],
        grid=(nc,),
        in_specs=in_specs,
        out_specs=out_specs,
        scratch_shapes=[pltpu.VMEM((2, N_ROWVEC, t, SUBLANES, E_W), F32),
                        pltpu.VMEM(state_shape, F32),
                        pltpu.VMEM((2 * b, t, SUBLANES, LANES), F32),
                        pltpu.VMEM((2 * b, t, SUBLANES, LANES), F32)],
        compiler_params=_cparams(("arbitrary",)),
        name="rwkv_scan",
    )(*args)


def _mix_out_kernel(x_ref, m_ref, g_ref, f_ref, a_ref, yf_ref, yb_ref, bonus_ref, zg_ref,
                    g2_ref, ln_ref, sii_ref, wf_ref, wa_ref, wr_ref, o_ref):
    x = x_ref[0]
    y = yf_ref[0] + yb_ref[0]
    sii = sii_ref[...]
    nblk = I_W // LANES
    cnt = float(2 * HEAD_DIM)

    def fold(z):
        acc = z[:, 0:LANES]
        for c in range(1, nblk):
            acc = acc + z[:, c * LANES:(c + 1) * LANES]
        return acc

    def spread(z):
        return jnp.concatenate([z] * nblk, axis=1)

    mu = _split_dot(fold(y), sii, 2) * (1.0 / cnt)
    dlt = y - spread(mu)
    var = _split_dot(fold(dlt * dlt), sii, 2) * (1.0 / cnt)
    ln = ln_ref[...]
    yn = dlt * spread(lax.rsqrt(var + GN_EPS)) * ln[0:1] + ln[1:2]
    gate = _dot(_sigmoid(zg_ref[0]).astype(BF16), g2_ref[...])
    rw = ((yn + bonus_ref[0]) * gate).astype(BF16)
    o = _dot(f_ref[0].astype(BF16), wf_ref[...]) + _dot(a_ref[0].astype(BF16), wa_ref[...]) + _dot(rw, wr_ref[...])
    o_ref[0] = x + m_ref[0] * _rms(o, g_ref[...])


def _mix_out(x, gate_row, g3, fo, ao, y_f, y_b, bonus, zg, g2p, ln, sii, wf, wa, wr, tm):
    b, n, d = x.shape

    def tile(w):
        return pl.BlockSpec((1, tm, w), lambda i, j: (i, j, 0))

    return pl.pallas_call(
        _mix_out_kernel,
        out_shape=jax.ShapeDtypeStruct(x.shape, F32),
        grid=(b, n // tm),
        in_specs=[tile(d), pl.BlockSpec((1, 1, d), lambda i, j: (i, 0, 0)), _const_spec((1, d)),
                  tile(FOURIER_W), tile(ATTN_W),
                  tile(I_W), tile(I_W), tile(I_W), tile(GATE_LORA),
                  _const_spec((GATE_LORA, I_W)), _const_spec((2, I_W)), _const_spec((LANES, LANES)),
                  _const_spec((FOURIER_W, d)), _const_spec((ATTN_W, d)), _const_spec((I_W, d))],
        out_specs=tile(d),
        compiler_params=_cparams(("arbitrary", "arbitrary")),
        name="mix_out",
    )(x, gate_row, g3, fo, ao, y_f, y_b, bonus, zg, g2p, ln, sii, wf, wa, wr)


def _rope_tables(n_tokens):
    rows_n = n_tokens // GRID_W
    row = jnp.repeat(jnp.arange(rows_n), GRID_W).astype(F32)
    colp = jnp.tile(jnp.arange(GRID_W), rows_n).astype(F32)
    n_freq = HEAD_DIM // 4
    inv = ROPE_BASE ** (-jnp.arange(n_freq, dtype=F32) / n_freq)
    ang = jnp.concatenate([row[:, None] * inv, colp[:, None] * inv], axis=-1)
    cos = jnp.tile(jnp.cos(ang), (1, 4))
    sin = jnp.tile(jnp.sin(ang), (1, 4))
    return cos, sin


def _rot_cols(w):
    d, n = w.shape
    w4 = w.reshape(d, n // HEAD_DIM, 2, HEAD_DIM // 2)
    return jnp.stack([-w4[:, :, 1], w4[:, :, 0]], axis=2).reshape(d, n)


def _layer_weights(li, p):
    f = p["ffn1_wo"].shape[1]
    fp = -(-f // (2 * LANES)) * (2 * LANES)
    out = {}
    for name in ("ffn1", "ffn2"):
        wi = p[name + "_wi"][li]
        wo = p[name + "_wo"][li]
        out[name] = (jnp.pad(wi[:, :f], ((0, 0), (0, fp - f))).astype(BF16),
                     jnp.pad(wi[:, f:], ((0, 0), (0, fp - f))).astype(BF16),
                     jnp.pad(wo, ((0, fp - f), (0, 0))).astype(BF16))
    w_in = p["mix_w_in"][li]
    offs = np.cumsum([0, FOURIER_W, ATTN_W, KV_W, KV_W, RWKV_W, RWKV_W, RWKV_W,
                      DECAY_LORA, DECAY_LORA, ICLR_LORA, ICLR_LORA, GATE_LORA])
    part = [w_in[:, offs[i]:offs[i + 1]] for i in range(12)]
    wf, wq, wk, wv, wrr, wrk, wrv = part[:7]
    swap = lambda w: jnp.concatenate([w[:, HEAD_DIM:], w[:, :HEAD_DIM]], axis=1)
    cols = {"f": wf, "q": wq, "qr": _rot_cols(wq),
            "k": jnp.concatenate([wk, swap(wk)], axis=1),
            "kr": jnp.concatenate([_rot_cols(wk), swap(_rot_cols(wk))], axis=1),
            "v": jnp.concatenate([wv, swap(wv)], axis=1),
            "r": _take_cols(wrr, J_IDX), "rk": _take_cols(wrk, J_IDX), "rv": _take_cols(wrv, I_IDX),
            "lora": jnp.concatenate(part[7:11], axis=1), "g": part[11]}
    out["w_all"] = jnp.concatenate([cols[n] for n, _ in MIX_COLS], axis=1).astype(BF16)
    conv = p["rwkv_conv"][li]
    out["cwr"] = _take_cols(conv[:, :RWKV_W], J_IDX)
    out["cwk"] = _take_cols(conv[:, RWKV_W:2 * RWKV_W], J_IDX)
    out["cwv"] = _take_cols(conv[:, 2 * RWKV_W:], I_IDX)
    def per_direction(w2):
        w2 = _take_cols(w2, J_IDX)
        z = jnp.zeros_like(w2[0])
        return jnp.stack([jnp.concatenate([w2[0], z], axis=0), jnp.concatenate([z, w2[1]], axis=0)])

    out["w0"] = _take_cols(p["rwkv_w0"][li], J_IDX)
    out["w2"] = per_direction(p["rwkv_w2"][li])
    out["a0"] = _take_cols(p["rwkv_a0"][li], J_IDX)
    out["a2"] = per_direction(p["rwkv_a2"][li])
    out["vec"] = _take_cols(jnp.stack([p["rwkv_k_k"][li], p["rwkv_k_a"][li], p["rwkv_r_k"][li].reshape(-1)]), J_IDX)
    out["g2"] = _take_cols(p["rwkv_g2"][li], I_IDX).astype(BF16)
    out["ln"] = _take_cols(jnp.stack([p["rwkv_ln_g"][li], p["rwkv_ln_b"][li]]), I_IDX)
    w_out = p["mix_w_out"][li]
    out["wo_f"] = w_out[:FOURIER_W].astype(BF16)
    out["wo_a"] = w_out[FOURIER_W:FOURIER_W + ATTN_W].astype(BF16)
    out["wo_r"] = _take_rows(w_out[FOURIER_W + ATTN_W:], I_IDX_ONCE).astype(BF16)
    sink = p["attn_sink"][li]
    out["sink"] = jnp.concatenate([sink, jnp.zeros((8 - ATTN_HEADS,), F32)])
    return out


def kernel(x, c, ctx, c_ctx, mod_w, mod_b, norm_g, ffn1_wi, ffn1_wo, mix_w_in, mix_w_out, attn_sink,
           rwkv_conv, rwkv_w0, rwkv_w2, rwkv_a0, rwkv_a2, rwkv_g2, rwkv_k_k, rwkv_k_a, rwkv_r_k,
           rwkv_ln_g, rwkv_ln_b, ffn2_wi, ffn2_wo):
    p = dict(ffn1_wi=ffn1_wi, ffn1_wo=ffn1_wo, ffn2_wi=ffn2_wi, ffn2_wo=ffn2_wo, mix_w_in=mix_w_in,
             mix_w_out=mix_w_out, attn_sink=attn_sink, rwkv_conv=rwkv_conv, rwkv_w0=rwkv_w0, rwkv_w2=rwkv_w2,
             rwkv_a0=rwkv_a0, rwkv_a2=rwkv_a2, rwkv_g2=rwkv_g2, rwkv_k_k=rwkv_k_k, rwkv_k_a=rwkv_k_a,
             rwkv_r_k=rwkv_r_k, rwkv_ln_g=rwkv_ln_g, rwkv_ln_b=rwkv_ln_b)
    b, s, d = x.shape
    n_c = ctx.shape[1]
    depth = mod_w.shape[0]
    assert b + 1 <= 8 and s % WINDOW == 0 and n_c % WINDOW == 0
    tm_l = 256
    tm_c = min(256, n_c)

    cond8 = jnp.zeros((8, d), F32).at[:b].set(c).at[b].set(c_ctx)
    mod = _ada_mod(cond8, mod_w, mod_b).reshape(depth, 8, N_MOD, d)
    cos_l, sin_l = _rope_tables(s)
    cos_c = jnp.ones((n_c, LANES), F32)
    sin_c = jnp.zeros((n_c, LANES), F32)
    rexp = jnp.asarray(_expand_matrix(), BF16)
    sjj, sji, sii = (jnp.asarray(m, BF16) for m in _seg_matrices())
    zero_state = jnp.zeros((2 * b, I_HI, JJ, SUBLANES, LANES), F32)

    xl, xc = x, ctx
    for li in range(depth):
        need_ctx_out = li < depth - 1
        w = _layer_weights(li, p)
        ml = mod[li, :b]
        mc = jnp.broadcast_to(mod[li, b:b + 1], (b, N_MOD, d))
        g = norm_g[li]
        xl = _ffn_half(xl, ml[:, 0:3], g[0:2], *w["ffn1"], tm_l)
        xc = _ffn_half(xc, mc[:, 0:3], g[0:2], *w["ffn1"], tm_c)

        def mixer_in(xx, mm, cos, sin, tm):
            return _mix_in(xx, mm[:, 3:6], g[2:3], w["w_all"], cos, sin, tm)

        fl, ql, kl, vl, rl, rkl, rvl, loral, zgl = mixer_in(xl, ml, cos_l, sin_l, tm_l)
        fc, qc, kc, vc, rc_, rkc, rvc, lorac, zgc = mixer_in(xc, mc, cos_c, sin_c, tm_c)

        attn_l = _attention(ql, kl, vl, kc, vc, w["sink"], True)

        def prep(r_, k_, v_, lora_, tm):
            return _rwkv_prep(r_, k_, v_, lora_, w["cwr"], w["cwk"], w["cwv"], w["w0"], w["w2"],
                              w["a0"], w["a2"], w["vec"], sjj, sji, tm)

        pc = prep(rc_, rkc, rvc, lorac, tm_c)
        plat = prep(rl, rkl, rvl, loral, tm_l)
        ycf, ycb, state_c = _rwkv_scan(pc[0], pc[1], pc[2], pc[4], pc[5], pc[6], zero_state, rexp)
        ylf, ylb, _ = _rwkv_scan(plat[0], plat[1], plat[2], plat[4], plat[5], plat[6], state_c, rexp)

        def mixer_out(xx, mm, fo, ao, y_f, y_b, bonus, zg, tm):
            return _mix_out(xx, mm[:, 5:6], g[3:4], fo, ao, y_f, y_b, bonus, zg, w["g2"], w["ln"], sii,
                            w["wo_f"], w["wo_a"], w["wo_r"], tm)

        xl = mixer_out(xl, ml, _fourier_mix(fl), attn_l, ylf, ylb, plat[3], zgl, tm_l)
        if need_ctx_out:
            attn_c = _attention(qc, None, None, kc, vc, w["sink"], False)
            xc = mixer_out(xc, mc, _fourier_mix(fc), attn_c, ycf, ycb, pc[3], zgc, tm_c)
            xc = _ffn_half(xc, mc[:, 6:9], g[4:6], *w["ffn2"], tm_c)
        xl = _ffn_half(xl, ml[:, 6:9], g[4:6], *w["ffn2"], tm_l)
    return xl
```

```python
import functools

import numpy as np
import jax
import jax.numpy as jnp
from jax import lax
from jax.experimental import pallas as pl
from jax.experimental.pallas import tpu as pltpu

F32 = jnp.float32
BF16 = jnp.bfloat16

HEAD_DIM = 64
GRID_W = 64
FOURIER_W = 256
FOURIER_GROUPS = 4
ATTN_HEADS = 6
ATTN_KV_HEADS = 2
ATTN_GROUP = ATTN_HEADS // ATTN_KV_HEADS
ATTN_W = ATTN_HEADS * HEAD_DIM
KV_W = ATTN_KV_HEADS * HEAD_DIM
RWKV_HEADS = 6
RWKV_W = RWKV_HEADS * HEAD_DIM
WINDOW = 128
ROPE_BASE = 10000.0
DECAY_LORA = 64
ICLR_LORA = 64
GATE_LORA = 128
CONV_W = 3
N_MOD = 9
NORM_EPS = 1e-6
GN_EPS = 64e-5

LANES = 128
SUBLANES = 8
VMEM_LIMIT = 56 * 1024 * 1024

HEAD_SLOTS = 8
J_PARTS = 4
I_REP = 4
JJ = HEAD_DIM // J_PARTS
I_HI = HEAD_DIM // (SUBLANES * I_REP)
J_W = 4 * LANES
E_W = 4 * LANES
I_W = SUBLANES * LANES
SCAN_T = 64
SCAN_UNROLL = 4
FOURIER_N1 = 64
FOURIER_DIRECT_MAX = 256


def _j_layout():
    idx = np.full((J_W,), -1, np.int64)
    for jh in range(4):
        for jl in range(4):
            for part in range(J_PARTS):
                for h in range(RWKV_HEADS):
                    idx[jh * 128 + jl * 32 + part * 8 + h] = h * HEAD_DIM + part * JJ + jh * 4 + jl
    return idx


def _i_layout(dup):
    idx = np.full((I_W,), -1, np.int64)
    for isub in range(8):
        for q in range(4 if dup else 2):
            for h in range(RWKV_HEADS):
                for il in range(I_REP):
                    idx[isub * 128 + q * 32 + h * 4 + il] = h * HEAD_DIM + (q % 2) * 32 + isub * 4 + il
    return idx


J_IDX = _j_layout()
I_IDX = _i_layout(True)
I_IDX_ONCE = _i_layout(False)


def _take_cols(w, idx):
    g = jnp.take(w, jnp.asarray(np.maximum(idx, 0)), axis=-1)
    return g * jnp.asarray((idx >= 0).astype(np.float32))


def _take_rows(w, idx):
    g = jnp.take(w, jnp.asarray(np.maximum(idx, 0)), axis=0)
    return g * jnp.asarray((idx >= 0).astype(np.float32))[:, None]


def _expand_matrix():
    m = np.zeros((LANES, E_W), np.float32)
    for jl in range(4):
        for part in range(J_PARTS):
            for h in range(HEAD_SLOTS):
                for rep in range(I_REP):
                    m[jl * 32 + part * 8 + h, jl * 128 + part * 32 + h * I_REP + rep] = 1.0
    return m


def _seg_matrices():
    hj = np.arange(LANES) % 8
    hi = (np.arange(LANES) // I_REP) % 8
    jj = (hj[:, None] == hj[None, :]).astype(np.float32)
    ji = (hj[:, None] == hi[None, :]).astype(np.float32)
    ii = (hi[:, None] == hi[None, :]).astype(np.float32)
    return jj, ji, ii


def _cparams(sem, vmem=VMEM_LIMIT):
    return pltpu.CompilerParams(dimension_semantics=sem, vmem_limit_bytes=vmem)


def _const_spec(shape):
    nd = len(shape)
    return pl.BlockSpec(shape, lambda *_: (0,) * nd)


def _sigmoid(x):
    return 1.0 / (1.0 + jnp.exp(-x))


def _rms(x, g):
    return x * lax.rsqrt(jnp.mean(x * x, axis=-1, keepdims=True) + NORM_EPS) * g


def _dot(a, b):
    return jnp.dot(a, b, preferred_element_type=F32)


def _split_dot(x, m, passes):
    acc = None
    rem = x
    for p in range(passes):
        piece = rem.astype(BF16)
        d = _dot(piece, m)
        acc = d if acc is None else acc + d
        if p + 1 < passes:
            rem = rem - piece.astype(F32)
    return acc


def _ada_kernel(c_ref, w_ref, b_ref, o_ref):
    c = c_ref[...]
    s = c * _sigmoid(c)
    o_ref[0] = _dot(s.astype(BF16), w_ref[0].astype(BF16)) + b_ref[0]


def _ada_mod(cond8, mod_w, mod_b):
    nl, d, nw = mod_w.shape
    tn = nw // 8
    return pl.pallas_call(
        _ada_kernel,
        out_shape=jax.ShapeDtypeStruct((nl, 8, nw), F32),
        grid=(nl, nw // tn),
        in_specs=[pl.BlockSpec((8, d), lambda l, j: (0, 0)),
                  pl.BlockSpec((1, d, tn), lambda l, j: (l, 0, j)),
                  pl.BlockSpec((1, 1, tn), lambda l, j: (l, 0, j))],
        out_specs=pl.BlockSpec((1, 8, tn), lambda l, j: (l, 0, j)),
        compiler_params=_cparams(("arbitrary", "arbitrary")),
        name="ada_mod",
    )(cond8, mod_w, mod_b.reshape(nl, 1, nw))


def _ffn_kernel(x_ref, m_ref, g_ref, wg_ref, wu_ref, wo_ref, o_ref):
    x = x_ref[0]
    m = m_ref[0]
    g = g_ref[...]
    h = _rms(x, g[0:1]) * (1.0 + m[1:2]) + m[0:1]
    hb = h.astype(BF16)
    gate = _dot(hb, wg_ref[...])
    up = _dot(hb, wu_ref[...])
    act = (gate * _sigmoid(gate) * up).astype(BF16)
    y = _dot(act, wo_ref[...])
    o_ref[0] = x + 0.5 * m[2:3] * _rms(y, g[1:2])


def _ffn_half(x, m3, g2, wg, wu, wo, tm):
    b, n, d = x.shape
    fp = wg.shape[1]
    return pl.pallas_call(
        _ffn_kernel,
        out_shape=jax.ShapeDtypeStruct(x.shape, F32),
        grid=(b, n // tm),
        in_specs=[pl.BlockSpec((1, tm, d), lambda i, j: (i, j, 0)),
                  pl.BlockSpec((1, 3, d), lambda i, j: (i, 0, 0)),
                  _const_spec((2, d)),
                  _const_spec((d, fp)), _const_spec((d, fp)), _const_spec((fp, d))],
        out_specs=pl.BlockSpec((1, tm, d), lambda i, j: (i, j, 0)),
        compiler_params=_cparams(("arbitrary", "arbitrary")),
        name="ffn_half",
    )(x, m3, g2, wg, wu, wo)


MIX_COLS = (("f", FOURIER_W), ("q", ATTN_W), ("qr", ATTN_W), ("k", 2 * KV_W), ("kr", 2 * KV_W),
            ("v", 2 * KV_W), ("r", J_W), ("rk", J_W), ("rv", I_W), ("lora", 4 * DECAY_LORA), ("g", GATE_LORA))
MIX_OFF = {}
_o = 0
for _n, _w in MIX_COLS:
    MIX_OFF[_n] = (_o, _o + _w)
    _o += _w
MIX_TOTAL = _o


def _mix_in_kernel(x_ref, m_ref, g_ref, w_ref, cos_ref, sin_ref,
                   f_ref, q_ref, k_ref, v_ref, r_ref, rk_ref, rv_ref, lora_ref, zg_ref):
    x = x_ref[0]
    m = m_ref[0]
    h = _rms(x, g_ref[...]) * (1.0 + m[1:2]) + m[0:1]
    z = _dot(h.astype(BF16), w_ref[...])

    def col(name):
        lo, hi = MIX_OFF[name]
        return z[:, lo:hi]

    cos = cos_ref[...]
    sin = sin_ref[...]
    cos3 = jnp.concatenate([cos, cos, cos], axis=1)
    sin3 = jnp.concatenate([sin, sin, sin], axis=1)
    cos2 = jnp.concatenate([cos, cos], axis=1)
    sin2 = jnp.concatenate([sin, sin], axis=1)
    f_ref[0] = col("f")
    q_ref[0] = ((col("q") * cos3 + col("qr") * sin3) * (HEAD_DIM ** -0.5)).astype(BF16)
    k_ref[0] = (col("k") * cos2 + col("kr") * sin2).astype(BF16)
    v_ref[0] = col("v").astype(BF16)
    r_ref[0] = col("r")
    rk_ref[0] = col("rk")
    rv_ref[0] = col("rv")
    lora_ref[0] = col("lora")
    zg_ref[0] = col("g")


def _mix_in(x, m3, g1, w_all, cos, sin, tm):
    b, n, d = x.shape
    widths = (FOURIER_W, ATTN_W, 2 * KV_W, 2 * KV_W, J_W, J_W, I_W, 4 * DECAY_LORA, GATE_LORA)
    dtypes = (F32, BF16, BF16, BF16, F32, F32, F32, F32, F32)
    return pl.pallas_call(
        _mix_in_kernel,
        out_shape=[jax.ShapeDtypeStruct((b, n, w), dt) for w, dt in zip(widths, dtypes)],
        grid=(b, n // tm),
        in_specs=[pl.BlockSpec((1, tm, d), lambda i, j: (i, j, 0)),
                  pl.BlockSpec((1, 3, d), lambda i, j: (i, 0, 0)),
                  _const_spec((1, d)),
                  _const_spec((d, MIX_TOTAL)),
                  pl.BlockSpec((tm, LANES), lambda i, j: (j, 0)),
                  pl.BlockSpec((tm, LANES), lambda i, j: (j, 0))],
        out_specs=[pl.BlockSpec((1, tm, w), lambda i, j: (i, j, 0)) for w in widths],
        compiler_params=_cparams(("arbitrary", "arbitrary")),
        name="mix_in",
    )(x, m3, g1, w_all, cos, sin)


def _fourier_stage1_kernel(l_ref, x_ref, o_ref):
    o_ref[0] = _dot(l_ref[...], x_ref[0].astype(BF16))


def _fourier_stage2_kernel(a_ref, mc_ref, ms_ref, cc_ref, sc_ref, o_ref, *, scale):
    ar = a_ref[0, 0, 0].astype(BF16)
    ai = a_ref[0, 1, 0].astype(BF16)
    mc = mc_ref[0]
    ms = ms_ref[0]
    gr = _dot(mc, ar) + _dot(ms, ai)
    gi = _dot(mc, ai) - _dot(ms, ar)
    o_ref[0] = (_dot(gr.astype(BF16), cc_ref[...]) + _dot(gi.astype(BF16), sc_ref[...])) * scale


def _fourier_small_kernel(z_ref, cn_ref, sn_ref, cc_ref, sc_ref, o_ref, *, scale):
    z = z_ref[0].astype(BF16)
    t1 = _dot(z, cc_ref[...]).astype(BF16)
    t2 = _dot(z, sc_ref[...]).astype(BF16)
    o_ref[0] = (_dot(cn_ref[...], t1) - _dot(sn_ref[...], t2)) * scale


def _channel_dft():
    gw = FOURIER_W // FOURIER_GROUPS
    c = np.arange(FOURIER_W)
    same = (c[:, None] // gw) == (c[None, :] // gw)
    ang = 2.0 * np.pi * ((c[:, None] % gw) * (c[None, :] % gw) % gw) / gw
    return (np.cos(ang) * same).astype(np.float32), (np.sin(ang) * same).astype(np.float32)


def _fourier_mix(z):
    b, n, w = z.shape
    gw = w // FOURIER_GROUPS
    scale = float(1.0 / np.sqrt(n * gw))
    cc, sc = _channel_dft()
    cc = jnp.asarray(cc, BF16)
    sc = jnp.asarray(sc, BF16)
    if n <= FOURIER_DIRECT_MAX:
        p = np.arange(n)
        ang = 2.0 * np.pi * ((p[:, None] * p[None, :]) % n) / n
        return pl.pallas_call(
            functools.partial(_fourier_small_kernel, scale=scale),
            out_shape=jax.ShapeDtypeStruct((b, n, w), F32),
            grid=(b,),
            in_specs=[pl.BlockSpec((1, n, w), lambda i: (i, 0, 0)),
                      _const_spec((n, n)), _const_spec((n, n)), _const_spec((w, w)), _const_spec((w, w))],
            out_specs=pl.BlockSpec((1, n, w), lambda i: (i, 0, 0)),
            compiler_params=_cparams(("arbitrary",)),
            name="fourier_small",
        )(z, jnp.asarray(np.cos(ang), BF16), jnp.asarray(np.sin(ang), BF16), cc, sc)
    n1 = FOURIER_N1
    n2 = n // n1
    k1 = np.arange(n1)
    ang1 = 2.0 * np.pi * ((k1[:, None] * k1[None, :]) % n1) / n1
    lhs1 = jnp.asarray(np.concatenate([np.cos(ang1), -np.sin(ang1)], axis=0), BF16)
    tc = min(n2 * w, 4096)
    a = pl.pallas_call(
        _fourier_stage1_kernel,
        out_shape=jax.ShapeDtypeStruct((b, 2 * n1, n2 * w), F32),
        grid=(b, (n2 * w) // tc),
        in_specs=[_const_spec((2 * n1, n1)),
                  pl.BlockSpec((1, n1, tc), lambda i, j: (i, 0, j))],
        out_specs=pl.BlockSpec((1, 2 * n1, tc), lambda i, j: (i, 0, j)),
        compiler_params=_cparams(("arbitrary", "arbitrary")),
        name="fourier_stage1",
    )(lhs1, z.reshape(b, n1, n2 * w))
    k2 = np.arange(n2)
    freq = (k1[:, None, None] + n1 * k2[None, :, None]) * k2[None, None, :]
    ang2 = 2.0 * np.pi * (freq % n) / n
    out = pl.pallas_call(
        functools.partial(_fourier_stage2_kernel, scale=scale),
        out_shape=jax.ShapeDtypeStruct((b, n2, n1 * w), F32),
        grid=(b, n1),
        in_specs=[pl.BlockSpec((1, 2, 1, n2, w), lambda i, j: (i, 0, j, 0, 0)),
                  pl.BlockSpec((1, n2, n2), lambda i, j: (j, 0, 0)),
                  pl.BlockSpec((1, n2, n2), lambda i, j: (j, 0, 0)),
                  _const_spec((w, w)), _const_spec((w, w))],
        out_specs=pl.BlockSpec((1, n2, w), lambda i, j: (i, 0, j)),
        compiler_params=_cparams(("arbitrary", "arbitrary")),
        name="fourier_stage2",
    )(a.reshape(b, 2, n1, n2, w), jnp.asarray(np.cos(ang2), BF16), jnp.asarray(np.sin(ang2), BF16), cc, sc)
    return out.reshape(b, n, w)


NEG = -1e30


def _attn_kernel(*refs, window):
    if window:
        (sink_ref, q_ref, kp_ref, kc_ref, kn_ref, vp_ref, vc_ref, vn_ref, kx_ref, vx_ref, o_ref) = refs
    else:
        (sink_ref, q_ref, kx_ref, vx_ref, o_ref) = refs
    i = pl.program_id(1)
    nb = pl.num_programs(1)
    tq = q_ref.shape[1]
    lane = lax.broadcasted_iota(jnp.int32, (1, KV_W), 1)
    low = lane < HEAD_DIM
    if window:
        k_all = jnp.concatenate([kp_ref[0], kc_ref[0], kn_ref[0], kx_ref[0]], axis=0)
        v_all = jnp.concatenate([vp_ref[0], vc_ref[0], vn_ref[0], vx_ref[0]], axis=0)
        row = lax.broadcasted_iota(jnp.int32, (tq, WINDOW), 0)
        col = lax.broadcasted_iota(jnp.int32, (tq, WINDOW), 1)
        prev_ok = jnp.logical_and(col >= row, i > 0)
        next_ok = jnp.logical_and(col <= row, i < nb - 1)
        mask = jnp.concatenate([jnp.where(prev_ok, 0.0, NEG), jnp.zeros((tq, WINDOW), F32),
                                jnp.where(next_ok, 0.0, NEG), jnp.zeros((tq, kx_ref.shape[1]), F32)], axis=1)
    else:
        k_all = kx_ref[0]
        v_all = vx_ref[0]
        mask = None
    zero = jnp.zeros((), BF16)
    one = jnp.ones((), BF16)
    for pair in range(ATTN_HEADS // 2):
        qp = q_ref[0, :, pair * LANES:(pair + 1) * LANES]
        outs = []
        sinks = []
        for slot in range(2):
            h = 2 * pair + slot
            kv = h // ATTN_GROUP
            swap = slot != kv
            keep = (lane >= HEAD_DIM) if slot else low
            kh = jnp.where(keep, k_all[:, KV_W:] if swap else k_all[:, :KV_W], zero)
            vh = jnp.where(keep, v_all[:, KV_W:] if swap else v_all[:, :KV_W], one)
            s = lax.dot_general(qp, kh, (((1,), (1,)), ((), ())), preferred_element_type=F32)
            if mask is not None:
                s = s + mask
            sk = sink_ref[h]
            mx = jnp.maximum(s.max(axis=-1, keepdims=True), sk)
            outs.append(_dot(jnp.exp(s - mx).astype(BF16), vh))
            sinks.append(jnp.exp(sk - mx))
        num = jnp.where(low, outs[0], outs[1])
        den = pltpu.roll(jnp.where(low, outs[1], outs[0]), HEAD_DIM, 1) + jnp.where(low, sinks[0], sinks[1])
        o_ref[0, :, pair * LANES:(pair + 1) * LANES] = num / den


def _attention(q, k, v, kx, vx, sink8, window):
    b, n, _ = q.shape
    c = kx.shape[1]
    tq = WINDOW
    nb = n // tq
    smem = pl.BlockSpec(memory_space=pltpu.SMEM)
    qspec = pl.BlockSpec((1, tq, ATTN_W), lambda i, j: (i, j, 0))
    xspec = pl.BlockSpec((1, c, 2 * KV_W), lambda i, j: (i, 0, 0))
    if window:
        prev = pl.BlockSpec((1, tq, 2 * KV_W), lambda i, j: (i, jnp.maximum(j - 1, 0), 0))
        cur = pl.BlockSpec((1, tq, 2 * KV_W), lambda i, j: (i, j, 0))
        nxt = pl.BlockSpec((1, tq, 2 * KV_W), lambda i, j: (i, jnp.minimum(j + 1, nb - 1), 0))
        in_specs = [smem, qspec, prev, cur, nxt, prev, cur, nxt, xspec, xspec]
        args = (sink8, q, k, k, k, v, v, v, kx, vx)
    else:
        in_specs = [smem, qspec, xspec, xspec]
        args = (sink8, q, kx, vx)
    return pl.pallas_call(
        functools.partial(_attn_kernel, window=window),
        out_shape=jax.ShapeDtypeStruct((b, n, ATTN_W), F32),
        grid=(b, nb),
        in_specs=in_specs,
        out_specs=pl.BlockSpec((1, tq, ATTN_W), lambda i, j: (i, j, 0)),
        compiler_params=_cparams(("arbitrary", "arbitrary")),
        name="window_attention" if window else "context_attention",
    )(*args)


def _shift_rows(x, prev_row, next_row):
    n = x.shape[0]
    row = lax.broadcasted_iota(jnp.int32, (n, 1), 0)
    xp = jnp.where(row == 0, prev_row, pltpu.roll(x, 1, 0))
    xn = jnp.where(row == n - 1, next_row, pltpu.roll(x, n - 1, 0))
    return xp, xn


def _rwkv_prep_kernel(r_ref, rp_ref, rn_ref, k_ref, kp_ref, kn_ref, v_ref, vp_ref, vn_ref, lora_ref,
                      cwr_ref, cwk_ref, cwv_ref, w0_ref, w2_ref, a0_ref, a2_ref, vec_ref, sjj_ref, sji_ref, tri_ref,
                      vc_ref, bonus_ref, at_ref, bt_ref, kt_ref, rt_ref, gm_ref):
    i = pl.program_id(1)
    first = (i > 0).astype(F32)
    last = (i < pl.num_programs(1) - 1).astype(F32)

    def conv(x_ref, p_ref, n_ref, cw_ref):
        x = x_ref[0]
        xp, xn = _shift_rows(x, p_ref[0, SUBLANES - 1:SUBLANES, :] * first, n_ref[0, 0:1, :] * last)
        cw = cw_ref[...]
        return xp * cw[0:1] + x * cw[1:2] + xn * cw[2:3]

    r = conv(r_ref, rp_ref, rn_ref, cwr_ref)
    k = conv(k_ref, kp_ref, kn_ref, cwk_ref)
    v = conv(v_ref, vp_ref, vn_ref, cwv_ref)
    vec = vec_ref[...]
    sjj = sjj_ref[...]
    sji = sji_ref[...]

    def fold(x):
        return (x[:, 0:LANES] + x[:, LANES:2 * LANES]) + (x[:, 2 * LANES:3 * LANES] + x[:, 3 * LANES:4 * LANES])

    kk = k * vec[0:1]
    ss = _split_dot(fold(kk * kk), sjj, 2)
    inv = lax.rsqrt(jnp.maximum(ss, 1e-24))
    kk = kk * jnp.concatenate([inv, inv, inv, inv], axis=1)
    rk = _split_dot(fold(r * k * vec[2:3]), sji, 2)
    vc_ref[0] = v
    bonus_ref[0] = v * jnp.concatenate([rk] * SUBLANES, axis=1)
    lora = lora_ref[0]
    zw = jnp.tanh(lora[:, :2 * DECAY_LORA])
    za = lora[:, 2 * DECAY_LORA:]
    for d in range(2):
        wpre = w0_ref[d:d + 1, :] + jnp.dot(zw, w2_ref[d], preferred_element_type=F32,
                                            precision=lax.Precision.HIGHEST)
        apre = a0_ref[d:d + 1, :] + jnp.dot(za, a2_ref[d], preferred_element_type=F32,
                                            precision=lax.Precision.HIGHEST)
        nw = -wpre
        softplus = jnp.maximum(nw, 0.0) + jnp.log(1.0 + jnp.exp(-jnp.abs(nw)))
        w = -softplus - 0.5
        a = _sigmoid(apre)
        log_decay = -jnp.exp(w)
        tri = tri_ref[d]
        hi = log_decay.astype(BF16)
        rest = log_decay - hi.astype(F32)
        mid = rest.astype(BF16)
        low = (rest - mid.astype(F32)).astype(BF16)
        log_g = _dot(tri, hi) + _dot(tri, mid) + _dot(tri, low)
        g = jnp.exp(log_g)
        g_inv = jnp.exp(-log_g)
        at_ref[d, 0] = -kk * jnp.exp(log_g - log_decay)
        bt_ref[d, 0] = kk * a * g_inv
        kt_ref[d, 0] = k * (1.0 + (a - 1.0) * vec[1:2]) * g_inv
        rt_ref[d, 0] = r * g
        gm_ref[d, 0] = g


def _rwkv_prep(r, k, v, lora, cwr, cwk, cwv, w0, w2, a0, a2, vec, sjj, sji, tm):
    b, n, _ = r.shape
    nh = n // SUBLANES

    def tile(w):
        return pl.BlockSpec((1, tm, w), lambda i, j: (i, j, 0))

    def halo_prev(w):
        return pl.BlockSpec((1, SUBLANES, w), lambda i, j: (i, jnp.maximum(j * (tm // SUBLANES) - 1, 0), 0))

    def halo_next(w):
        return pl.BlockSpec((1, SUBLANES, w), lambda i, j: (i, jnp.minimum((j + 1) * (tm // SUBLANES), nh - 1), 0))

    def dir_tile(w):
        return pl.BlockSpec((2, 1, tm, w), lambda i, j: (0, i, j, 0))

    t = np.arange(tm)
    same = (t[:, None] // SCAN_T) == (t[None, :] // SCAN_T)
    tri = jnp.asarray(np.stack([same & (t[None, :] <= t[:, None]), same & (t[None, :] >= t[:, None])]), BF16)
    out_shape = ([jax.ShapeDtypeStruct((b, n, I_W), F32)] * 2
                 + [jax.ShapeDtypeStruct((2, b, n, J_W), F32)] * 5)
    return pl.pallas_call(
        _rwkv_prep_kernel,
        out_shape=out_shape,
        grid=(b, n // tm),
        in_specs=[tile(J_W), halo_prev(J_W), halo_next(J_W),
                  tile(J_W), halo_prev(J_W), halo_next(J_W),
                  tile(I_W), halo_prev(I_W), halo_next(I_W),
                  tile(4 * DECAY_LORA),
                  _const_spec((CONV_W, J_W)), _const_spec((CONV_W, J_W)), _const_spec((CONV_W, I_W)),
                  _const_spec((2, J_W)), _const_spec((2, 2 * DECAY_LORA, J_W)),
                  _const_spec((2, J_W)), _const_spec((2, 2 * ICLR_LORA, J_W)),
                  _const_spec((3, J_W)), _const_spec((LANES, LANES)), _const_spec((LANES, LANES)),
                  _const_spec((2, tm, tm))],
        out_specs=[tile(I_W), tile(I_W)] + [dir_tile(J_W)] * 5,
        compiler_params=_cparams(("arbitrary", "arbitrary")),
        name="rwkv_prep",
    )(r, r, r, k, k, k, v, v, v, lora, cwr, cwk, cwv, w0, w2, a0, a2, vec, sjj, sji, tri)


N_ROWVEC = 4
N_DIR_REFS = N_ROWVEC + 2
SCAN_PASSES = 1
DECAY_PASSES = 3


def _scan_kernel(*refs, t_chunk, n_batch):
    ins = refs[:2 * N_DIR_REFS]
    rexp_ref, s0_ref = refs[2 * N_DIR_REFS:2 * N_DIR_REFS + 2]
    y_refs = refs[2 * N_DIR_REFS + 2:2 * N_DIR_REFS + 4]
    sout_ref = refs[2 * N_DIR_REFS + 4]
    e_ref, s_ref, vbuf_ref, ybuf_ref = refs[2 * N_DIR_REFS + 5:]
    step_id = pl.program_id(0)
    assert n_batch == 2

    @pl.when(step_id == 0)
    def _():
        s_ref[...] = s0_ref[...]

    rexp = rexp_ref[...]

    def expand(both, passes):
        rows = pltpu.einshape("t(kl)->tkl", both, l=LANES).reshape(both.shape[0] * SUBLANES, LANES)
        return _split_dot(rows, rexp, passes).reshape(both.shape[0], SUBLANES, E_W)

    for d in range(2):
        for x in range(N_ROWVEC):
            ref = ins[d * N_DIR_REFS + x]
            e_ref[d, x] = expand(jnp.concatenate([ref[0, bi] for bi in range(n_batch)], axis=1), SCAN_PASSES)
        for bi in range(n_batch):
            vbuf_ref[d * n_batch + bi] = pltpu.einshape("t(kl)->tkl", ins[d * N_DIR_REFS + N_ROWVEC + 1][bi], l=LANES)

    lane = lax.broadcasted_iota(jnp.int32, (SUBLANES, LANES), 1)
    q_even = (lane // 32) % 2 == 0

    def allparts(p):
        return (p + pltpu.roll(p, 64, 1)) + (pltpu.roll(p, 32, 1) + pltpu.roll(p, 96, 1))

    def step(t, d, bi):
        g = d * n_batch + bi
        vp = vbuf_ref[g, t]
        vr = pltpu.roll(vp, 32, 1)
        v = (jnp.where(q_even, vp, vr), jnp.where(q_even, vr, vp))

        def row(x, jj):
            r0 = bi * 4 + jj // 4
            return e_ref[d, x, t, r0:r0 + 1, (jj % 4) * LANES:(jj % 4 + 1) * LANES]

        def add(acc, ih, jj, term):
            k = (ih, jj % 4)
            acc[k] = term if k not in acc else acc[k] + term

        def total(acc, ih):
            return allparts((acc[ih, 0] + acc[ih, 1]) + (acc[ih, 2] + acc[ih, 3]))

        acc = {}
        for jj in range(JJ):
            a = row(0, jj)
            for ih in range(I_HI):
                add(acc, ih, jj, s_ref[g, ih, jj] * a)
        sa = [total(acc, ih) for ih in range(I_HI)]
        acc = {}
        for jj in range(JJ):
            b, k, rr = row(1, jj), row(2, jj), row(3, jj)
            for ih in range(I_HI):
                new = s_ref[g, ih, jj] + sa[ih] * b + v[ih] * k
                s_ref[g, ih, jj] = new
                add(acc, ih, jj, new * rr)
        y = [total(acc, ih) for ih in range(I_HI)]
        ybuf_ref[g, t] = jnp.where(q_even, y[0], y[1])

    def one_token(s, carry):
        for d in range(2):
            for bi in range(n_batch):
                step(s if d == 0 else t_chunk - 1 - s, d, bi)
        return carry

    lax.fori_loop(0, t_chunk, one_token, 0, unroll=SCAN_UNROLL)

    for d in range(2):
        for bi in range(n_batch):
            y_refs[d][bi] = pltpu.einshape("tkl->t(kl)", ybuf_ref[d * n_batch + bi])
        gm_ref = ins[d * N_DIR_REFS + N_ROWVEC]
        tile0, tok = (t_chunk - SUBLANES, SUBLANES - 1) if d == 0 else (0, 0)
        g_rows = expand(jnp.concatenate([gm_ref[0, bi, tile0:tile0 + SUBLANES, :] for bi in range(n_batch)], axis=1),
                        DECAY_PASSES)[tok]
        for bi in range(n_batch):
            for jj in range(JJ):
                r0 = bi * 4 + jj // 4
                scale = g_rows[r0:r0 + 1, (jj % 4) * LANES:(jj % 4 + 1) * LANES]
                for ih in range(I_HI):
                    s_ref[d * n_batch + bi, ih, jj] = s_ref[d * n_batch + bi, ih, jj] * scale

    @pl.when(step_id == pl.num_programs(0) - 1)
    def _():
        sout_ref[...] = s_ref[...]


def _rwkv_scan(vc, at, bt, kt, rt, gm, state0, rexp):
    b, n, _ = vc.shape
    t = SCAN_T
    nc = n // t
    in_specs = []
    args = []
    out_specs = []
    for d in range(2):
        def chunk(i, d=d):
            return i if d == 0 else nc - 1 - i

        perdir = pl.BlockSpec((1, b, t, J_W), lambda i, d=d, chunk=chunk: (d, 0, chunk(i), 0))
        ispec = pl.BlockSpec((b, t, I_W), lambda i, chunk=chunk: (0, chunk(i), 0))
        in_specs += [perdir] * (N_ROWVEC + 1) + [ispec]
        args += [at, bt, kt, rt, gm, vc]
        out_specs.append(ispec)
    state_shape = (2 * b, I_HI, JJ, SUBLANES, LANES)
    in_specs += [_const_spec((LANES, E_W)), _const_spec(state_shape)]
    args += [rexp, state0]
    out_specs.append(_const_spec(state_shape))
    y_shape = jax.ShapeDtypeStruct((b, n, I_W), F32)
    return pl.pallas_call(
        functools.partial(_scan_kernel, t_chunk=t, n_batch=b),
        out_shape=[y_shape, y_shape, jax.ShapeDtypeStruct(state_shape, F32)],
        grid=(nc,),
        in_specs=in_specs,
        out_specs=out_specs,
        scratch_shapes=[pltpu.VMEM((2, N_ROWVEC, t, SUBLANES, E_W), F32),
                        pltpu.VMEM(state_shape, F32),
                        pltpu.VMEM((2 * b, t, SUBLANES, LANES), F32),
                        pltpu.VMEM((2 * b, t, SUBLANES, LANES), F32)],
        compiler_params=_cparams(("arbitrary",)),
        name="rwkv_scan",
    )(*args)


def _mix_out_kernel(x_ref, m_ref, g_ref, f_ref, a_ref, yf_ref, yb_ref, bonus_ref, zg_ref,
                    g2_ref, ln_ref, sii_ref, wf_ref, wa_ref, wr_ref, o_ref):
    x = x_ref[0]
    y = yf_ref[0] + yb_ref[0]
    sii = sii_ref[...]
    nblk = I_W // LANES
    cnt = float(2 * HEAD_DIM)

    def fold(z):
        acc = z[:, 0:LANES]
        for c in range(1, nblk):
            acc = acc + z[:, c * LANES:(c + 1) * LANES]
        return acc

    def spread(z):
        return jnp.concatenate([z] * nblk, axis=1)

    mu = _split_dot(fold(y), sii, 2) * (1.0 / cnt)
    dlt = y - spread(mu)
    var = _split_dot(fold(dlt * dlt), sii, 2) * (1.0 / cnt)
    ln = ln_ref[...]
    yn = dlt * spread(lax.rsqrt(var + GN_EPS)) * ln[0:1] + ln[1:2]
    gate = _dot(_sigmoid(zg_ref[0]).astype(BF16), g2_ref[...])
    rw = ((yn + bonus_ref[0]) * gate).astype(BF16)
    o = _dot(f_ref[0].astype(BF16), wf_ref[...]) + _dot(a_ref[0].astype(BF16), wa_ref[...]) + _dot(rw, wr_ref[...])
    o_ref[0] = x + m_ref[0] * _rms(o, g_ref[...])


def _mix_out(x, gate_row, g3, fo, ao, y_f, y_b, bonus, zg, g2p, ln, sii, wf, wa, wr, tm):
    b, n, d = x.shape

    def tile(w):
        return pl.BlockSpec((1, tm, w), lambda i, j: (i, j, 0))

    return pl.pallas_call(
        _mix_out_kernel,
        out_shape=jax.ShapeDtypeStruct(x.shape, F32),
        grid=(b, n // tm),
        in_specs=[tile(d), pl.BlockSpec((1, 1, d), lambda i, j: (i, 0, 0)), _const_spec((1, d)),
                  tile(FOURIER_W), tile(ATTN_W),
                  tile(I_W), tile(I_W), tile(I_W), tile(GATE_LORA),
                  _const_spec((GATE_LORA, I_W)), _const_spec((2, I_W)), _const_spec((LANES, LANES)),
                  _const_spec((FOURIER_W, d)), _const_spec((ATTN_W, d)), _const_spec((I_W, d))],
        out_specs=tile(d),
        compiler_params=_cparams(("arbitrary", "arbitrary")),
        name="mix_out",
    )(x, gate_row, g3, fo, ao, y_f, y_b, bonus, zg, g2p, ln, sii, wf, wa, wr)


def _rope_tables(n_tokens):
    rows_n = n_tokens // GRID_W
    row = jnp.repeat(jnp.arange(rows_n), GRID_W).astype(F32)
    colp = jnp.tile(jnp.arange(GRID_W), rows_n).astype(F32)
    n_freq = HEAD_DIM // 4
    inv = ROPE_BASE ** (-jnp.arange(n_freq, dtype=F32) / n_freq)
    ang = jnp.concatenate([row[:, None] * inv, colp[:, None] * inv], axis=-1)
    cos = jnp.tile(jnp.cos(ang), (1, 4))
    sin = jnp.tile(jnp.sin(ang), (1, 4))
    return cos, sin


def _rot_cols(w):
    d, n = w.shape
    w4 = w.reshape(d, n // HEAD_DIM, 2, HEAD_DIM // 2)
    return jnp.stack([-w4[:, :, 1], w4[:, :, 0]], axis=2).reshape(d, n)


def _layer_weights(li, p):
    f = p["ffn1_wo"].shape[1]
    fp = -(-f // (2 * LANES)) * (2 * LANES)
    out = {}
    for name in ("ffn1", "ffn2"):
        wi = p[name + "_wi"][li]
        wo = p[name + "_wo"][li]
        out[name] = (jnp.pad(wi[:, :f], ((0, 0), (0, fp - f))).astype(BF16),
                     jnp.pad(wi[:, f:], ((0, 0), (0, fp - f))).astype(BF16),
                     jnp.pad(wo, ((0, fp - f), (0, 0))).astype(BF16))
    w_in = p["mix_w_in"][li]
    offs = np.cumsum([0, FOURIER_W, ATTN_W, KV_W, KV_W, RWKV_W, RWKV_W, RWKV_W,
                      DECAY_LORA, DECAY_LORA, ICLR_LORA, ICLR_LORA, GATE_LORA])
    part = [w_in[:, offs[i]:offs[i + 1]] for i in range(12)]
    wf, wq, wk, wv, wrr, wrk, wrv = part[:7]
    swap = lambda w: jnp.concatenate([w[:, HEAD_DIM:], w[:, :HEAD_DIM]], axis=1)
    cols = {"f": wf, "q": wq, "qr": _rot_cols(wq),
            "k": jnp.concatenate([wk, swap(wk)], axis=1),
            "kr": jnp.concatenate([_rot_cols(wk), swap(_rot_cols(wk))], axis=1),
            "v": jnp.concatenate([wv, swap(wv)], axis=1),
            "r": _take_cols(wrr, J_IDX), "rk": _take_cols(wrk, J_IDX), "rv": _take_cols(wrv, I_IDX),
            "lora": jnp.concatenate(part[7:11], axis=1), "g": part[11]}
    out["w_all"] = jnp.concatenate([cols[n] for n, _ in MIX_COLS], axis=1).astype(BF16)
    conv = p["rwkv_conv"][li]
    out["cwr"] = _take_cols(conv[:, :RWKV_W], J_IDX)
    out["cwk"] = _take_cols(conv[:, RWKV_W:2 * RWKV_W], J_IDX)
    out["cwv"] = _take_cols(conv[:, 2 * RWKV_W:], I_IDX)
    def per_direction(w2):
        w2 = _take_cols(w2, J_IDX)
        z = jnp.zeros_like(w2[0])
        return jnp.stack([jnp.concatenate([w2[0], z], axis=0), jnp.concatenate([z, w2[1]], axis=0)])

    out["w0"] = _take_cols(p["rwkv_w0"][li], J_IDX)
    out["w2"] = per_direction(p["rwkv_w2"][li])
    out["a0"] = _take_cols(p["rwkv_a0"][li], J_IDX)
    out["a2"] = per_direction(p["rwkv_a2"][li])
    out["vec"] = _take_cols(jnp.stack([p["rwkv_k_k"][li], p["rwkv_k_a"][li], p["rwkv_r_k"][li].reshape(-1)]), J_IDX)
    out["g2"] = _take_cols(p["rwkv_g2"][li], I_IDX).astype(BF16)
    out["ln"] = _take_cols(jnp.stack([p["rwkv_ln_g"][li], p["rwkv_ln_b"][li]]), I_IDX)
    w_out = p["mix_w_out"][li]
    out["wo_f"] = w_out[:FOURIER_W].astype(BF16)
    out["wo_a"] = w_out[FOURIER_W:FOURIER_W + ATTN_W].astype(BF16)
    out["wo_r"] = _take_rows(w_out[FOURIER_W + ATTN_W:], I_IDX_ONCE).astype(BF16)
    sink = p["attn_sink"][li]
    out["sink"] = jnp.concatenate([sink, jnp.zeros((8 - ATTN_HEADS,), F32)])
    return out


def kernel(x, c, ctx, c_ctx, mod_w, mod_b, norm_g, ffn1_wi, ffn1_wo, mix_w_in, mix_w_out, attn_sink,
           rwkv_conv, rwkv_w0, rwkv_w2, rwkv_a0, rwkv_a2, rwkv_g2, rwkv_k_k, rwkv_k_a, rwkv_r_k,
           rwkv_ln_g, rwkv_ln_b, ffn2_wi, ffn2_wo):
    p = dict(ffn1_wi=ffn1_wi, ffn1_wo=ffn1_wo, ffn2_wi=ffn2_wi, ffn2_wo=ffn2_wo, mix_w_in=mix_w_in,
             mix_w_out=mix_w_out, attn_sink=attn_sink, rwkv_conv=rwkv_conv, rwkv_w0=rwkv_w0, rwkv_w2=rwkv_w2,
             rwkv_a0=rwkv_a0, rwkv_a2=rwkv_a2, rwkv_g2=rwkv_g2, rwkv_k_k=rwkv_k_k, rwkv_k_a=rwkv_k_a,
             rwkv_r_k=rwkv_r_k, rwkv_ln_g=rwkv_ln_g, rwkv_ln_b=rwkv_ln_b)
    b, s, d = x.shape
    n_c = ctx.shape[1]
    depth = mod_w.shape[0]
    assert b + 1 <= 8 and s % WINDOW == 0 and n_c % WINDOW == 0 and WINDOW % SCAN_T == 0
    tm_l = 256
    tm_c = min(256, n_c)

    cond8 = jnp.zeros((8, d), F32).at[:b].set(c).at[b].set(c_ctx)
    mod = _ada_mod(cond8, mod_w, mod_b).reshape(depth, 8, N_MOD, d)
    cos_l, sin_l = _rope_tables(s)
    cos_c = jnp.ones((n_c, LANES), F32)
    sin_c = jnp.zeros((n_c, LANES), F32)
    rexp = jnp.asarray(_expand_matrix(), BF16)
    sjj, sji, sii = (jnp.asarray(m, BF16) for m in _seg_matrices())
    zero_state = jnp.zeros((2 * b, I_HI, JJ, SUBLANES, LANES), F32)

    xl, xc = x, ctx
    for li in range(depth):
        need_ctx_out = li < depth - 1
        w = _layer_weights(li, p)
        ml = mod[li, :b]
        mc = jnp.broadcast_to(mod[li, b:b + 1], (b, N_MOD, d))
        g = norm_g[li]
        xl = _ffn_half(xl, ml[:, 0:3], g[0:2], *w["ffn1"], tm_l)
        xc = _ffn_half(xc, mc[:, 0:3], g[0:2], *w["ffn1"], tm_c)

        def mixer_in(xx, mm, cos, sin, tm):
            return _mix_in(xx, mm[:, 3:6], g[2:3], w["w_all"], cos, sin, tm)

        fl, ql, kl, vl, rl, rkl, rvl, loral, zgl = mixer_in(xl, ml, cos_l, sin_l, tm_l)
        fc, qc, kc, vc, rc_, rkc, rvc, lorac, zgc = mixer_in(xc, mc, cos_c, sin_c, tm_c)

        attn_l = _attention(ql, kl, vl, kc, vc, w["sink"], True)

        def prep(r_, k_, v_, lora_, tm):
            return _rwkv_prep(r_, k_, v_, lora_, w["cwr"], w["cwk"], w["cwv"], w["w0"], w["w2"],
                              w["a0"], w["a2"], w["vec"], sjj, sji, tm)

        pc = prep(rc_, rkc, rvc, lorac, tm_c)
        plat = prep(rl, rkl, rvl, loral, tm_l)
        ycf, ycb, state_c = _rwkv_scan(pc[0], *pc[2:], zero_state, rexp)
        ylf, ylb, _ = _rwkv_scan(plat[0], *plat[2:], state_c, rexp)

        def mixer_out(xx, mm, fo, ao, y_f, y_b, bonus, zg, tm):
            return _mix_out(xx, mm[:, 5:6], g[3:4], fo, ao, y_f, y_b, bonus, zg, w["g2"], w["ln"], sii,
                            w["wo_f"], w["wo_a"], w["wo_r"], tm)

        xl = mixer_out(xl, ml, _fourier_mix(fl), attn_l, ylf, ylb, plat[1], zgl, tm_l)
        if need_ctx_out:
            attn_c = _attention(qc, None, None, kc, vc, w["sink"], False)
            xc = mixer_out(xc, mc, _fourier_mix(fc), attn_c, ycf, ycb, pc[1], zgc, tm_c)
            xc = _ffn_half(xc, mc[:, 6:9], g[4:6], *w["ffn2"], tm_c)
        xl = _ffn_half(xl, ml[:, 6:9], g[4:6], *w["ffn2"], tm_l)
    return xl
```

```python
import functools

import numpy as np
import jax
import jax.numpy as jnp
from jax import lax
from jax.experimental import pallas as pl
from jax.experimental.pallas import tpu as pltpu

F32 = jnp.float32
BF16 = jnp.bfloat16

HEAD_DIM = 64
GRID_W = 64
FOURIER_W = 256
FOURIER_GROUPS = 4
ATTN_HEADS = 6
ATTN_KV_HEADS = 2
ATTN_GROUP = ATTN_HEADS // ATTN_KV_HEADS
ATTN_W = ATTN_HEADS * HEAD_DIM
KV_W = ATTN_KV_HEADS * HEAD_DIM
RWKV_HEADS = 6
RWKV_W = RWKV_HEADS * HEAD_DIM
WINDOW = 128
ROPE_BASE = 10000.0
DECAY_LORA = 64
ICLR_LORA = 64
GATE_LORA = 128
CONV_W = 3
N_MOD = 9
NORM_EPS = 1e-6
GN_EPS = 64e-5

LANES = 128
SUBLANES = 8
VMEM_LIMIT = 56 * 1024 * 1024

HEAD_SLOTS = 8
J_PARTS = 4
I_REP = 4
JJ = HEAD_DIM // J_PARTS
I_HI = HEAD_DIM // (SUBLANES * I_REP)
J_W = 4 * LANES
E_W = 4 * LANES
I_W = SUBLANES * LANES
SCAN_T = 64
SCAN_UNROLL = 8
FOURIER_N1 = 64
FOURIER_DIRECT_MAX = 256


def _j_layout():
    idx = np.full((J_W,), -1, np.int64)
    for jh in range(4):
        for jl in range(4):
            for part in range(J_PARTS):
                for h in range(RWKV_HEADS):
                    idx[jh * 128 + jl * 32 + part * 8 + h] = h * HEAD_DIM + part * JJ + jh * 4 + jl
    return idx


def _i_layout(dup):
    idx = np.full((I_W,), -1, np.int64)
    for isub in range(8):
        for q in range(4 if dup else 2):
            for h in range(RWKV_HEADS):
                for il in range(I_REP):
                    idx[isub * 128 + q * 32 + h * 4 + il] = h * HEAD_DIM + (q % 2) * 32 + isub * 4 + il
    return idx


J_IDX = _j_layout()
I_IDX = _i_layout(True)
I_IDX_ONCE = _i_layout(False)


def _take_cols(w, idx):
    g = jnp.take(w, jnp.asarray(np.maximum(idx, 0)), axis=-1)
    return g * jnp.asarray((idx >= 0).astype(np.float32))


def _take_rows(w, idx):
    g = jnp.take(w, jnp.asarray(np.maximum(idx, 0)), axis=0)
    return g * jnp.asarray((idx >= 0).astype(np.float32))[:, None]


def _expand_matrix():
    m = np.zeros((LANES, E_W), np.float32)
    for jl in range(4):
        for part in range(J_PARTS):
            for h in range(HEAD_SLOTS):
                for rep in range(I_REP):
                    m[jl * 32 + part * 8 + h, jl * 128 + part * 32 + h * I_REP + rep] = 1.0
    return m


def _seg_matrices():
    hj = np.arange(LANES) % 8
    hi = (np.arange(LANES) // I_REP) % 8
    jj = (hj[:, None] == hj[None, :]).astype(np.float32)
    ji = (hj[:, None] == hi[None, :]).astype(np.float32)
    ii = (hi[:, None] == hi[None, :]).astype(np.float32)
    return jj, ji, ii


def _cparams(sem, vmem=VMEM_LIMIT):
    return pltpu.CompilerParams(dimension_semantics=sem, vmem_limit_bytes=vmem)


def _const_spec(shape):
    nd = len(shape)
    return pl.BlockSpec(shape, lambda *_: (0,) * nd)


def _sigmoid(x):
    return 1.0 / (1.0 + jnp.exp(-x))


def _rms(x, g):
    return x * lax.rsqrt(jnp.mean(x * x, axis=-1, keepdims=True) + NORM_EPS) * g


def _dot(a, b):
    return jnp.dot(a, b, preferred_element_type=F32)


def _split_dot(x, m, passes):
    acc = None
    rem = x
    for p in range(passes):
        piece = rem.astype(BF16)
        d = _dot(piece, m)
        acc = d if acc is None else acc + d
        if p + 1 < passes:
            rem = rem - piece.astype(F32)
    return acc


def _ada_kernel(c_ref, w_ref, b_ref, o_ref):
    c = c_ref[...]
    s = c * _sigmoid(c)
    o_ref[0] = _dot(s.astype(BF16), w_ref[0].astype(BF16)) + b_ref[0]


def _ada_mod(cond8, mod_w, mod_b):
    nl, d, nw = mod_w.shape
    tn = nw // 8
    return pl.pallas_call(
        _ada_kernel,
        out_shape=jax.ShapeDtypeStruct((nl, 8, nw), F32),
        grid=(nl, nw // tn),
        in_specs=[pl.BlockSpec((8, d), lambda l, j: (0, 0)),
                  pl.BlockSpec((1, d, tn), lambda l, j: (l, 0, j)),
                  pl.BlockSpec((1, 1, tn), lambda l, j: (l, 0, j))],
        out_specs=pl.BlockSpec((1, 8, tn), lambda l, j: (l, 0, j)),
        compiler_params=_cparams(("arbitrary", "arbitrary")),
        name="ada_mod",
    )(cond8, mod_w, mod_b.reshape(nl, 1, nw))


def _ffn_kernel(x_ref, m_ref, g_ref, wg_ref, wu_ref, wo_ref, o_ref):
    x = x_ref[0]
    m = m_ref[0]
    g = g_ref[...]
    h = _rms(x, g[0:1]) * (1.0 + m[1:2]) + m[0:1]
    hb = h.astype(BF16)
    gate = _dot(hb, wg_ref[...])
    up = _dot(hb, wu_ref[...])
    act = (gate * _sigmoid(gate) * up).astype(BF16)
    y = _dot(act, wo_ref[...])
    o_ref[0] = x + 0.5 * m[2:3] * _rms(y, g[1:2])


def _ffn_half(x, m3, g2, wg, wu, wo, tm):
    b, n, d = x.shape
    fp = wg.shape[1]
    return pl.pallas_call(
        _ffn_kernel,
        out_shape=jax.ShapeDtypeStruct(x.shape, F32),
        grid=(b, n // tm),
        in_specs=[pl.BlockSpec((1, tm, d), lambda i, j: (i, j, 0)),
                  pl.BlockSpec((1, 3, d), lambda i, j: (i, 0, 0)),
                  _const_spec((2, d)),
                  _const_spec((d, fp)), _const_spec((d, fp)), _const_spec((fp, d))],
        out_specs=pl.BlockSpec((1, tm, d), lambda i, j: (i, j, 0)),
        compiler_params=_cparams(("arbitrary", "arbitrary")),
        name="ffn_half",
    )(x, m3, g2, wg, wu, wo)


MIX_COLS = (("f", FOURIER_W), ("q", ATTN_W), ("qr", ATTN_W), ("k", 2 * KV_W), ("kr", 2 * KV_W),
            ("v", 2 * KV_W), ("r", J_W), ("rk", J_W), ("rv", I_W), ("lora", 4 * DECAY_LORA), ("g", GATE_LORA))
MIX_OFF = {}
_o = 0
for _n, _w in MIX_COLS:
    MIX_OFF[_n] = (_o, _o + _w)
    _o += _w
MIX_TOTAL = _o


def _mix_in_kernel(x_ref, m_ref, g_ref, w_ref, cos_ref, sin_ref,
                   f_ref, q_ref, k_ref, v_ref, r_ref, rk_ref, rv_ref, lora_ref, zg_ref):
    x = x_ref[0]
    m = m_ref[0]
    h = _rms(x, g_ref[...]) * (1.0 + m[1:2]) + m[0:1]
    z = _dot(h.astype(BF16), w_ref[...])

    def col(name):
        lo, hi = MIX_OFF[name]
        return z[:, lo:hi]

    cos = cos_ref[...]
    sin = sin_ref[...]
    cos3 = jnp.concatenate([cos, cos, cos], axis=1)
    sin3 = jnp.concatenate([sin, sin, sin], axis=1)
    cos2 = jnp.concatenate([cos, cos], axis=1)
    sin2 = jnp.concatenate([sin, sin], axis=1)
    f_ref[0] = col("f")
    q_ref[0] = ((col("q") * cos3 + col("qr") * sin3) * (HEAD_DIM ** -0.5)).astype(BF16)
    k_ref[0] = (col("k") * cos2 + col("kr") * sin2).astype(BF16)
    v_ref[0] = col("v").astype(BF16)
    r_ref[0] = col("r")
    rk_ref[0] = col("rk")
    rv_ref[0] = col("rv")
    lora_ref[0] = col("lora")
    zg_ref[0] = col("g")


def _mix_in(x, m3, g1, w_all, cos, sin, tm):
    b, n, d = x.shape
    widths = (FOURIER_W, ATTN_W, 2 * KV_W, 2 * KV_W, J_W, J_W, I_W, 4 * DECAY_LORA, GATE_LORA)
    dtypes = (F32, BF16, BF16, BF16, F32, F32, F32, F32, F32)
    return pl.pallas_call(
        _mix_in_kernel,
        out_shape=[jax.ShapeDtypeStruct((b, n, w), dt) for w, dt in zip(widths, dtypes)],
        grid=(b, n // tm),
        in_specs=[pl.BlockSpec((1, tm, d), lambda i, j: (i, j, 0)),
                  pl.BlockSpec((1, 3, d), lambda i, j: (i, 0, 0)),
                  _const_spec((1, d)),
                  _const_spec((d, MIX_TOTAL)),
                  pl.BlockSpec((tm, LANES), lambda i, j: (j, 0)),
                  pl.BlockSpec((tm, LANES), lambda i, j: (j, 0))],
        out_specs=[pl.BlockSpec((1, tm, w), lambda i, j: (i, j, 0)) for w in widths],
        compiler_params=_cparams(("arbitrary", "arbitrary")),
        name="mix_in",
    )(x, m3, g1, w_all, cos, sin)


def _fourier_stage1_kernel(l_ref, x_ref, o_ref):
    o_ref[0] = _dot(l_ref[...], x_ref[0].astype(BF16))


def _fourier_stage2_kernel(a_ref, mc_ref, ms_ref, cc_ref, sc_ref, o_ref, *, scale):
    ar = a_ref[0, 0, 0].astype(BF16)
    ai = a_ref[0, 1, 0].astype(BF16)
    mc = mc_ref[0]
    ms = ms_ref[0]
    gr = _dot(mc, ar) + _dot(ms, ai)
    gi = _dot(mc, ai) - _dot(ms, ar)
    o_ref[0] = (_dot(gr.astype(BF16), cc_ref[...]) + _dot(gi.astype(BF16), sc_ref[...])) * scale


def _fourier_small_kernel(z_ref, cn_ref, sn_ref, cc_ref, sc_ref, o_ref, *, scale):
    z = z_ref[0].astype(BF16)
    t1 = _dot(z, cc_ref[...]).astype(BF16)
    t2 = _dot(z, sc_ref[...]).astype(BF16)
    o_ref[0] = (_dot(cn_ref[...], t1) - _dot(sn_ref[...], t2)) * scale


def _channel_dft():
    gw = FOURIER_W // FOURIER_GROUPS
    c = np.arange(FOURIER_W)
    same = (c[:, None] // gw) == (c[None, :] // gw)
    ang = 2.0 * np.pi * ((c[:, None] % gw) * (c[None, :] % gw) % gw) / gw
    return (np.cos(ang) * same).astype(np.float32), (np.sin(ang) * same).astype(np.float32)


def _fourier_mix(z):
    b, n, w = z.shape
    gw = w // FOURIER_GROUPS
    scale = float(1.0 / np.sqrt(n * gw))
    cc, sc = _channel_dft()
    cc = jnp.asarray(cc, BF16)
    sc = jnp.asarray(sc, BF16)
    if n <= FOURIER_DIRECT_MAX:
        p = np.arange(n)
        ang = 2.0 * np.pi * ((p[:, None] * p[None, :]) % n) / n
        return pl.pallas_call(
            functools.partial(_fourier_small_kernel, scale=scale),
            out_shape=jax.ShapeDtypeStruct((b, n, w), F32),
            grid=(b,),
            in_specs=[pl.BlockSpec((1, n, w), lambda i: (i, 0, 0)),
                      _const_spec((n, n)), _const_spec((n, n)), _const_spec((w, w)), _const_spec((w, w))],
            out_specs=pl.BlockSpec((1, n, w), lambda i: (i, 0, 0)),
            compiler_params=_cparams(("arbitrary",)),
            name="fourier_small",
        )(z, jnp.asarray(np.cos(ang), BF16), jnp.asarray(np.sin(ang), BF16), cc, sc)
    n1 = FOURIER_N1
    n2 = n // n1
    k1 = np.arange(n1)
    ang1 = 2.0 * np.pi * ((k1[:, None] * k1[None, :]) % n1) / n1
    lhs1 = jnp.asarray(np.concatenate([np.cos(ang1), -np.sin(ang1)], axis=0), BF16)
    tc = min(n2 * w, 4096)
    a = pl.pallas_call(
        _fourier_stage1_kernel,
        out_shape=jax.ShapeDtypeStruct((b, 2 * n1, n2 * w), F32),
        grid=(b, (n2 * w) // tc),
        in_specs=[_const_spec((2 * n1, n1)),
                  pl.BlockSpec((1, n1, tc), lambda i, j: (i, 0, j))],
        out_specs=pl.BlockSpec((1, 2 * n1, tc), lambda i, j: (i, 0, j)),
        compiler_params=_cparams(("arbitrary", "arbitrary")),
        name="fourier_stage1",
    )(lhs1, z.reshape(b, n1, n2 * w))
    k2 = np.arange(n2)
    freq = (k1[:, None, None] + n1 * k2[None, :, None]) * k2[None, None, :]
    ang2 = 2.0 * np.pi * (freq % n) / n
    out = pl.pallas_call(
        functools.partial(_fourier_stage2_kernel, scale=scale),
        out_shape=jax.ShapeDtypeStruct((b, n2, n1 * w), F32),
        grid=(b, n1),
        in_specs=[pl.BlockSpec((1, 2, 1, n2, w), lambda i, j: (i, 0, j, 0, 0)),
                  pl.BlockSpec((1, n2, n2), lambda i, j: (j, 0, 0)),
                  pl.BlockSpec((1, n2, n2), lambda i, j: (j, 0, 0)),
                  _const_spec((w, w)), _const_spec((w, w))],
        out_specs=pl.BlockSpec((1, n2, w), lambda i, j: (i, 0, j)),
        compiler_params=_cparams(("arbitrary", "arbitrary")),
        name="fourier_stage2",
    )(a.reshape(b, 2, n1, n2, w), jnp.asarray(np.cos(ang2), BF16), jnp.asarray(np.sin(ang2), BF16), cc, sc)
    return out.reshape(b, n, w)


NEG = -1e30


def _attn_kernel(*refs, window):
    if window:
        (sink_ref, q_ref, kp_ref, kc_ref, kn_ref, vp_ref, vc_ref, vn_ref, kx_ref, vx_ref, o_ref) = refs
    else:
        (sink_ref, q_ref, kx_ref, vx_ref, o_ref) = refs
    i = pl.program_id(1)
    nb = pl.num_programs(1)
    tq = q_ref.shape[1]
    lane = lax.broadcasted_iota(jnp.int32, (1, KV_W), 1)
    low = lane < HEAD_DIM
    if window:
        k_all = jnp.concatenate([kp_ref[0], kc_ref[0], kn_ref[0], kx_ref[0]], axis=0)
        v_all = jnp.concatenate([vp_ref[0], vc_ref[0], vn_ref[0], vx_ref[0]], axis=0)
        row = lax.broadcasted_iota(jnp.int32, (tq, WINDOW), 0)
        col = lax.broadcasted_iota(jnp.int32, (tq, WINDOW), 1)
        prev_ok = jnp.logical_and(col >= row, i > 0)
        next_ok = jnp.logical_and(col <= row, i < nb - 1)
        mask = jnp.concatenate([jnp.where(prev_ok, 0.0, NEG), jnp.zeros((tq, WINDOW), F32),
                                jnp.where(next_ok, 0.0, NEG), jnp.zeros((tq, kx_ref.shape[1]), F32)], axis=1)
    else:
        k_all = kx_ref[0]
        v_all = vx_ref[0]
        mask = None
    zero = jnp.zeros((), BF16)
    one = jnp.ones((), BF16)
    for pair in range(ATTN_HEADS // 2):
        qp = q_ref[0, :, pair * LANES:(pair + 1) * LANES]
        outs = []
        sinks = []
        for slot in range(2):
            h = 2 * pair + slot
            kv = h // ATTN_GROUP
            swap = slot != kv
            keep = (lane >= HEAD_DIM) if slot else low
            kh = jnp.where(keep, k_all[:, KV_W:] if swap else k_all[:, :KV_W], zero)
            vh = jnp.where(keep, v_all[:, KV_W:] if swap else v_all[:, :KV_W], one)
            s = lax.dot_general(qp, kh, (((1,), (1,)), ((), ())), preferred_element_type=F32)
            if mask is not None:
                s = s + mask
            sk = sink_ref[h]
            mx = jnp.maximum(s.max(axis=-1, keepdims=True), sk)
            outs.append(_dot(jnp.exp(s - mx).astype(BF16), vh))
            sinks.append(jnp.exp(sk - mx))
        num = jnp.where(low, outs[0], outs[1])
        den = pltpu.roll(jnp.where(low, outs[1], outs[0]), HEAD_DIM, 1) + jnp.where(low, sinks[0], sinks[1])
        o_ref[0, :, pair * LANES:(pair + 1) * LANES] = num / den


def _attention(q, k, v, kx, vx, sink8, window):
    b, n, _ = q.shape
    c = kx.shape[1]
    tq = WINDOW
    nb = n // tq
    smem = pl.BlockSpec(memory_space=pltpu.SMEM)
    qspec = pl.BlockSpec((1, tq, ATTN_W), lambda i, j: (i, j, 0))
    xspec = pl.BlockSpec((1, c, 2 * KV_W), lambda i, j: (i, 0, 0))
    if window:
        prev = pl.BlockSpec((1, tq, 2 * KV_W), lambda i, j: (i, jnp.maximum(j - 1, 0), 0))
        cur = pl.BlockSpec((1, tq, 2 * KV_W), lambda i, j: (i, j, 0))
        nxt = pl.BlockSpec((1, tq, 2 * KV_W), lambda i, j: (i, jnp.minimum(j + 1, nb - 1), 0))
        in_specs = [smem, qspec, prev, cur, nxt, prev, cur, nxt, xspec, xspec]
        args = (sink8, q, k, k, k, v, v, v, kx, vx)
    else:
        in_specs = [smem, qspec, xspec, xspec]
        args = (sink8, q, kx, vx)
    return pl.pallas_call(
        functools.partial(_attn_kernel, window=window),
        out_shape=jax.ShapeDtypeStruct((b, n, ATTN_W), F32),
        grid=(b, nb),
        in_specs=in_specs,
        out_specs=pl.BlockSpec((1, tq, ATTN_W), lambda i, j: (i, j, 0)),
        compiler_params=_cparams(("arbitrary", "arbitrary")),
        name="window_attention" if window else "context_attention",
    )(*args)


def _shift_rows(x, prev_row, next_row):
    n = x.shape[0]
    row = lax.broadcasted_iota(jnp.int32, (n, 1), 0)
    xp = jnp.where(row == 0, prev_row, pltpu.roll(x, 1, 0))
    xn = jnp.where(row == n - 1, next_row, pltpu.roll(x, n - 1, 0))
    return xp, xn


def _rwkv_prep_kernel(r_ref, rp_ref, rn_ref, k_ref, kp_ref, kn_ref, v_ref, vp_ref, vn_ref, lora_ref,
                      cwr_ref, cwk_ref, cwv_ref, w0_ref, w2_ref, a0_ref, a2_ref, vec_ref, sjj_ref, sji_ref, tri_ref,
                      vc_ref, bonus_ref, at_ref, bt_ref, kt_ref, rt_ref, gm_ref):
    i = pl.program_id(1)
    first = (i > 0).astype(F32)
    last = (i < pl.num_programs(1) - 1).astype(F32)

    def conv(x_ref, p_ref, n_ref, cw_ref):
        x = x_ref[0]
        xp, xn = _shift_rows(x, p_ref[0, SUBLANES - 1:SUBLANES, :] * first, n_ref[0, 0:1, :] * last)
        cw = cw_ref[...]
        return xp * cw[0:1] + x * cw[1:2] + xn * cw[2:3]

    r = conv(r_ref, rp_ref, rn_ref, cwr_ref)
    k = conv(k_ref, kp_ref, kn_ref, cwk_ref)
    v = conv(v_ref, vp_ref, vn_ref, cwv_ref)
    vec = vec_ref[...]
    sjj = sjj_ref[...]
    sji = sji_ref[...]

    def fold(x):
        return (x[:, 0:LANES] + x[:, LANES:2 * LANES]) + (x[:, 2 * LANES:3 * LANES] + x[:, 3 * LANES:4 * LANES])

    kk = k * vec[0:1]
    ss = _split_dot(fold(kk * kk), sjj, 2)
    inv = lax.rsqrt(jnp.maximum(ss, 1e-24))
    kk = kk * jnp.concatenate([inv, inv, inv, inv], axis=1)
    rk = _split_dot(fold(r * k * vec[2:3]), sji, 2)
    vc_ref[0] = v
    bonus_ref[0] = v * jnp.concatenate([rk] * SUBLANES, axis=1)
    lora = lora_ref[0]
    zw = jnp.tanh(lora[:, :2 * DECAY_LORA])
    za = lora[:, 2 * DECAY_LORA:]
    for d in range(2):
        wpre = w0_ref[d:d + 1, :] + jnp.dot(zw, w2_ref[d], preferred_element_type=F32,
                                            precision=lax.Precision.HIGHEST)
        apre = a0_ref[d:d + 1, :] + jnp.dot(za, a2_ref[d], preferred_element_type=F32,
                                            precision=lax.Precision.HIGHEST)
        nw = -wpre
        softplus = jnp.maximum(nw, 0.0) + jnp.log(1.0 + jnp.exp(-jnp.abs(nw)))
        w = -softplus - 0.5
        a = _sigmoid(apre)
        log_decay = -jnp.exp(w)
        tri = tri_ref[d]
        hi = log_decay.astype(BF16)
        rest = log_decay - hi.astype(F32)
        mid = rest.astype(BF16)
        low = (rest - mid.astype(F32)).astype(BF16)
        log_g = _dot(tri, hi) + _dot(tri, mid) + _dot(tri, low)
        g = jnp.exp(log_g)
        g_inv = jnp.exp(-log_g)
        at_ref[d, 0] = -kk * jnp.exp(log_g - log_decay)
        bt_ref[d, 0] = kk * a * g_inv
        kt_ref[d, 0] = k * (1.0 + (a - 1.0) * vec[1:2]) * g_inv
        rt_ref[d, 0] = r * g
        gm_ref[d, 0] = g


def _rwkv_prep(r, k, v, lora, cwr, cwk, cwv, w0, w2, a0, a2, vec, sjj, sji, tm):
    b, n, _ = r.shape
    nh = n // SUBLANES

    def tile(w):
        return pl.BlockSpec((1, tm, w), lambda i, j: (i, j, 0))

    def halo_prev(w):
        return pl.BlockSpec((1, SUBLANES, w), lambda i, j: (i, jnp.maximum(j * (tm // SUBLANES) - 1, 0), 0))

    def halo_next(w):
        return pl.BlockSpec((1, SUBLANES, w), lambda i, j: (i, jnp.minimum((j + 1) * (tm // SUBLANES), nh - 1), 0))

    def dir_tile(w):
        return pl.BlockSpec((2, 1, tm, w), lambda i, j: (0, i, j, 0))

    t = np.arange(tm)
    same = (t[:, None] // SCAN_T) == (t[None, :] // SCAN_T)
    tri = jnp.asarray(np.stack([same & (t[None, :] <= t[:, None]), same & (t[None, :] >= t[:, None])]), BF16)
    out_shape = ([jax.ShapeDtypeStruct((b, n, I_W), F32)] * 2
                 + [jax.ShapeDtypeStruct((2, b, n, J_W), F32)] * 5)
    return pl.pallas_call(
        _rwkv_prep_kernel,
        out_shape=out_shape,
        grid=(b, n // tm),
        in_specs=[tile(J_W), halo_prev(J_W), halo_next(J_W),
                  tile(J_W), halo_prev(J_W), halo_next(J_W),
                  tile(I_W), halo_prev(I_W), halo_next(I_W),
                  tile(4 * DECAY_LORA),
                  _const_spec((CONV_W, J_W)), _const_spec((CONV_W, J_W)), _const_spec((CONV_W, I_W)),
                  _const_spec((2, J_W)), _const_spec((2, 2 * DECAY_LORA, J_W)),
                  _const_spec((2, J_W)), _const_spec((2, 2 * ICLR_LORA, J_W)),
                  _const_spec((3, J_W)), _const_spec((LANES, LANES)), _const_spec((LANES, LANES)),
                  _const_spec((2, tm, tm))],
        out_specs=[tile(I_W), tile(I_W)] + [dir_tile(J_W)] * 5,
        compiler_params=_cparams(("arbitrary", "arbitrary")),
        name="rwkv_prep",
    )(r, r, r, k, k, k, v, v, v, lora, cwr, cwk, cwv, w0, w2, a0, a2, vec, sjj, sji, tri)


N_ROWVEC = 4
N_DIR_REFS = N_ROWVEC + 2
SCAN_PASSES = 1
DECAY_PASSES = 3
N_SG = 3
SG_PARTS = (((0, 0, 6, 0), (1, 0, 2, 6)),
            ((1, 2, 4, 0), (2, 0, 4, 4)),
            ((2, 4, 2, 0), (3, 0, 6, 2)))
SG_MIXED = 1


def _sg_expand_matrices():
    m = np.zeros((N_SG, 2 * LANES, E_W), np.float32)
    for sg, parts in enumerate(SG_PARTS):
        for half, (_, h0, nh, slot0) in enumerate(parts):
            for jl in range(4):
                for part in range(J_PARTS):
                    for h in range(h0, h0 + nh):
                        for rep in range(I_REP):
                            m[sg, half * LANES + jl * 32 + part * 8 + h,
                              jl * 128 + part * 32 + (slot0 + h - h0) * I_REP + rep] = 1.0
    return m


def _scan_kernel(*refs, t_chunk, n_batch):
    ins = refs[:2 * N_DIR_REFS]
    r2_ref, anti_ref, s0_ref = refs[2 * N_DIR_REFS:2 * N_DIR_REFS + 3]
    y_refs = refs[2 * N_DIR_REFS + 3:2 * N_DIR_REFS + 5]
    sout_ref = refs[2 * N_DIR_REFS + 5]
    e_ref, s_ref, vbuf_ref, ybuf_ref, yrev_ref = refs[2 * N_DIR_REFS + 6:]
    step_id = pl.program_id(0)
    assert n_batch == 2
    last = t_chunk - 1

    @pl.when(step_id == 0)
    def _():
        s_ref[...] = s0_ref[...]

    def src(x, s):
        return ins[(s // n_batch) * N_DIR_REFS + x][0, s % n_batch]

    def tiles(a, b):
        return pltpu.einshape("t(kl)->tkl", jnp.concatenate([a, b], axis=1), l=LANES)

    def expand(sg, first, second, passes):
        lhs = jnp.concatenate([first, second], axis=2)
        lhs = lhs.reshape(lhs.shape[0] * SUBLANES, 2 * LANES)
        return _split_dot(lhs, r2_ref[sg], passes).reshape(first.shape[0], SUBLANES, E_W)

    anti = anti_ref[...]
    for sg, parts in enumerate(SG_PARTS):
        for pair in range(N_ROWVEC // 2):
            halves = []
            for s, _, _, _ in parts:
                ops = [src(2 * pair + k, s) for k in range(2)]
                if sg == SG_MIXED and s >= n_batch:
                    ops = [_dot(anti, o.astype(BF16)) for o in ops]
                halves.append(tiles(*ops))
            e_ref[sg, pair] = expand(sg, halves[0], halves[1], SCAN_PASSES)
    for s in range(2 * n_batch):
        vbuf_ref[s] = pltpu.einshape("t(kl)->tkl", ins[(s // n_batch) * N_DIR_REFS + N_ROWVEC + 1][s % n_batch], l=LANES)

    lane = lax.broadcasted_iota(jnp.int32, (SUBLANES, LANES), 1)
    q_even = (lane // 32) % 2 == 0
    slot = (lane // I_REP) % HEAD_SLOTS

    def shifted(x, slots):
        return x if slots == 0 else pltpu.roll(x, (slots * I_REP) % LANES, 1)

    def allparts(p):
        return (p + pltpu.roll(p, 64, 1)) + (pltpu.roll(p, 32, 1) + pltpu.roll(p, 96, 1))

    def step(s, sg):
        (sa_src, ha, _, slot_a), (sb_src, hb, _, slot_b) = SG_PARTS[sg]
        ta = s if sa_src < n_batch else last - s
        tb = s if sb_src < n_batch else last - s
        te = s if sg == SG_MIXED else ta
        vp = jnp.where(slot < slot_b, shifted(vbuf_ref[sa_src, ta], slot_a - ha),
                       shifted(vbuf_ref[sb_src, tb], slot_b - hb))
        vr = pltpu.roll(vp, 32, 1)
        v = (jnp.where(q_even, vp, vr), jnp.where(q_even, vr, vp))
        g = sg
        t = te

        def row(x, jj):
            r0 = (x % 2) * 4 + jj // 4
            return e_ref[sg, x // 2, te, r0:r0 + 1, (jj % 4) * LANES:(jj % 4 + 1) * LANES]

        def add(acc, ih, jj, term):
            k = (ih, jj % 4)
            acc[k] = term if k not in acc else acc[k] + term

        def total(acc, ih):
            return allparts((acc[ih, 0] + acc[ih, 1]) + (acc[ih, 2] + acc[ih, 3]))

        acc = {}
        for jj in range(JJ):
            a = row(0, jj)
            for ih in range(I_HI):
                add(acc, ih, jj, s_ref[g, ih, jj] * a)
        sa = [total(acc, ih) for ih in range(I_HI)]
        acc = {}
        for jj in range(JJ):
            b, k, rr = row(1, jj), row(2, jj), row(3, jj)
            for ih in range(I_HI):
                new = (s_ref[g, ih, jj] + v[ih] * k) + sa[ih] * b
                s_ref[g, ih, jj] = new
                add(acc, ih, jj, new * rr)
        y = [total(acc, ih) for ih in range(I_HI)]
        packed = jnp.where(q_even, y[0], y[1])
        ybuf_ref[g, t] = packed
        if sg == SG_MIXED:
            yrev_ref[last - s] = packed

    def one_token(s, carry):
        for sg in range(N_SG):
            step(s, sg)
        return carry

    lax.fori_loop(0, t_chunk, one_token, 0, unroll=SCAN_UNROLL)

    for s in range(2 * n_batch):
        out = jnp.zeros((t_chunk, SUBLANES, LANES), F32)
        for sg, parts in enumerate(SG_PARTS):
            for s2, h0, nh, slot0 in parts:
                if s2 == s:
                    buf = yrev_ref[...] if (sg == SG_MIXED and s >= n_batch) else ybuf_ref[sg]
                    if h0 != slot0:
                        buf = pltpu.roll(buf, ((h0 - slot0) * I_REP) % LANES, 2)
                    out = jnp.where(jnp.logical_and(slot >= h0, slot < h0 + nh)[None], buf, out)
        y_refs[s // n_batch][s % n_batch] = pltpu.einshape("tkl->t(kl)", out)

    for sg, parts in enumerate(SG_PARTS):
        halves = []
        toks = []
        for s, _, _, _ in parts:
            tile0, tok = (t_chunk - SUBLANES, SUBLANES - 1) if s < n_batch else (0, 0)
            g8 = src(N_ROWVEC, s)[tile0:tile0 + SUBLANES, :]
            halves.append(tiles(g8, g8))
            toks.append(tok)
        res = expand(sg, halves[0], halves[1], DECAY_PASSES)
        first_rows, second_rows = res[toks[0]], res[toks[1]]
        for jj in range(JJ):
            r0 = jj // 4
            blk = slice((jj % 4) * LANES, (jj % 4 + 1) * LANES)
            scale = jnp.where(slot[0:1] < parts[1][3], first_rows[r0:r0 + 1, blk], second_rows[r0:r0 + 1, blk])
            for ih in range(I_HI):
                s_ref[sg, ih, jj] = s_ref[sg, ih, jj] * scale

    @pl.when(step_id == pl.num_programs(0) - 1)
    def _():
        sout_ref[...] = s_ref[...]


def _rwkv_scan(vc, at, bt, kt, rt, gm, state0, r2, anti):
    b, n, _ = vc.shape
    t = SCAN_T
    nc = n // t
    in_specs = []
    args = []
    out_specs = []
    for d in range(2):
        def chunk(i, d=d):
            return i if d == 0 else nc - 1 - i

        perdir = pl.BlockSpec((1, b, t, J_W), lambda i, d=d, chunk=chunk: (d, 0, chunk(i), 0))
        ispec = pl.BlockSpec((b, t, I_W), lambda i, chunk=chunk: (0, chunk(i), 0))
        in_specs += [perdir] * (N_ROWVEC + 1) + [ispec]
        args += [at, bt, kt, rt, gm, vc]
        out_specs.append(ispec)
    state_shape = (N_SG, I_HI, JJ, SUBLANES, LANES)
    in_specs += [_const_spec((N_SG, 2 * LANES, E_W)), _const_spec((t, t)), _const_spec(state_shape)]
    args += [r2, anti, state0]
    out_specs.append(_const_spec(state_shape))
    y_shape = jax.ShapeDtypeStruct((b, n, I_W), F32)
    return pl.pallas_call(
        functools.partial(_scan_kernel, t_chunk=t, n_batch=b),
        out_shape=[y_shape, y_shape, jax.ShapeDtypeStruct(state_shape, F32)],
        grid=(nc,),
        in_specs=in_specs,
        out_specs=out_specs,
        scratch_shapes=[pltpu.VMEM((N_SG, N_ROWVEC // 2, t, SUBLANES, E_W), F32),
                        pltpu.VMEM(state_shape, F32),
                        pltpu.VMEM((2 * b, t, SUBLANES, LANES), F32),
                        pltpu.VMEM((N_SG, t, SUBLANES, LANES), F32),
                        pltpu.VMEM((t, SUBLANES, LANES), F32)],
        compiler_params=_cparams(("arbitrary",)),
        name="rwkv_scan",
    )(*args)


def _mix_out_kernel(x_ref, m_ref, g_ref, f_ref, a_ref, yf_ref, yb_ref, bonus_ref, zg_ref,
                    g2_ref, ln_ref, sii_ref, wf_ref, wa_ref, wr_ref, o_ref):
    x = x_ref[0]
    y = yf_ref[0] + yb_ref[0]
    sii = sii_ref[...]
    nblk = I_W // LANES
    cnt = float(2 * HEAD_DIM)

    def fold(z):
        acc = z[:, 0:LANES]
        for c in range(1, nblk):
            acc = acc + z[:, c * LANES:(c + 1) * LANES]
        return acc

    def spread(z):
        return jnp.concatenate([z] * nblk, axis=1)

    mu = _split_dot(fold(y), sii, 2) * (1.0 / cnt)
    dlt = y - spread(mu)
    var = _split_dot(fold(dlt * dlt), sii, 2) * (1.0 / cnt)
    ln = ln_ref[...]
    yn = dlt * spread(lax.rsqrt(var + GN_EPS)) * ln[0:1] + ln[1:2]
    gate = _dot(_sigmoid(zg_ref[0]).astype(BF16), g2_ref[...])
    rw = ((yn + bonus_ref[0]) * gate).astype(BF16)
    o = _dot(f_ref[0].astype(BF16), wf_ref[...]) + _dot(a_ref[0].astype(BF16), wa_ref[...]) + _dot(rw, wr_ref[...])
    o_ref[0] = x + m_ref[0] * _rms(o, g_ref[...])


def _mix_out(x, gate_row, g3, fo, ao, y_f, y_b, bonus, zg, g2p, ln, sii, wf, wa, wr, tm):
    b, n, d = x.shape

    def tile(w):
        return pl.BlockSpec((1, tm, w), lambda i, j: (i, j, 0))

    return pl.pallas_call(
        _mix_out_kernel,
        out_shape=jax.ShapeDtypeStruct(x.shape, F32),
        grid=(b, n // tm),
        in_specs=[tile(d), pl.BlockSpec((1, 1, d), lambda i, j: (i, 0, 0)), _const_spec((1, d)),
                  tile(FOURIER_W), tile(ATTN_W),
                  tile(I_W), tile(I_W), tile(I_W), tile(GATE_LORA),
                  _const_spec((GATE_LORA, I_W)), _const_spec((2, I_W)), _const_spec((LANES, LANES)),
                  _const_spec((FOURIER_W, d)), _const_spec((ATTN_W, d)), _const_spec((I_W, d))],
        out_specs=tile(d),
        compiler_params=_cparams(("arbitrary", "arbitrary")),
        name="mix_out",
    )(x, gate_row, g3, fo, ao, y_f, y_b, bonus, zg, g2p, ln, sii, wf, wa, wr)


def _rope_tables(n_tokens):
    rows_n = n_tokens // GRID_W
    row = jnp.repeat(jnp.arange(rows_n), GRID_W).astype(F32)
    colp = jnp.tile(jnp.arange(GRID_W), rows_n).astype(F32)
    n_freq = HEAD_DIM // 4
    inv = ROPE_BASE ** (-jnp.arange(n_freq, dtype=F32) / n_freq)
    ang = jnp.concatenate([row[:, None] * inv, colp[:, None] * inv], axis=-1)
    cos = jnp.tile(jnp.cos(ang), (1, 4))
    sin = jnp.tile(jnp.sin(ang), (1, 4))
    return cos, sin


def _rot_cols(w):
    d, n = w.shape
    w4 = w.reshape(d, n // HEAD_DIM, 2, HEAD_DIM // 2)
    return jnp.stack([-w4[:, :, 1], w4[:, :, 0]], axis=2).reshape(d, n)


def _layer_weights(li, p):
    f = p["ffn1_wo"].shape[1]
    fp = -(-f // (2 * LANES)) * (2 * LANES)
    out = {}
    for name in ("ffn1", "ffn2"):
        wi = p[name + "_wi"][li]
        wo = p[name + "_wo"][li]
        out[name] = (jnp.pad(wi[:, :f], ((0, 0), (0, fp - f))).astype(BF16),
                     jnp.pad(wi[:, f:], ((0, 0), (0, fp - f))).astype(BF16),
                     jnp.pad(wo, ((0, fp - f), (0, 0))).astype(BF16))
    w_in = p["mix_w_in"][li]
    offs = np.cumsum([0, FOURIER_W, ATTN_W, KV_W, KV_W, RWKV_W, RWKV_W, RWKV_W,
                      DECAY_LORA, DECAY_LORA, ICLR_LORA, ICLR_LORA, GATE_LORA])
    part = [w_in[:, offs[i]:offs[i + 1]] for i in range(12)]
    wf, wq, wk, wv, wrr, wrk, wrv = part[:7]
    swap = lambda w: jnp.concatenate([w[:, HEAD_DIM:], w[:, :HEAD_DIM]], axis=1)
    cols = {"f": wf, "q": wq, "qr": _rot_cols(wq),
            "k": jnp.concatenate([wk, swap(wk)], axis=1),
            "kr": jnp.concatenate([_rot_cols(wk), swap(_rot_cols(wk))], axis=1),
            "v": jnp.concatenate([wv, swap(wv)], axis=1),
            "r": _take_cols(wrr, J_IDX), "rk": _take_cols(wrk, J_IDX), "rv": _take_cols(wrv, I_IDX),
            "lora": jnp.concatenate(part[7:11], axis=1), "g": part[11]}
    out["w_all"] = jnp.concatenate([cols[n] for n, _ in MIX_COLS], axis=1).astype(BF16)
    conv = p["rwkv_conv"][li]
    out["cwr"] = _take_cols(conv[:, :RWKV_W], J_IDX)
    out["cwk"] = _take_cols(conv[:, RWKV_W:2 * RWKV_W], J_IDX)
    out["cwv"] = _take_cols(conv[:, 2 * RWKV_W:], I_IDX)
    def per_direction(w2):
        w2 = _take_cols(w2, J_IDX)
        z = jnp.zeros_like(w2[0])
        return jnp.stack([jnp.concatenate([w2[0], z], axis=0), jnp.concatenate([z, w2[1]], axis=0)])

    out["w0"] = _take_cols(p["rwkv_w0"][li], J_IDX)
    out["w2"] = per_direction(p["rwkv_w2"][li])
    out["a0"] = _take_cols(p["rwkv_a0"][li], J_IDX)
    out["a2"] = per_direction(p["rwkv_a2"][li])
    out["vec"] = _take_cols(jnp.stack([p["rwkv_k_k"][li], p["rwkv_k_a"][li], p["rwkv_r_k"][li].reshape(-1)]), J_IDX)
    out["g2"] = _take_cols(p["rwkv_g2"][li], I_IDX).astype(BF16)
    out["ln"] = _take_cols(jnp.stack([p["rwkv_ln_g"][li], p["rwkv_ln_b"][li]]), I_IDX)
    w_out = p["mix_w_out"][li]
    out["wo_f"] = w_out[:FOURIER_W].astype(BF16)
    out["wo_a"] = w_out[FOURIER_W:FOURIER_W + ATTN_W].astype(BF16)
    out["wo_r"] = _take_rows(w_out[FOURIER_W + ATTN_W:], I_IDX_ONCE).astype(BF16)
    sink = p["attn_sink"][li]
    out["sink"] = jnp.concatenate([sink, jnp.zeros((8 - ATTN_HEADS,), F32)])
    return out


def kernel(x, c, ctx, c_ctx, mod_w, mod_b, norm_g, ffn1_wi, ffn1_wo, mix_w_in, mix_w_out, attn_sink,
           rwkv_conv, rwkv_w0, rwkv_w2, rwkv_a0, rwkv_a2, rwkv_g2, rwkv_k_k, rwkv_k_a, rwkv_r_k,
           rwkv_ln_g, rwkv_ln_b, ffn2_wi, ffn2_wo):
    p = dict(ffn1_wi=ffn1_wi, ffn1_wo=ffn1_wo, ffn2_wi=ffn2_wi, ffn2_wo=ffn2_wo, mix_w_in=mix_w_in,
             mix_w_out=mix_w_out, attn_sink=attn_sink, rwkv_conv=rwkv_conv, rwkv_w0=rwkv_w0, rwkv_w2=rwkv_w2,
             rwkv_a0=rwkv_a0, rwkv_a2=rwkv_a2, rwkv_g2=rwkv_g2, rwkv_k_k=rwkv_k_k, rwkv_k_a=rwkv_k_a,
             rwkv_r_k=rwkv_r_k, rwkv_ln_g=rwkv_ln_g, rwkv_ln_b=rwkv_ln_b)
    b, s, d = x.shape
    n_c = ctx.shape[1]
    depth = mod_w.shape[0]
    assert b + 1 <= 8 and s % WINDOW == 0 and n_c % WINDOW == 0 and WINDOW % SCAN_T == 0
    tm_l = 256
    tm_c = min(256, n_c)

    cond8 = jnp.zeros((8, d), F32).at[:b].set(c).at[b].set(c_ctx)
    mod = _ada_mod(cond8, mod_w, mod_b).reshape(depth, 8, N_MOD, d)
    cos_l, sin_l = _rope_tables(s)
    cos_c = jnp.ones((n_c, LANES), F32)
    sin_c = jnp.zeros((n_c, LANES), F32)
    r2 = jnp.asarray(_sg_expand_matrices(), BF16)
    anti = jnp.asarray(np.eye(SCAN_T)[::-1], BF16)
    sjj, sji, sii = (jnp.asarray(m, BF16) for m in _seg_matrices())
    zero_state = jnp.zeros((N_SG, I_HI, JJ, SUBLANES, LANES), F32)

    xl, xc = x, ctx
    for li in range(depth):
        need_ctx_out = li < depth - 1
        w = _layer_weights(li, p)
        ml = mod[li, :b]
        mc = jnp.broadcast_to(mod[li, b:b + 1], (b, N_MOD, d))
        g = norm_g[li]
        xl = _ffn_half(xl, ml[:, 0:3], g[0:2], *w["ffn1"], tm_l)
        xc = _ffn_half(xc, mc[:, 0:3], g[0:2], *w["ffn1"], tm_c)

        def mixer_in(xx, mm, cos, sin, tm):
            return _mix_in(xx, mm[:, 3:6], g[2:3], w["w_all"], cos, sin, tm)

        fl, ql, kl, vl, rl, rkl, rvl, loral, zgl = mixer_in(xl, ml, cos_l, sin_l, tm_l)
        fc, qc, kc, vc, rc_, rkc, rvc, lorac, zgc = mixer_in(xc, mc, cos_c, sin_c, tm_c)

        attn_l = _attention(ql, kl, vl, kc, vc, w["sink"], True)

        def prep(r_, k_, v_, lora_, tm):
            return _rwkv_prep(r_, k_, v_, lora_, w["cwr"], w["cwk"], w["cwv"], w["w0"], w["w2"],
                              w["a0"], w["a2"], w["vec"], sjj, sji, tm)

        pc = prep(rc_, rkc, rvc, lorac, tm_c)
        plat = prep(rl, rkl, rvl, loral, tm_l)
        ycf, ycb, state_c = _rwkv_scan(pc[0], *pc[2:], zero_state, r2, anti)
        ylf, ylb, _ = _rwkv_scan(plat[0], *plat[2:], state_c, r2, anti)

        def mixer_out(xx, mm, fo, ao, y_f, y_b, bonus, zg, tm):
            return _mix_out(xx, mm[:, 5:6], g[3:4], fo, ao, y_f, y_b, bonus, zg, w["g2"], w["ln"], sii,
                            w["wo_f"], w["wo_a"], w["wo_r"], tm)

        xl = mixer_out(xl, ml, _fourier_mix(fl), attn_l, ylf, ylb, plat[1], zgl, tm_l)
        if need_ctx_out:
            attn_c = _attention(qc, None, None, kc, vc, w["sink"], False)
            xc = mixer_out(xc, mc, _fourier_mix(fc), attn_c, ycf, ycb, pc[1], zgc, tm_c)
            xc = _ffn_half(xc, mc[:, 6:9], g[4:6], *w["ffn2"], tm_c)
        xl = _ffn_half(xl, ml[:, 6:9], g[4:6], *w["ffn2"], tm_l)
    return xl
```

```python
import functools

import numpy as np
import jax
import jax.numpy as jnp
from jax import lax
from jax.experimental import pallas as pl
from jax.experimental.pallas import tpu as pltpu

F32 = jnp.float32
BF16 = jnp.bfloat16

HEAD_DIM = 64
GRID_W = 64
FOURIER_W = 256
FOURIER_GROUPS = 4
ATTN_HEADS = 6
ATTN_KV_HEADS = 2
ATTN_GROUP = ATTN_HEADS // ATTN_KV_HEADS
ATTN_W = ATTN_HEADS * HEAD_DIM
KV_W = ATTN_KV_HEADS * HEAD_DIM
RWKV_HEADS = 6
RWKV_W = RWKV_HEADS * HEAD_DIM
WINDOW = 128
ROPE_BASE = 10000.0
DECAY_LORA = 64
ICLR_LORA = 64
GATE_LORA = 128
CONV_W = 3
N_MOD = 9
NORM_EPS = 1e-6
GN_EPS = 64e-5

LANES = 128
SUBLANES = 8
VMEM_LIMIT = 56 * 1024 * 1024

HEAD_SLOTS = 8
J_PARTS = 4
I_REP = 4
JJ = HEAD_DIM // J_PARTS
I_HI = HEAD_DIM // (SUBLANES * I_REP)
J_W = 4 * LANES
E_W = 4 * LANES
I_W = SUBLANES * LANES
SCAN_T = 64
SCAN_UNROLL = 8
FOURIER_N1 = 64
FOURIER_DIRECT_MAX = 256


def _j_layout():
    idx = np.full((J_W,), -1, np.int64)
    for jh in range(4):
        for jl in range(4):
            for part in range(J_PARTS):
                for h in range(RWKV_HEADS):
                    idx[jh * 128 + jl * 32 + part * 8 + h] = h * HEAD_DIM + part * JJ + jh * 4 + jl
    return idx


def _i_layout(dup):
    idx = np.full((I_W,), -1, np.int64)
    for isub in range(8):
        for q in range(4 if dup else 2):
            for h in range(RWKV_HEADS):
                for il in range(I_REP):
                    idx[isub * 128 + q * 32 + h * 4 + il] = h * HEAD_DIM + (q % 2) * 32 + isub * 4 + il
    return idx


J_IDX = _j_layout()
I_IDX = _i_layout(True)
I_IDX_ONCE = _i_layout(False)


def _take_cols(w, idx):
    g = jnp.take(w, jnp.asarray(np.maximum(idx, 0)), axis=-1)
    return g * jnp.asarray((idx >= 0).astype(np.float32))


def _take_rows(w, idx):
    g = jnp.take(w, jnp.asarray(np.maximum(idx, 0)), axis=0)
    return g * jnp.asarray((idx >= 0).astype(np.float32))[:, None]


def _expand_matrix():
    m = np.zeros((LANES, E_W), np.float32)
    for jl in range(4):
        for part in range(J_PARTS):
            for h in range(HEAD_SLOTS):
                for rep in range(I_REP):
                    m[jl * 32 + part * 8 + h, jl * 128 + part * 32 + h * I_REP + rep] = 1.0
    return m


def _seg_matrices():
    hj = np.arange(LANES) % 8
    hi = (np.arange(LANES) // I_REP) % 8
    jj = (hj[:, None] == hj[None, :]).astype(np.float32)
    ji = (hj[:, None] == hi[None, :]).astype(np.float32)
    ii = (hi[:, None] == hi[None, :]).astype(np.float32)
    return jj, ji, ii


def _cparams(sem, vmem=VMEM_LIMIT):
    return pltpu.CompilerParams(dimension_semantics=sem, vmem_limit_bytes=vmem)


def _const_spec(shape):
    nd = len(shape)
    return pl.BlockSpec(shape, lambda *_: (0,) * nd)


def _sigmoid(x):
    return 1.0 / (1.0 + jnp.exp(-x))


def _rms(x, g):
    return x * lax.rsqrt(jnp.mean(x * x, axis=-1, keepdims=True) + NORM_EPS) * g


def _dot(a, b):
    return jnp.dot(a, b, preferred_element_type=F32)


def _split_dot(x, m, passes):
    acc = None
    rem = x
    for p in range(passes):
        piece = rem.astype(BF16)
        d = _dot(piece, m)
        acc = d if acc is None else acc + d
        if p + 1 < passes:
            rem = rem - piece.astype(F32)
    return acc


def _ada_kernel(c_ref, w_ref, b_ref, o_ref):
    c = c_ref[...]
    s = c * _sigmoid(c)
    o_ref[0] = _dot(s.astype(BF16), w_ref[0].astype(BF16)) + b_ref[0]


def _ada_mod(cond8, mod_w, mod_b):
    nl, d, nw = mod_w.shape
    tn = nw // 8
    return pl.pallas_call(
        _ada_kernel,
        out_shape=jax.ShapeDtypeStruct((nl, 8, nw), F32),
        grid=(nl, nw // tn),
        in_specs=[pl.BlockSpec((8, d), lambda l, j: (0, 0)),
                  pl.BlockSpec((1, d, tn), lambda l, j: (l, 0, j)),
                  pl.BlockSpec((1, 1, tn), lambda l, j: (l, 0, j))],
        out_specs=pl.BlockSpec((1, 8, tn), lambda l, j: (l, 0, j)),
        compiler_params=_cparams(("arbitrary", "arbitrary")),
        name="ada_mod",
    )(cond8, mod_w, mod_b.reshape(nl, 1, nw))


def _ffn_kernel(x_ref, m_ref, g_ref, wg_ref, wu_ref, wo_ref, o_ref):
    x = x_ref[0]
    m = m_ref[0]
    g = g_ref[...]
    h = _rms(x, g[0:1]) * (1.0 + m[1:2]) + m[0:1]
    hb = h.astype(BF16)
    gate = _dot(hb, wg_ref[...])
    up = _dot(hb, wu_ref[...])
    act = (gate * _sigmoid(gate) * up).astype(BF16)
    y = _dot(act, wo_ref[...])
    o_ref[0] = x + 0.5 * m[2:3] * _rms(y, g[1:2])


def _ffn_half(x, m3, g2, wg, wu, wo, tm):
    b, n, d = x.shape
    fp = wg.shape[1]
    return pl.pallas_call(
        _ffn_kernel,
        out_shape=jax.ShapeDtypeStruct(x.shape, F32),
        grid=(b, n // tm),
        in_specs=[pl.BlockSpec((1, tm, d), lambda i, j: (i, j, 0)),
                  pl.BlockSpec((1, 3, d), lambda i, j: (i, 0, 0)),
                  _const_spec((2, d)),
                  _const_spec((d, fp)), _const_spec((d, fp)), _const_spec((fp, d))],
        out_specs=pl.BlockSpec((1, tm, d), lambda i, j: (i, j, 0)),
        compiler_params=_cparams(("arbitrary", "arbitrary")),
        name="ffn_half",
    )(x, m3, g2, wg, wu, wo)


MIX_COLS = (("f", FOURIER_W), ("q", ATTN_W), ("qr", ATTN_W), ("k", 2 * KV_W), ("kr", 2 * KV_W),
            ("v", 2 * KV_W), ("r", J_W), ("rk", J_W), ("rv", I_W), ("lora", 4 * DECAY_LORA), ("g", GATE_LORA))
MIX_OFF = {}
_o = 0
for _n, _w in MIX_COLS:
    MIX_OFF[_n] = (_o, _o + _w)
    _o += _w
MIX_TOTAL = _o


def _mix_in_kernel(x_ref, m_ref, g_ref, w_ref, cos_ref, sin_ref,
                   f_ref, q_ref, k_ref, v_ref, r_ref, rk_ref, rv_ref, lora_ref, zg_ref):
    x = x_ref[0]
    m = m_ref[0]
    h = _rms(x, g_ref[...]) * (1.0 + m[1:2]) + m[0:1]
    z = _dot(h.astype(BF16), w_ref[...])

    def col(name):
        lo, hi = MIX_OFF[name]
        return z[:, lo:hi]

    cos = cos_ref[...]
    sin = sin_ref[...]
    cos3 = jnp.concatenate([cos, cos, cos], axis=1)
    sin3 = jnp.concatenate([sin, sin, sin], axis=1)
    cos2 = jnp.concatenate([cos, cos], axis=1)
    sin2 = jnp.concatenate([sin, sin], axis=1)
    f_ref[0] = col("f")
    q_ref[0] = ((col("q") * cos3 + col("qr") * sin3) * (HEAD_DIM ** -0.5)).astype(BF16)
    k_ref[0] = (col("k") * cos2 + col("kr") * sin2).astype(BF16)
    v_ref[0] = col("v").astype(BF16)
    r_ref[0] = col("r")
    rk_ref[0] = col("rk")
    rv_ref[0] = col("rv")
    lora_ref[0] = col("lora")
    zg_ref[0] = col("g")


def _mix_in(x, m3, g1, w_all, cos, sin, tm):
    b, n, d = x.shape
    widths = (FOURIER_W, ATTN_W, 2 * KV_W, 2 * KV_W, J_W, J_W, I_W, 4 * DECAY_LORA, GATE_LORA)
    dtypes = (F32, BF16, BF16, BF16, F32, F32, F32, F32, F32)
    return pl.pallas_call(
        _mix_in_kernel,
        out_shape=[jax.ShapeDtypeStruct((b, n, w), dt) for w, dt in zip(widths, dtypes)],
        grid=(b, n // tm),
        in_specs=[pl.BlockSpec((1, tm, d), lambda i, j: (i, j, 0)),
                  pl.BlockSpec((1, 3, d), lambda i, j: (i, 0, 0)),
                  _const_spec((1, d)),
                  _const_spec((d, MIX_TOTAL)),
                  pl.BlockSpec((tm, LANES), lambda i, j: (j, 0)),
                  pl.BlockSpec((tm, LANES), lambda i, j: (j, 0))],
        out_specs=[pl.BlockSpec((1, tm, w), lambda i, j: (i, j, 0)) for w in widths],
        compiler_params=_cparams(("arbitrary", "arbitrary")),
        name="mix_in",
    )(x, m3, g1, w_all, cos, sin)


def _fourier_stage1_kernel(l_ref, x_ref, o_ref):
    o_ref[0] = _dot(l_ref[...], x_ref[0].astype(BF16))


def _fourier_stage2_kernel(a_ref, mc_ref, ms_ref, cc_ref, sc_ref, o_ref, *, scale):
    ar = a_ref[0, 0, 0].astype(BF16)
    ai = a_ref[0, 1, 0].astype(BF16)
    mc = mc_ref[0]
    ms = ms_ref[0]
    gr = _dot(mc, ar) + _dot(ms, ai)
    gi = _dot(mc, ai) - _dot(ms, ar)
    o_ref[0] = (_dot(gr.astype(BF16), cc_ref[...]) + _dot(gi.astype(BF16), sc_ref[...])) * scale


def _fourier_small_kernel(z_ref, cn_ref, sn_ref, cc_ref, sc_ref, o_ref, *, scale):
    z = z_ref[0].astype(BF16)
    t1 = _dot(z, cc_ref[...]).astype(BF16)
    t2 = _dot(z, sc_ref[...]).astype(BF16)
    o_ref[0] = (_dot(cn_ref[...], t1) - _dot(sn_ref[...], t2)) * scale


def _channel_dft():
    gw = FOURIER_W // FOURIER_GROUPS
    c = np.arange(FOURIER_W)
    same = (c[:, None] // gw) == (c[None, :] // gw)
    ang = 2.0 * np.pi * ((c[:, None] % gw) * (c[None, :] % gw) % gw) / gw
    return (np.cos(ang) * same).astype(np.float32), (np.sin(ang) * same).astype(np.float32)


def _fourier_mix(z):
    b, n, w = z.shape
    gw = w // FOURIER_GROUPS
    scale = float(1.0 / np.sqrt(n * gw))
    cc, sc = _channel_dft()
    cc = jnp.asarray(cc, BF16)
    sc = jnp.asarray(sc, BF16)
    if n <= FOURIER_DIRECT_MAX:
        p = np.arange(n)
        ang = 2.0 * np.pi * ((p[:, None] * p[None, :]) % n) / n
        return pl.pallas_call(
            functools.partial(_fourier_small_kernel, scale=scale),
            out_shape=jax.ShapeDtypeStruct((b, n, w), F32),
            grid=(b,),
            in_specs=[pl.BlockSpec((1, n, w), lambda i: (i, 0, 0)),
                      _const_spec((n, n)), _const_spec((n, n)), _const_spec((w, w)), _const_spec((w, w))],
            out_specs=pl.BlockSpec((1, n, w), lambda i: (i, 0, 0)),
            compiler_params=_cparams(("arbitrary",)),
            name="fourier_small",
        )(z, jnp.asarray(np.cos(ang), BF16), jnp.asarray(np.sin(ang), BF16), cc, sc)
    n1 = FOURIER_N1
    n2 = n // n1
    k1 = np.arange(n1)
    ang1 = 2.0 * np.pi * ((k1[:, None] * k1[None, :]) % n1) / n1
    lhs1 = jnp.asarray(np.concatenate([np.cos(ang1), -np.sin(ang1)], axis=0), BF16)
    tc = min(n2 * w, 4096)
    a = pl.pallas_call(
        _fourier_stage1_kernel,
        out_shape=jax.ShapeDtypeStruct((b, 2 * n1, n2 * w), F32),
        grid=(b, (n2 * w) // tc),
        in_specs=[_const_spec((2 * n1, n1)),
                  pl.BlockSpec((1, n1, tc), lambda i, j: (i, 0, j))],
        out_specs=pl.BlockSpec((1, 2 * n1, tc), lambda i, j: (i, 0, j)),
        compiler_params=_cparams(("arbitrary", "arbitrary")),
        name="fourier_stage1",
    )(lhs1, z.reshape(b, n1, n2 * w))
    k2 = np.arange(n2)
    freq = (k1[:, None, None] + n1 * k2[None, :, None]) * k2[None, None, :]
    ang2 = 2.0 * np.pi * (freq % n) / n
    out = pl.pallas_call(
        functools.partial(_fourier_stage2_kernel, scale=scale),
        out_shape=jax.ShapeDtypeStruct((b, n2, n1 * w), F32),
        grid=(b, n1),
        in_specs=[pl.BlockSpec((1, 2, 1, n2, w), lambda i, j: (i, 0, j, 0, 0)),
                  pl.BlockSpec((1, n2, n2), lambda i, j: (j, 0, 0)),
                  pl.BlockSpec((1, n2, n2), lambda i, j: (j, 0, 0)),
                  _const_spec((w, w)), _const_spec((w, w))],
        out_specs=pl.BlockSpec((1, n2, w), lambda i, j: (i, 0, j)),
        compiler_params=_cparams(("arbitrary", "arbitrary")),
        name="fourier_stage2",
    )(a.reshape(b, 2, n1, n2, w), jnp.asarray(np.cos(ang2), BF16), jnp.asarray(np.sin(ang2), BF16), cc, sc)
    return out.reshape(b, n, w)


NEG = -1e30


def _attn_kernel(*refs, window):
    if window:
        (sink_ref, q_ref, kp_ref, kc_ref, kn_ref, vp_ref, vc_ref, vn_ref, kx_ref, vx_ref, o_ref) = refs
    else:
        (sink_ref, q_ref, kx_ref, vx_ref, o_ref) = refs
    i = pl.program_id(1)
    nb = pl.num_programs(1)
    tq = q_ref.shape[1]
    lane = lax.broadcasted_iota(jnp.int32, (1, KV_W), 1)
    low = lane < HEAD_DIM
    if window:
        k_all = jnp.concatenate([kp_ref[0], kc_ref[0], kn_ref[0], kx_ref[0]], axis=0)
        v_all = jnp.concatenate([vp_ref[0], vc_ref[0], vn_ref[0], vx_ref[0]], axis=0)
        row = lax.broadcasted_iota(jnp.int32, (tq, WINDOW), 0)
        col = lax.broadcasted_iota(jnp.int32, (tq, WINDOW), 1)
        prev_ok = jnp.logical_and(col >= row, i > 0)
        next_ok = jnp.logical_and(col <= row, i < nb - 1)
        mask = jnp.concatenate([jnp.where(prev_ok, 0.0, NEG), jnp.zeros((tq, WINDOW), F32),
                                jnp.where(next_ok, 0.0, NEG), jnp.zeros((tq, kx_ref.shape[1]), F32)], axis=1)
    else:
        k_all = kx_ref[0]
        v_all = vx_ref[0]
        mask = None
    zero = jnp.zeros((), BF16)
    one = jnp.ones((), BF16)
    for pair in range(ATTN_HEADS // 2):
        qp = q_ref[0, :, pair * LANES:(pair + 1) * LANES]
        outs = []
        sinks = []
        for slot in range(2):
            h = 2 * pair + slot
            kv = h // ATTN_GROUP
            swap = slot != kv
            keep = (lane >= HEAD_DIM) if slot else low
            kh = jnp.where(keep, k_all[:, KV_W:] if swap else k_all[:, :KV_W], zero)
            vh = jnp.where(keep, v_all[:, KV_W:] if swap else v_all[:, :KV_W], one)
            s = lax.dot_general(qp, kh, (((1,), (1,)), ((), ())), preferred_element_type=F32)
            if mask is not None:
                s = s + mask
            sk = sink_ref[h]
            mx = jnp.maximum(s.max(axis=-1, keepdims=True), sk)
            outs.append(_dot(jnp.exp(s - mx).astype(BF16), vh))
            sinks.append(jnp.exp(sk - mx))
        num = jnp.where(low, outs[0], outs[1])
        den = pltpu.roll(jnp.where(low, outs[1], outs[0]), HEAD_DIM, 1) + jnp.where(low, sinks[0], sinks[1])
        o_ref[0, :, pair * LANES:(pair + 1) * LANES] = num / den


def _attention(q, k, v, kx, vx, sink8, window):
    b, n, _ = q.shape
    c = kx.shape[1]
    tq = WINDOW
    nb = n // tq
    smem = pl.BlockSpec(memory_space=pltpu.SMEM)
    qspec = pl.BlockSpec((1, tq, ATTN_W), lambda i, j: (i, j, 0))
    xspec = pl.BlockSpec((1, c, 2 * KV_W), lambda i, j: (i, 0, 0))
    if window:
        prev = pl.BlockSpec((1, tq, 2 * KV_W), lambda i, j: (i, jnp.maximum(j - 1, 0), 0))
        cur = pl.BlockSpec((1, tq, 2 * KV_W), lambda i, j: (i, j, 0))
        nxt = pl.BlockSpec((1, tq, 2 * KV_W), lambda i, j: (i, jnp.minimum(j + 1, nb - 1), 0))
        in_specs = [smem, qspec, prev, cur, nxt, prev, cur, nxt, xspec, xspec]
        args = (sink8, q, k, k, k, v, v, v, kx, vx)
    else:
        in_specs = [smem, qspec, xspec, xspec]
        args = (sink8, q, kx, vx)
    return pl.pallas_call(
        functools.partial(_attn_kernel, window=window),
        out_shape=jax.ShapeDtypeStruct((b, n, ATTN_W), F32),
        grid=(b, nb),
        in_specs=in_specs,
        out_specs=pl.BlockSpec((1, tq, ATTN_W), lambda i, j: (i, j, 0)),
        compiler_params=_cparams(("arbitrary", "arbitrary")),
        name="window_attention" if window else "context_attention",
    )(*args)


def _lane_blocks_to_rows(x):
    return pltpu.einshape("t(kl)->tkl", x, l=LANES)


def _shift_rows(x, prev_row, next_row):
    n = x.shape[0]
    row = lax.broadcasted_iota(jnp.int32, (n, 1), 0)
    xp = jnp.where(row == 0, prev_row, pltpu.roll(x, 1, 0))
    xn = jnp.where(row == n - 1, next_row, pltpu.roll(x, n - 1, 0))
    return xp, xn


def _rwkv_prep_kernel(r_ref, rp_ref, rn_ref, k_ref, kp_ref, kn_ref, v_ref, vp_ref, vn_ref, lora_ref,
                      cwr_ref, cwk_ref, cwv_ref, w0_ref, w2_ref, a0_ref, a2_ref, vec_ref, sjj_ref, sji_ref, tri_ref,
                      vt_ref, bonus_ref, ab_ref, kr_ref, gm_ref):
    i = pl.program_id(1)
    first = (i > 0).astype(F32)
    last = (i < pl.num_programs(1) - 1).astype(F32)

    def conv(x_ref, p_ref, n_ref, cw_ref):
        x = x_ref[0]
        xp, xn = _shift_rows(x, p_ref[0, SUBLANES - 1:SUBLANES, :] * first, n_ref[0, 0:1, :] * last)
        cw = cw_ref[...]
        return xp * cw[0:1] + x * cw[1:2] + xn * cw[2:3]

    r = conv(r_ref, rp_ref, rn_ref, cwr_ref)
    k = conv(k_ref, kp_ref, kn_ref, cwk_ref)
    v = conv(v_ref, vp_ref, vn_ref, cwv_ref)
    vec = vec_ref[...]
    sjj = sjj_ref[...]
    sji = sji_ref[...]

    def fold(x):
        return (x[:, 0:LANES] + x[:, LANES:2 * LANES]) + (x[:, 2 * LANES:3 * LANES] + x[:, 3 * LANES:4 * LANES])

    kk = k * vec[0:1]
    ss = _split_dot(fold(kk * kk), sjj, 2)
    inv = lax.rsqrt(jnp.maximum(ss, 1e-24))
    kk = kk * jnp.concatenate([inv, inv, inv, inv], axis=1)
    rk = _split_dot(fold(r * k * vec[2:3]), sji, 2)
    vt_ref[0] = _lane_blocks_to_rows(v)
    bonus_ref[0] = v * jnp.concatenate([rk] * SUBLANES, axis=1)
    lora = lora_ref[0]
    zw = jnp.tanh(lora[:, :2 * DECAY_LORA])
    za = lora[:, 2 * DECAY_LORA:]
    for d in range(2):
        wpre = w0_ref[d:d + 1, :] + jnp.dot(zw, w2_ref[d], preferred_element_type=F32,
                                            precision=lax.Precision.HIGHEST)
        apre = a0_ref[d:d + 1, :] + jnp.dot(za, a2_ref[d], preferred_element_type=F32,
                                            precision=lax.Precision.HIGHEST)
        nw = -wpre
        softplus = jnp.maximum(nw, 0.0) + jnp.log(1.0 + jnp.exp(-jnp.abs(nw)))
        w = -softplus - 0.5
        a = _sigmoid(apre)
        log_decay = -jnp.exp(w)
        tri = tri_ref[d]
        hi = log_decay.astype(BF16)
        rest = log_decay - hi.astype(F32)
        mid = rest.astype(BF16)
        low = (rest - mid.astype(F32)).astype(BF16)
        log_g = _dot(tri, hi) + _dot(tri, mid) + _dot(tri, low)
        g = jnp.exp(log_g)
        g_inv = jnp.exp(-log_g)
        a_t = -kk * jnp.exp(log_g - log_decay)
        b_t = kk * a * g_inv
        k_t = k * (1.0 + (a - 1.0) * vec[1:2]) * g_inv
        r_t = r * g
        ab_ref[d, 0] = _lane_blocks_to_rows(jnp.concatenate([a_t, b_t], axis=1))
        kr_ref[d, 0] = _lane_blocks_to_rows(jnp.concatenate([k_t, r_t], axis=1))
        gm_ref[d, 0] = g


def _rwkv_prep(r, k, v, lora, cwr, cwk, cwv, w0, w2, a0, a2, vec, sjj, sji, tm):
    b, n, _ = r.shape
    nh = n // SUBLANES

    def tile(w):
        return pl.BlockSpec((1, tm, w), lambda i, j: (i, j, 0))

    def halo_prev(w):
        return pl.BlockSpec((1, SUBLANES, w), lambda i, j: (i, jnp.maximum(j * (tm // SUBLANES) - 1, 0), 0))

    def halo_next(w):
        return pl.BlockSpec((1, SUBLANES, w), lambda i, j: (i, jnp.minimum((j + 1) * (tm // SUBLANES), nh - 1), 0))

    def dir_tile(w):
        return pl.BlockSpec((2, 1, tm, w), lambda i, j: (0, i, j, 0))

    t = np.arange(tm)
    same = (t[:, None] // SCAN_T) == (t[None, :] // SCAN_T)
    tri = jnp.asarray(np.stack([same & (t[None, :] <= t[:, None]), same & (t[None, :] >= t[:, None])]), BF16)
    def row_tiles(lead):
        return jax.ShapeDtypeStruct(lead + (n, SUBLANES, LANES), F32)

    def dir_row_tile():
        return pl.BlockSpec((2, 1, tm, SUBLANES, LANES), lambda i, j: (0, i, j, 0, 0))

    out_shape = [row_tiles((b,)), jax.ShapeDtypeStruct((b, n, I_W), F32), row_tiles((2, b)), row_tiles((2, b)),
                 jax.ShapeDtypeStruct((2, b, n, J_W), F32)]
    return pl.pallas_call(
        _rwkv_prep_kernel,
        out_shape=out_shape,
        grid=(b, n // tm),
        in_specs=[tile(J_W), halo_prev(J_W), halo_next(J_W),
                  tile(J_W), halo_prev(J_W), halo_next(J_W),
                  tile(I_W), halo_prev(I_W), halo_next(I_W),
                  tile(4 * DECAY_LORA),
                  _const_spec((CONV_W, J_W)), _const_spec((CONV_W, J_W)), _const_spec((CONV_W, I_W)),
                  _const_spec((2, J_W)), _const_spec((2, 2 * DECAY_LORA, J_W)),
                  _const_spec((2, J_W)), _const_spec((2, 2 * ICLR_LORA, J_W)),
                  _const_spec((3, J_W)), _const_spec((LANES, LANES)), _const_spec((LANES, LANES)),
                  _const_spec((2, tm, tm))],
        out_specs=[pl.BlockSpec((1, tm, SUBLANES, LANES), lambda i, j: (i, j, 0, 0)), tile(I_W),
                   dir_row_tile(), dir_row_tile(), dir_tile(J_W)],
        compiler_params=_cparams(("arbitrary", "arbitrary")),
        name="rwkv_prep",
    )(r, r, r, k, k, k, v, v, v, lora, cwr, cwk, cwv, w0, w2, a0, a2, vec, sjj, sji, tri)


N_ROWVEC = 4
N_PAIRS = N_ROWVEC // 2
N_DIR_REFS = N_PAIRS + 2
SCAN_PASSES = 1
DECAY_PASSES = 3
N_SG = 3
SG_PARTS = (((0, 0, 6, 0), (1, 0, 2, 6)),
            ((1, 2, 4, 0), (2, 0, 4, 4)),
            ((2, 4, 2, 0), (3, 0, 6, 2)))
SG_MIXED = 1


def _sg_expand_matrices():
    m = np.zeros((N_SG, 2 * LANES, E_W), np.float32)
    for sg, parts in enumerate(SG_PARTS):
        for half, (_, h0, nh, slot0) in enumerate(parts):
            for jl in range(4):
                for part in range(J_PARTS):
                    for h in range(h0, h0 + nh):
                        for rep in range(I_REP):
                            m[sg, half * LANES + jl * 32 + part * 8 + h,
                              jl * 128 + part * 32 + (slot0 + h - h0) * I_REP + rep] = 1.0
    return m


def _scan_kernel(*refs, t_chunk, n_batch):
    ins = refs[:2 * N_DIR_REFS]
    r2_ref, s0_ref = refs[2 * N_DIR_REFS:2 * N_DIR_REFS + 2]
    y_refs = refs[2 * N_DIR_REFS + 2:2 * N_DIR_REFS + 4]
    sout_ref = refs[2 * N_DIR_REFS + 4]
    e_ref, s_ref, ybuf_ref, yrev_ref = refs[2 * N_DIR_REFS + 5:]
    step_id = pl.program_id(0)
    assert n_batch == 2
    last = t_chunk - 1

    @pl.when(step_id == 0)
    def _():
        s_ref[...] = s0_ref[...]

    def src(x, s):
        return ins[(s // n_batch) * N_DIR_REFS + x]

    def expand(sg, first, second, passes):
        lhs = jnp.concatenate([first, second], axis=2)
        lhs = lhs.reshape(lhs.shape[0] * SUBLANES, 2 * LANES)
        return _split_dot(lhs, r2_ref[sg], passes).reshape(first.shape[0], SUBLANES, E_W)

    for sg, parts in enumerate(SG_PARTS):
        for pair in range(N_PAIRS):
            halves = []
            for s, _, _, _ in parts:
                ref = src(pair, s)
                if sg == SG_MIXED and s >= n_batch:
                    halves.append(jnp.concatenate([ref[0, s % n_batch, t:t + 1] for t in reversed(range(t_chunk))],
                                                  axis=0))
                else:
                    halves.append(ref[0, s % n_batch])
            e_ref[sg, pair] = expand(sg, halves[0], halves[1], SCAN_PASSES)

    lane = lax.broadcasted_iota(jnp.int32, (SUBLANES, LANES), 1)
    q_even = (lane // 32) % 2 == 0
    slot = (lane // I_REP) % HEAD_SLOTS

    def shifted(x, slots):
        return x if slots == 0 else pltpu.roll(x, (slots * I_REP) % LANES, 1)

    def allparts(p):
        return (p + pltpu.roll(p, 64, 1)) + (pltpu.roll(p, 32, 1) + pltpu.roll(p, 96, 1))

    def step(s, sg):
        (sa_src, ha, _, slot_a), (sb_src, hb, _, slot_b) = SG_PARTS[sg]
        ta = s if sa_src < n_batch else last - s
        tb = s if sb_src < n_batch else last - s
        te = s if sg == SG_MIXED else ta
        vp = jnp.where(slot < slot_b, shifted(src(N_PAIRS + 1, sa_src)[sa_src % n_batch, ta], slot_a - ha),
                       shifted(src(N_PAIRS + 1, sb_src)[sb_src % n_batch, tb], slot_b - hb))
        vr = pltpu.roll(vp, 32, 1)
        v = (jnp.where(q_even, vp, vr), jnp.where(q_even, vr, vp))
        g = sg
        t = te

        def row(x, jj):
            r0 = (x % 2) * 4 + jj // 4
            return e_ref[sg, x // 2, te, r0:r0 + 1, (jj % 4) * LANES:(jj % 4 + 1) * LANES]

        def add(acc, ih, jj, term):
            k = (ih, jj % 4)
            acc[k] = term if k not in acc else acc[k] + term

        def total(acc, ih):
            return allparts((acc[ih, 0] + acc[ih, 1]) + (acc[ih, 2] + acc[ih, 3]))

        acc = {}
        for jj in range(JJ):
            a = row(0, jj)
            for ih in range(I_HI):
                add(acc, ih, jj, s_ref[g, ih, jj] * a)
        sa = [total(acc, ih) for ih in range(I_HI)]
        acc = {}
        for jj in range(JJ):
            b, k, rr = row(1, jj), row(2, jj), row(3, jj)
            for ih in range(I_HI):
                new = (s_ref[g, ih, jj] + v[ih] * k) + sa[ih] * b
                s_ref[g, ih, jj] = new
                add(acc, ih, jj, new * rr)
        y = [total(acc, ih) for ih in range(I_HI)]
        packed = jnp.where(q_even, y[0], y[1])
        ybuf_ref[g, t] = packed
        if sg == SG_MIXED:
            yrev_ref[last - s] = packed

    def one_token(s, carry):
        for sg in range(N_SG):
            step(s, sg)
        return carry

    lax.fori_loop(0, t_chunk, one_token, 0, unroll=SCAN_UNROLL)

    for s in range(2 * n_batch):
        out = jnp.zeros((t_chunk, SUBLANES, LANES), F32)
        for sg, parts in enumerate(SG_PARTS):
            for s2, h0, nh, slot0 in parts:
                if s2 == s:
                    buf = yrev_ref[...] if (sg == SG_MIXED and s >= n_batch) else ybuf_ref[sg]
                    if h0 != slot0:
                        buf = pltpu.roll(buf, ((h0 - slot0) * I_REP) % LANES, 2)
                    out = jnp.where(jnp.logical_and(slot >= h0, slot < h0 + nh)[None], buf, out)
        y_refs[s // n_batch][s % n_batch] = out

    for sg, parts in enumerate(SG_PARTS):
        halves = []
        toks = []
        for s, _, _, _ in parts:
            tile0, tok = (t_chunk - SUBLANES, SUBLANES - 1) if s < n_batch else (0, 0)
            g8 = src(N_PAIRS, s)[0, s % n_batch, tile0:tile0 + SUBLANES, :]
            halves.append(_lane_blocks_to_rows(jnp.concatenate([g8, g8], axis=1)))
            toks.append(tok)
        res = expand(sg, halves[0], halves[1], DECAY_PASSES)
        first_rows, second_rows = res[toks[0]], res[toks[1]]
        for jj in range(JJ):
            r0 = jj // 4
            blk = slice((jj % 4) * LANES, (jj % 4 + 1) * LANES)
            scale = jnp.where(slot[0:1] < parts[1][3], first_rows[r0:r0 + 1, blk], second_rows[r0:r0 + 1, blk])
            for ih in range(I_HI):
                s_ref[sg, ih, jj] = s_ref[sg, ih, jj] * scale

    @pl.when(step_id == pl.num_programs(0) - 1)
    def _():
        sout_ref[...] = s_ref[...]


def _rwkv_scan(vt, ab, kr, gm, state0, r2):
    b, n = vt.shape[:2]
    t = SCAN_T
    nc = n // t
    in_specs = []
    args = []
    out_specs = []
    for d in range(2):
        def chunk(i, d=d):
            return i if d == 0 else nc - 1 - i

        pair_spec = pl.BlockSpec((1, b, t, SUBLANES, LANES), lambda i, d=d, chunk=chunk: (d, 0, chunk(i), 0, 0))
        gm_spec = pl.BlockSpec((1, b, t, J_W), lambda i, d=d, chunk=chunk: (d, 0, chunk(i), 0))
        tile_spec = pl.BlockSpec((b, t, SUBLANES, LANES), lambda i, chunk=chunk: (0, chunk(i), 0, 0))
        in_specs += [pair_spec, pair_spec, gm_spec, tile_spec]
        args += [ab, kr, gm, vt]
        out_specs.append(tile_spec)
    state_shape = (N_SG, I_HI, JJ, SUBLANES, LANES)
    in_specs += [_const_spec((N_SG, 2 * LANES, E_W)), _const_spec(state_shape)]
    args += [r2, state0]
    out_specs.append(_const_spec(state_shape))
    y_shape = jax.ShapeDtypeStruct((b, n, SUBLANES, LANES), F32)
    return pl.pallas_call(
        functools.partial(_scan_kernel, t_chunk=t, n_batch=b),
        out_shape=[y_shape, y_shape, jax.ShapeDtypeStruct(state_shape, F32)],
        grid=(nc,),
        in_specs=in_specs,
        out_specs=out_specs,
        scratch_shapes=[pltpu.VMEM((N_SG, N_PAIRS, t, SUBLANES, E_W), F32),
                        pltpu.VMEM(state_shape, F32),
                        pltpu.VMEM((N_SG, t, SUBLANES, LANES), F32),
                        pltpu.VMEM((t, SUBLANES, LANES), F32)],
        compiler_params=_cparams(("arbitrary",)),
        name="rwkv_scan",
    )(*args)


def _mix_out_kernel(x_ref, m_ref, g_ref, f_ref, a_ref, yf_ref, yb_ref, bonus_ref, zg_ref,
                    g2_ref, ln_ref, sii_ref, wf_ref, wa_ref, wr_ref, o_ref):
    x = x_ref[0]
    y = pltpu.einshape("tkl->t(kl)", yf_ref[0] + yb_ref[0])
    sii = sii_ref[...]
    nblk = I_W // LANES
    cnt = float(2 * HEAD_DIM)

    def fold(z):
        acc = z[:, 0:LANES]
        for c in range(1, nblk):
            acc = acc + z[:, c * LANES:(c + 1) * LANES]
        return acc

    def spread(z):
        return jnp.concatenate([z] * nblk, axis=1)

    mu = _split_dot(fold(y), sii, 2) * (1.0 / cnt)
    dlt = y - spread(mu)
    var = _split_dot(fold(dlt * dlt), sii, 2) * (1.0 / cnt)
    ln = ln_ref[...]
    yn = dlt * spread(lax.rsqrt(var + GN_EPS)) * ln[0:1] + ln[1:2]
    gate = _dot(_sigmoid(zg_ref[0]).astype(BF16), g2_ref[...])
    rw = ((yn + bonus_ref[0]) * gate).astype(BF16)
    o = _dot(f_ref[0].astype(BF16), wf_ref[...]) + _dot(a_ref[0].astype(BF16), wa_ref[...]) + _dot(rw, wr_ref[...])
    o_ref[0] = x + m_ref[0] * _rms(o, g_ref[...])


def _mix_out(x, gate_row, g3, fo, ao, y_f, y_b, bonus, zg, g2p, ln, sii, wf, wa, wr, tm):
    b, n, d = x.shape

    def tile(w):
        return pl.BlockSpec((1, tm, w), lambda i, j: (i, j, 0))

    return pl.pallas_call(
        _mix_out_kernel,
        out_shape=jax.ShapeDtypeStruct(x.shape, F32),
        grid=(b, n // tm),
        in_specs=[tile(d), pl.BlockSpec((1, 1, d), lambda i, j: (i, 0, 0)), _const_spec((1, d)),
                  tile(FOURIER_W), tile(ATTN_W),
                  pl.BlockSpec((1, tm, SUBLANES, LANES), lambda i, j: (i, j, 0, 0)),
                  pl.BlockSpec((1, tm, SUBLANES, LANES), lambda i, j: (i, j, 0, 0)),
                  tile(I_W), tile(GATE_LORA),
                  _const_spec((GATE_LORA, I_W)), _const_spec((2, I_W)), _const_spec((LANES, LANES)),
                  _const_spec((FOURIER_W, d)), _const_spec((ATTN_W, d)), _const_spec((I_W, d))],
        out_specs=tile(d),
        compiler_params=_cparams(("arbitrary", "arbitrary")),
        name="mix_out",
    )(x, gate_row, g3, fo, ao, y_f, y_b, bonus, zg, g2p, ln, sii, wf, wa, wr)


def _rope_tables(n_tokens):
    rows_n = n_tokens // GRID_W
    row = jnp.repeat(jnp.arange(rows_n), GRID_W).astype(F32)
    colp = jnp.tile(jnp.arange(GRID_W), rows_n).astype(F32)
    n_freq = HEAD_DIM // 4
    inv = ROPE_BASE ** (-jnp.arange(n_freq, dtype=F32) / n_freq)
    ang = jnp.concatenate([row[:, None] * inv, colp[:, None] * inv], axis=-1)
    cos = jnp.tile(jnp.cos(ang), (1, 4))
    sin = jnp.tile(jnp.sin(ang), (1, 4))
    return cos, sin


def _rot_cols(w):
    d, n = w.shape
    w4 = w.reshape(d, n // HEAD_DIM, 2, HEAD_DIM // 2)
    return jnp.stack([-w4[:, :, 1], w4[:, :, 0]], axis=2).reshape(d, n)


def _layer_weights(li, p):
    f = p["ffn1_wo"].shape[1]
    fp = -(-f // (2 * LANES)) * (2 * LANES)
    out = {}
    for name in ("ffn1", "ffn2"):
        wi = p[name + "_wi"][li]
        wo = p[name + "_wo"][li]
        out[name] = (jnp.pad(wi[:, :f], ((0, 0), (0, fp - f))).astype(BF16),
                     jnp.pad(wi[:, f:], ((0, 0), (0, fp - f))).astype(BF16),
                     jnp.pad(wo, ((0, fp - f), (0, 0))).astype(BF16))
    w_in = p["mix_w_in"][li]
    offs = np.cumsum([0, FOURIER_W, ATTN_W, KV_W, KV_W, RWKV_W, RWKV_W, RWKV_W,
                      DECAY_LORA, DECAY_LORA, ICLR_LORA, ICLR_LORA, GATE_LORA])
    part = [w_in[:, offs[i]:offs[i + 1]] for i in range(12)]
    wf, wq, wk, wv, wrr, wrk, wrv = part[:7]
    swap = lambda w: jnp.concatenate([w[:, HEAD_DIM:], w[:, :HEAD_DIM]], axis=1)
    cols = {"f": wf, "q": wq, "qr": _rot_cols(wq),
            "k": jnp.concatenate([wk, swap(wk)], axis=1),
            "kr": jnp.concatenate([_rot_cols(wk), swap(_rot_cols(wk))], axis=1),
            "v": jnp.concatenate([wv, swap(wv)], axis=1),
            "r": _take_cols(wrr, J_IDX), "rk": _take_cols(wrk, J_IDX), "rv": _take_cols(wrv, I_IDX),
            "lora": jnp.concatenate(part[7:11], axis=1), "g": part[11]}
    out["w_all"] = jnp.concatenate([cols[n] for n, _ in MIX_COLS], axis=1).astype(BF16)
    conv = p["rwkv_conv"][li]
    out["cwr"] = _take_cols(conv[:, :RWKV_W], J_IDX)
    out["cwk"] = _take_cols(conv[:, RWKV_W:2 * RWKV_W], J_IDX)
    out["cwv"] = _take_cols(conv[:, 2 * RWKV_W:], I_IDX)
    def per_direction(w2):
        w2 = _take_cols(w2, J_IDX)
        z = jnp.zeros_like(w2[0])
        return jnp.stack([jnp.concatenate([w2[0], z], axis=0), jnp.concatenate([z, w2[1]], axis=0)])

    out["w0"] = _take_cols(p["rwkv_w0"][li], J_IDX)
    out["w2"] = per_direction(p["rwkv_w2"][li])
    out["a0"] = _take_cols(p["rwkv_a0"][li], J_IDX)
    out["a2"] = per_direction(p["rwkv_a2"][li])
    out["vec"] = _take_cols(jnp.stack([p["rwkv_k_k"][li], p["rwkv_k_a"][li], p["rwkv_r_k"][li].reshape(-1)]), J_IDX)
    out["g2"] = _take_cols(p["rwkv_g2"][li], I_IDX).astype(BF16)
    out["ln"] = _take_cols(jnp.stack([p["rwkv_ln_g"][li], p["rwkv_ln_b"][li]]), I_IDX)
    w_out = p["mix_w_out"][li]
    out["wo_f"] = w_out[:FOURIER_W].astype(BF16)
    out["wo_a"] = w_out[FOURIER_W:FOURIER_W + ATTN_W].astype(BF16)
    out["wo_r"] = _take_rows(w_out[FOURIER_W + ATTN_W:], I_IDX_ONCE).astype(BF16)
    sink = p["attn_sink"][li]
    out["sink"] = jnp.concatenate([sink, jnp.zeros((8 - ATTN_HEADS,), F32)])
    return out


def kernel(x, c, ctx, c_ctx, mod_w, mod_b, norm_g, ffn1_wi, ffn1_wo, mix_w_in, mix_w_out, attn_sink,
           rwkv_conv, rwkv_w0, rwkv_w2, rwkv_a0, rwkv_a2, rwkv_g2, rwkv_k_k, rwkv_k_a, rwkv_r_k,
           rwkv_ln_g, rwkv_ln_b, ffn2_wi, ffn2_wo):
    p = dict(ffn1_wi=ffn1_wi, ffn1_wo=ffn1_wo, ffn2_wi=ffn2_wi, ffn2_wo=ffn2_wo, mix_w_in=mix_w_in,
             mix_w_out=mix_w_out, attn_sink=attn_sink, rwkv_conv=rwkv_conv, rwkv_w0=rwkv_w0, rwkv_w2=rwkv_w2,
             rwkv_a0=rwkv_a0, rwkv_a2=rwkv_a2, rwkv_g2=rwkv_g2, rwkv_k_k=rwkv_k_k, rwkv_k_a=rwkv_k_a,
             rwkv_r_k=rwkv_r_k, rwkv_ln_g=rwkv_ln_g, rwkv_ln_b=rwkv_ln_b)
    b, s, d = x.shape
    n_c = ctx.shape[1]
    depth = mod_w.shape[0]
    assert b + 1 <= 8 and s % WINDOW == 0 and n_c % WINDOW == 0 and WINDOW % SCAN_T == 0
    tm_l = 256
    tm_c = min(256, n_c)

    cond8 = jnp.zeros((8, d), F32).at[:b].set(c).at[b].set(c_ctx)
    mod = _ada_mod(cond8, mod_w, mod_b).reshape(depth, 8, N_MOD, d)
    cos_l, sin_l = _rope_tables(s)
    cos_c = jnp.ones((n_c, LANES), F32)
    sin_c = jnp.zeros((n_c, LANES), F32)
    r2 = jnp.asarray(_sg_expand_matrices(), BF16)
    sjj, sji, sii = (jnp.asarray(m, BF16) for m in _seg_matrices())
    zero_state = jnp.zeros((N_SG, I_HI, JJ, SUBLANES, LANES), F32)

    xl, xc = x, ctx
    for li in range(depth):
        need_ctx_out = li < depth - 1
        w = _layer_weights(li, p)
        ml = mod[li, :b]
        mc = jnp.broadcast_to(mod[li, b:b + 1], (b, N_MOD, d))
        g = norm_g[li]
        xl = _ffn_half(xl, ml[:, 0:3], g[0:2], *w["ffn1"], tm_l)
        xc = _ffn_half(xc, mc[:, 0:3], g[0:2], *w["ffn1"], tm_c)

        def mixer_in(xx, mm, cos, sin, tm):
            return _mix_in(xx, mm[:, 3:6], g[2:3], w["w_all"], cos, sin, tm)

        fl, ql, kl, vl, rl, rkl, rvl, loral, zgl = mixer_in(xl, ml, cos_l, sin_l, tm_l)
        fc, qc, kc, vc, rc_, rkc, rvc, lorac, zgc = mixer_in(xc, mc, cos_c, sin_c, tm_c)

        attn_l = _attention(ql, kl, vl, kc, vc, w["sink"], True)

        def prep(r_, k_, v_, lora_, tm):
            return _rwkv_prep(r_, k_, v_, lora_, w["cwr"], w["cwk"], w["cwv"], w["w0"], w["w2"],
                              w["a0"], w["a2"], w["vec"], sjj, sji, tm)

        pc = prep(rc_, rkc, rvc, lorac, tm_c)
        plat = prep(rl, rkl, rvl, loral, tm_l)
        ycf, ycb, state_c = _rwkv_scan(pc[0], *pc[2:], zero_state, r2)
        ylf, ylb, _ = _rwkv_scan(plat[0], *plat[2:], state_c, r2)

        def mixer_out(xx, mm, fo, ao, y_f, y_b, bonus, zg, tm):
            return _mix_out(xx, mm[:, 5:6], g[3:4], fo, ao, y_f, y_b, bonus, zg, w["g2"], w["ln"], sii,
                            w["wo_f"], w["wo_a"], w["wo_r"], tm)

        xl = mixer_out(xl, ml, _fourier_mix(fl), attn_l, ylf, ylb, plat[1], zgl, tm_l)
        if need_ctx_out:
            attn_c = _attention(qc, None, None, kc, vc, w["sink"], False)
            xc = mixer_out(xc, mc, _fourier_mix(fc), attn_c, ycf, ycb, pc[1], zgc, tm_c)
            xc = _ffn_half(xc, mc[:, 6:9], g[4:6], *w["ffn2"], tm_c)
        xl = _ffn_half(xl, ml[:, 6:9], g[4:6], *w["ffn2"], tm_l)
    return xl
```

```python
import functools

import numpy as np
import jax
import jax.numpy as jnp
from jax import lax
from jax.experimental import pallas as pl
from jax.experimental.pallas import tpu as pltpu

F32 = jnp.float32
BF16 = jnp.bfloat16

HEAD_DIM = 64
GRID_W = 64
FOURIER_W = 256
FOURIER_GROUPS = 4
ATTN_HEADS = 6
ATTN_KV_HEADS = 2
ATTN_GROUP = ATTN_HEADS // ATTN_KV_HEADS
ATTN_W = ATTN_HEADS * HEAD_DIM
KV_W = ATTN_KV_HEADS * HEAD_DIM
RWKV_HEADS = 6
RWKV_W = RWKV_HEADS * HEAD_DIM
WINDOW = 128
ROPE_BASE = 10000.0
DECAY_LORA = 64
ICLR_LORA = 64
GATE_LORA = 128
CONV_W = 3
N_MOD = 9
NORM_EPS = 1e-6
GN_EPS = 64e-5

LANES = 128
SUBLANES = 8
VMEM_LIMIT = 56 * 1024 * 1024

HEAD_SLOTS = 8
J_PARTS = 4
I_REP = 4
JJ = HEAD_DIM // J_PARTS
I_HI = HEAD_DIM // (SUBLANES * I_REP)
J_W = 4 * LANES
E_W = 4 * LANES
I_W = SUBLANES * LANES
SCAN_T = 64
SCAN_UNROLL = 32
FOURIER_N1 = 64
FOURIER_DIRECT_MAX = 256


def _j_layout():
    idx = np.full((J_W,), -1, np.int64)
    for jh in range(4):
        for jl in range(4):
            for part in range(J_PARTS):
                for h in range(RWKV_HEADS):
                    idx[jh * 128 + jl * 32 + part * 8 + h] = h * HEAD_DIM + part * JJ + jh * 4 + jl
    return idx


def _i_layout(dup):
    idx = np.full((I_W,), -1, np.int64)
    for isub in range(8):
        for q in range(4 if dup else 2):
            for h in range(RWKV_HEADS):
                for il in range(I_REP):
                    idx[isub * 128 + q * 32 + h * 4 + il] = h * HEAD_DIM + (q % 2) * 32 + isub * 4 + il
    return idx


J_IDX = _j_layout()
I_IDX = _i_layout(True)
I_IDX_ONCE = _i_layout(False)


def _take_cols(w, idx):
    g = jnp.take(w, jnp.asarray(np.maximum(idx, 0)), axis=-1)
    return g * jnp.asarray((idx >= 0).astype(np.float32))


def _take_rows(w, idx):
    g = jnp.take(w, jnp.asarray(np.maximum(idx, 0)), axis=0)
    return g * jnp.asarray((idx >= 0).astype(np.float32))[:, None]


def _expand_matrix():
    m = np.zeros((LANES, E_W), np.float32)
    for jl in range(4):
        for part in range(J_PARTS):
            for h in range(HEAD_SLOTS):
                for rep in range(I_REP):
                    m[jl * 32 + part * 8 + h, jl * 128 + part * 32 + h * I_REP + rep] = 1.0
    return m


def _seg_matrices():
    hj = np.arange(LANES) % 8
    hi = (np.arange(LANES) // I_REP) % 8
    jj = (hj[:, None] == hj[None, :]).astype(np.float32)
    ji = (hj[:, None] == hi[None, :]).astype(np.float32)
    ii = (hi[:, None] == hi[None, :]).astype(np.float32)
    return jj, ji, ii


def _cparams(sem, vmem=VMEM_LIMIT):
    return pltpu.CompilerParams(dimension_semantics=sem, vmem_limit_bytes=vmem)


def _const_spec(shape):
    nd = len(shape)
    return pl.BlockSpec(shape, lambda *_: (0,) * nd)


def _sigmoid(x):
    return 1.0 / (1.0 + jnp.exp(-x))


def _rms(x, g):
    return x * lax.rsqrt(jnp.mean(x * x, axis=-1, keepdims=True) + NORM_EPS) * g


def _dot(a, b):
    return jnp.dot(a, b, preferred_element_type=F32)


def _split_dot(x, m, passes):
    acc = None
    rem = x
    for p in range(passes):
        piece = rem.astype(BF16)
        d = _dot(piece, m)
        acc = d if acc is None else acc + d
        if p + 1 < passes:
            rem = rem - piece.astype(F32)
    return acc


def _ada_kernel(c_ref, w_ref, b_ref, o_ref):
    c = c_ref[...]
    s = c * _sigmoid(c)
    o_ref[0] = _dot(s.astype(BF16), w_ref[0].astype(BF16)) + b_ref[0]


def _ada_mod(cond8, mod_w, mod_b):
    nl, d, nw = mod_w.shape
    tn = nw // 8
    return pl.pallas_call(
        _ada_kernel,
        out_shape=jax.ShapeDtypeStruct((nl, 8, nw), F32),
        grid=(nl, nw // tn),
        in_specs=[pl.BlockSpec((8, d), lambda l, j: (0, 0)),
                  pl.BlockSpec((1, d, tn), lambda l, j: (l, 0, j)),
                  pl.BlockSpec((1, 1, tn), lambda l, j: (l, 0, j))],
        out_specs=pl.BlockSpec((1, 8, tn), lambda l, j: (l, 0, j)),
        compiler_params=_cparams(("arbitrary", "arbitrary")),
        name="ada_mod",
    )(cond8, mod_w, mod_b.reshape(nl, 1, nw))


def _ffn_kernel(x_ref, m_ref, g_ref, wg_ref, wu_ref, wo_ref, o_ref):
    x = x_ref[0]
    m = m_ref[0]
    g = g_ref[...]
    h = _rms(x, g[0:1]) * (1.0 + m[1:2]) + m[0:1]
    hb = h.astype(BF16)
    gate = _dot(hb, wg_ref[...])
    up = _dot(hb, wu_ref[...])
    act = (gate * _sigmoid(gate) * up).astype(BF16)
    y = _dot(act, wo_ref[...])
    o_ref[0] = x + 0.5 * m[2:3] * _rms(y, g[1:2])


def _ffn_half(x, m3, g2, wg, wu, wo, tm):
    b, n, d = x.shape
    fp = wg.shape[1]
    return pl.pallas_call(
        _ffn_kernel,
        out_shape=jax.ShapeDtypeStruct(x.shape, F32),
        grid=(b, n // tm),
        in_specs=[pl.BlockSpec((1, tm, d), lambda i, j: (i, j, 0)),
                  pl.BlockSpec((1, 3, d), lambda i, j: (i, 0, 0)),
                  _const_spec((2, d)),
                  _const_spec((d, fp)), _const_spec((d, fp)), _const_spec((fp, d))],
        out_specs=pl.BlockSpec((1, tm, d), lambda i, j: (i, j, 0)),
        compiler_params=_cparams(("arbitrary", "arbitrary")),
        name="ffn_half",
    )(x, m3, g2, wg, wu, wo)


MIX_COLS = (("f", FOURIER_W), ("q", ATTN_W), ("qr", ATTN_W), ("k", 2 * KV_W), ("kr", 2 * KV_W),
            ("v", 2 * KV_W), ("r", J_W), ("rk", J_W), ("rv", I_W), ("lora", 4 * DECAY_LORA), ("g", GATE_LORA))
MIX_OFF = {}
_o = 0
for _n, _w in MIX_COLS:
    MIX_OFF[_n] = (_o, _o + _w)
    _o += _w
MIX_TOTAL = _o


def _mix_in_kernel(x_ref, m_ref, g_ref, w_ref, cos_ref, sin_ref,
                   f_ref, q_ref, k_ref, v_ref, r_ref, rk_ref, rv_ref, lora_ref, zg_ref):
    x = x_ref[0]
    m = m_ref[0]
    h = _rms(x, g_ref[...]) * (1.0 + m[1:2]) + m[0:1]
    z = _dot(h.astype(BF16), w_ref[...])

    def col(name):
        lo, hi = MIX_OFF[name]
        return z[:, lo:hi]

    cos = cos_ref[...]
    sin = sin_ref[...]
    cos3 = jnp.concatenate([cos, cos, cos], axis=1)
    sin3 = jnp.concatenate([sin, sin, sin], axis=1)
    cos2 = jnp.concatenate([cos, cos], axis=1)
    sin2 = jnp.concatenate([sin, sin], axis=1)
    f_ref[0] = col("f")
    q_ref[0] = ((col("q") * cos3 + col("qr") * sin3) * (HEAD_DIM ** -0.5)).astype(BF16)
    k_ref[0] = (col("k") * cos2 + col("kr") * sin2).astype(BF16)
    v_ref[0] = col("v").astype(BF16)
    r_ref[0] = col("r")
    rk_ref[0] = col("rk")
    rv_ref[0] = col("rv")
    lora_ref[0] = col("lora")
    zg_ref[0] = col("g")


def _mix_in(x, m3, g1, w_all, cos, sin, tm):
    b, n, d = x.shape
    widths = (FOURIER_W, ATTN_W, 2 * KV_W, 2 * KV_W, J_W, J_W, I_W, 4 * DECAY_LORA, GATE_LORA)
    dtypes = (F32, BF16, BF16, BF16, F32, F32, F32, F32, F32)
    return pl.pallas_call(
        _mix_in_kernel,
        out_shape=[jax.ShapeDtypeStruct((b, n, w), dt) for w, dt in zip(widths, dtypes)],
        grid=(b, n // tm),
        in_specs=[pl.BlockSpec((1, tm, d), lambda i, j: (i, j, 0)),
                  pl.BlockSpec((1, 3, d), lambda i, j: (i, 0, 0)),
                  _const_spec((1, d)),
                  _const_spec((d, MIX_TOTAL)),
                  pl.BlockSpec((tm, LANES), lambda i, j: (j, 0)),
                  pl.BlockSpec((tm, LANES), lambda i, j: (j, 0))],
        out_specs=[pl.BlockSpec((1, tm, w), lambda i, j: (i, j, 0)) for w in widths],
        compiler_params=_cparams(("arbitrary", "arbitrary")),
        name="mix_in",
    )(x, m3, g1, w_all, cos, sin)


def _fourier_stage1_kernel(l_ref, x_ref, o_ref):
    o_ref[0] = _dot(l_ref[...], x_ref[0].astype(BF16))


def _fourier_stage2_kernel(a_ref, mc_ref, ms_ref, cc_ref, sc_ref, o_ref, *, scale):
    ar = a_ref[0, 0, 0].astype(BF16)
    ai = a_ref[0, 1, 0].astype(BF16)
    mc = mc_ref[0]
    ms = ms_ref[0]
    gr = _dot(mc, ar) + _dot(ms, ai)
    gi = _dot(mc, ai) - _dot(ms, ar)
    o_ref[0] = (_dot(gr.astype(BF16), cc_ref[...]) + _dot(gi.astype(BF16), sc_ref[...])) * scale


def _fourier_small_kernel(z_ref, cn_ref, sn_ref, cc_ref, sc_ref, o_ref, *, scale):
    z = z_ref[0].astype(BF16)
    t1 = _dot(z, cc_ref[...]).astype(BF16)
    t2 = _dot(z, sc_ref[...]).astype(BF16)
    o_ref[0] = (_dot(cn_ref[...], t1) - _dot(sn_ref[...], t2)) * scale


def _channel_dft():
    gw = FOURIER_W // FOURIER_GROUPS
    c = np.arange(FOURIER_W)
    same = (c[:, None] // gw) == (c[None, :] // gw)
    ang = 2.0 * np.pi * ((c[:, None] % gw) * (c[None, :] % gw) % gw) / gw
    return (np.cos(ang) * same).astype(np.float32), (np.sin(ang) * same).astype(np.float32)


def _fourier_mix(z):
    b, n, w = z.shape
    gw = w // FOURIER_GROUPS
    scale = float(1.0 / np.sqrt(n * gw))
    cc, sc = _channel_dft()
    cc = jnp.asarray(cc, BF16)
    sc = jnp.asarray(sc, BF16)
    if n <= FOURIER_DIRECT_MAX:
        p = np.arange(n)
        ang = 2.0 * np.pi * ((p[:, None] * p[None, :]) % n) / n
        return pl.pallas_call(
            functools.partial(_fourier_small_kernel, scale=scale),
            out_shape=jax.ShapeDtypeStruct((b, n, w), F32),
            grid=(b,),
            in_specs=[pl.BlockSpec((1, n, w), lambda i: (i, 0, 0)),
                      _const_spec((n, n)), _const_spec((n, n)), _const_spec((w, w)), _const_spec((w, w))],
            out_specs=pl.BlockSpec((1, n, w), lambda i: (i, 0, 0)),
            compiler_params=_cparams(("arbitrary",)),
            name="fourier_small",
        )(z, jnp.asarray(np.cos(ang), BF16), jnp.asarray(np.sin(ang), BF16), cc, sc)
    n1 = FOURIER_N1
    n2 = n // n1
    k1 = np.arange(n1)
    ang1 = 2.0 * np.pi * ((k1[:, None] * k1[None, :]) % n1) / n1
    lhs1 = jnp.asarray(np.concatenate([np.cos(ang1), -np.sin(ang1)], axis=0), BF16)
    tc = min(n2 * w, 4096)
    a = pl.pallas_call(
        _fourier_stage1_kernel,
        out_shape=jax.ShapeDtypeStruct((b, 2 * n1, n2 * w), F32),
        grid=(b, (n2 * w) // tc),
        in_specs=[_const_spec((2 * n1, n1)),
                  pl.BlockSpec((1, n1, tc), lambda i, j: (i, 0, j))],
        out_specs=pl.BlockSpec((1, 2 * n1, tc), lambda i, j: (i, 0, j)),
        compiler_params=_cparams(("arbitrary", "arbitrary")),
        name="fourier_stage1",
    )(lhs1, z.reshape(b, n1, n2 * w))
    k2 = np.arange(n2)
    freq = (k1[:, None, None] + n1 * k2[None, :, None]) * k2[None, None, :]
    ang2 = 2.0 * np.pi * (freq % n) / n
    out = pl.pallas_call(
        functools.partial(_fourier_stage2_kernel, scale=scale),
        out_shape=jax.ShapeDtypeStruct((b, n2, n1 * w), F32),
        grid=(b, n1),
        in_specs=[pl.BlockSpec((1, 2, 1, n2, w), lambda i, j: (i, 0, j, 0, 0)),
                  pl.BlockSpec((1, n2, n2), lambda i, j: (j, 0, 0)),
                  pl.BlockSpec((1, n2, n2), lambda i, j: (j, 0, 0)),
                  _const_spec((w, w)), _const_spec((w, w))],
        out_specs=pl.BlockSpec((1, n2, w), lambda i, j: (i, 0, j)),
        compiler_params=_cparams(("arbitrary", "arbitrary")),
        name="fourier_stage2",
    )(a.reshape(b, 2, n1, n2, w), jnp.asarray(np.cos(ang2), BF16), jnp.asarray(np.sin(ang2), BF16), cc, sc)
    return out.reshape(b, n, w)


NEG = -1e30


def _attn_kernel(*refs, window):
    if window:
        (sink_ref, q_ref, kp_ref, kc_ref, kn_ref, vp_ref, vc_ref, vn_ref, kx_ref, vx_ref, o_ref, s_ref) = refs
    else:
        (sink_ref, q_ref, kx_ref, vx_ref, o_ref, s_ref) = refs
    i = pl.program_id(1)
    nb = pl.num_programs(1)
    tq = q_ref.shape[1]
    lane = lax.broadcasted_iota(jnp.int32, (1, KV_W), 1)
    low = lane < HEAD_DIM
    if window:
        k_all = jnp.concatenate([kp_ref[0], kc_ref[0], kn_ref[0], kx_ref[0]], axis=0)
        v_all = jnp.concatenate([vp_ref[0], vc_ref[0], vn_ref[0], vx_ref[0]], axis=0)
        row = lax.broadcasted_iota(jnp.int32, (tq, WINDOW), 0)
        col = lax.broadcasted_iota(jnp.int32, (tq, WINDOW), 1)
        prev_ok = jnp.logical_and(col >= row, i > 0)
        next_ok = jnp.logical_and(col <= row, i < nb - 1)
        mask = jnp.concatenate([jnp.where(prev_ok, 0.0, NEG), jnp.zeros((tq, WINDOW), F32),
                                jnp.where(next_ok, 0.0, NEG), jnp.zeros((tq, kx_ref.shape[1]), F32)], axis=1)
    else:
        k_all = kx_ref[0]
        v_all = vx_ref[0]
        mask = None
    zero = jnp.zeros((), BF16)
    one = jnp.ones((), BF16)
    def operands(h):
        slot = h % 2
        swap = slot != h // ATTN_GROUP
        keep = (lane >= HEAD_DIM) if slot else low
        kh = jnp.where(keep, k_all[:, KV_W:] if swap else k_all[:, :KV_W], zero)
        vh = jnp.where(keep, v_all[:, KV_W:] if swap else v_all[:, :KV_W], one)
        return kh, vh

    def scores(h):
        qp = q_ref[0, :, (h // 2) * LANES:(h // 2 + 1) * LANES]
        s = lax.dot_general(qp, operands(h)[0], (((1,), (1,)), ((), ())), preferred_element_type=F32)
        return s if mask is None else s + mask

    for h in range(ATTN_HEADS):
        s_ref[h] = scores(h)
    for pair in range(ATTN_HEADS // 2):
        outs = []
        sinks = []
        for slot in range(2):
            h = 2 * pair + slot
            s = s_ref[h]
            sk = sink_ref[h]
            mx = jnp.maximum(s.max(axis=-1, keepdims=True), sk)
            outs.append(_dot(jnp.exp(s - mx).astype(BF16), operands(h)[1]))
            sinks.append(jnp.exp(sk - mx))
        num = jnp.where(low, outs[0], outs[1])
        den = pltpu.roll(jnp.where(low, outs[1], outs[0]), HEAD_DIM, 1) + jnp.where(low, sinks[0], sinks[1])
        o_ref[0, :, pair * LANES:(pair + 1) * LANES] = num / den


def _attention(q, k, v, kx, vx, sink8, window):
    b, n, _ = q.shape
    c = kx.shape[1]
    tq = WINDOW
    nb = n // tq
    smem = pl.BlockSpec(memory_space=pltpu.SMEM)
    qspec = pl.BlockSpec((1, tq, ATTN_W), lambda i, j: (i, j, 0))
    xspec = pl.BlockSpec((1, c, 2 * KV_W), lambda i, j: (i, 0, 0))
    if window:
        prev = pl.BlockSpec((1, tq, 2 * KV_W), lambda i, j: (i, jnp.maximum(j - 1, 0), 0))
        cur = pl.BlockSpec((1, tq, 2 * KV_W), lambda i, j: (i, j, 0))
        nxt = pl.BlockSpec((1, tq, 2 * KV_W), lambda i, j: (i, jnp.minimum(j + 1, nb - 1), 0))
        in_specs = [smem, qspec, prev, cur, nxt, prev, cur, nxt, xspec, xspec]
        args = (sink8, q, k, k, k, v, v, v, kx, vx)
    else:
        in_specs = [smem, qspec, xspec, xspec]
        args = (sink8, q, kx, vx)
    return pl.pallas_call(
        functools.partial(_attn_kernel, window=window),
        out_shape=jax.ShapeDtypeStruct((b, n, ATTN_W), F32),
        grid=(b, nb),
        in_specs=in_specs,
        out_specs=pl.BlockSpec((1, tq, ATTN_W), lambda i, j: (i, j, 0)),
        scratch_shapes=[pltpu.VMEM((ATTN_HEADS, tq, (3 * tq if window else 0) + c), F32)],
        compiler_params=_cparams(("arbitrary", "arbitrary")),
        name="window_attention" if window else "context_attention",
    )(*args)


def _lane_blocks_to_rows(x):
    return pltpu.einshape("t(kl)->tkl", x, l=LANES)


def _shift_rows(x, prev_row, next_row):
    n = x.shape[0]
    row = lax.broadcasted_iota(jnp.int32, (n, 1), 0)
    xp = jnp.where(row == 0, prev_row, pltpu.roll(x, 1, 0))
    xn = jnp.where(row == n - 1, next_row, pltpu.roll(x, n - 1, 0))
    return xp, xn


def _rwkv_prep_kernel(r_ref, rp_ref, rn_ref, k_ref, kp_ref, kn_ref, v_ref, vp_ref, vn_ref, lora_ref,
                      cwr_ref, cwk_ref, cwv_ref, w0_ref, w2_ref, a0_ref, a2_ref, vec_ref, sjj_ref, sji_ref, tri_ref,
                      vt_ref, bonus_ref, ab_ref, kr_ref, gm_ref):
    i = pl.program_id(1)
    first = (i > 0).astype(F32)
    last = (i < pl.num_programs(1) - 1).astype(F32)

    def conv(x_ref, p_ref, n_ref, cw_ref):
        x = x_ref[0]
        xp, xn = _shift_rows(x, p_ref[0, SUBLANES - 1:SUBLANES, :] * first, n_ref[0, 0:1, :] * last)
        cw = cw_ref[...]
        return xp * cw[0:1] + x * cw[1:2] + xn * cw[2:3]

    r = conv(r_ref, rp_ref, rn_ref, cwr_ref)
    k = conv(k_ref, kp_ref, kn_ref, cwk_ref)
    v = conv(v_ref, vp_ref, vn_ref, cwv_ref)
    vec = vec_ref[...]
    sjj = sjj_ref[...]
    sji = sji_ref[...]

    def fold(x):
        return (x[:, 0:LANES] + x[:, LANES:2 * LANES]) + (x[:, 2 * LANES:3 * LANES] + x[:, 3 * LANES:4 * LANES])

    kk = k * vec[0:1]
    ss = _split_dot(fold(kk * kk), sjj, 2)
    inv = lax.rsqrt(jnp.maximum(ss, 1e-24))
    kk = kk * jnp.concatenate([inv, inv, inv, inv], axis=1)
    rk = _split_dot(fold(r * k * vec[2:3]), sji, 2)
    vt_ref[0] = _lane_blocks_to_rows(v)
    bonus_ref[0] = v * jnp.concatenate([rk] * SUBLANES, axis=1)
    lora = lora_ref[0]
    zw = jnp.tanh(lora[:, :2 * DECAY_LORA])
    za = lora[:, 2 * DECAY_LORA:]
    zw_hi = zw.astype(BF16)
    zw_lo = (zw - zw_hi.astype(F32)).astype(BF16)
    za_hi = za.astype(BF16)
    za_lo = (za - za_hi.astype(F32)).astype(BF16)

    def lora(x_hi, x_lo, w_ref, d):
        return _dot(x_hi, w_ref[0, d]) + (_dot(x_lo, w_ref[0, d]) + _dot(x_hi, w_ref[1, d]))

    for d in range(2):
        wpre = w0_ref[d:d + 1, :] + lora(zw_hi, zw_lo, w2_ref, d)
        apre = a0_ref[d:d + 1, :] + lora(za_hi, za_lo, a2_ref, d)
        a = _sigmoid(apre)
        log_decay = -np.float32(np.exp(-0.5)) * _sigmoid(wpre)
        tri = tri_ref[d]
        hi = log_decay.astype(BF16)
        rest = log_decay - hi.astype(F32)
        mid = rest.astype(BF16)
        low = (rest - mid.astype(F32)).astype(BF16)
        log_g = _dot(tri, hi) + _dot(tri, mid) + _dot(tri, low)
        g = jnp.exp(log_g)
        g_inv = jnp.exp(-log_g)
        a_t = -kk * jnp.exp(log_g - log_decay)
        b_t = kk * a * g_inv
        k_t = k * (1.0 + (a - 1.0) * vec[1:2]) * g_inv
        r_t = r * g
        ab_ref[d, 0] = _lane_blocks_to_rows(jnp.concatenate([a_t, b_t], axis=1))
        kr_ref[d, 0] = _lane_blocks_to_rows(jnp.concatenate([k_t, r_t], axis=1))
        gm_ref[d, 0] = g


def _rwkv_prep(r, k, v, lora, cwr, cwk, cwv, w0, w2, a0, a2, vec, sjj, sji, tm):
    b, n, _ = r.shape
    nh = n // SUBLANES

    def tile(w):
        return pl.BlockSpec((1, tm, w), lambda i, j: (i, j, 0))

    def halo_prev(w):
        return pl.BlockSpec((1, SUBLANES, w), lambda i, j: (i, jnp.maximum(j * (tm // SUBLANES) - 1, 0), 0))

    def halo_next(w):
        return pl.BlockSpec((1, SUBLANES, w), lambda i, j: (i, jnp.minimum((j + 1) * (tm // SUBLANES), nh - 1), 0))

    def dir_tile(w):
        return pl.BlockSpec((2, 1, tm, w), lambda i, j: (0, i, j, 0))

    t = np.arange(tm)
    same = (t[:, None] // SCAN_T) == (t[None, :] // SCAN_T)
    tri = jnp.asarray(np.stack([same & (t[None, :] <= t[:, None]), same & (t[None, :] >= t[:, None])]), BF16)
    def row_tiles(lead):
        return jax.ShapeDtypeStruct(lead + (n, SUBLANES, LANES), F32)

    def dir_row_tile():
        return pl.BlockSpec((2, 1, tm, SUBLANES, LANES), lambda i, j: (0, i, j, 0, 0))

    out_shape = [row_tiles((b,)), jax.ShapeDtypeStruct((b, n, I_W), F32), row_tiles((2, b)), row_tiles((2, b)),
                 jax.ShapeDtypeStruct((2, b, n, J_W), F32)]
    return pl.pallas_call(
        _rwkv_prep_kernel,
        out_shape=out_shape,
        grid=(b, n // tm),
        in_specs=[tile(J_W), halo_prev(J_W), halo_next(J_W),
                  tile(J_W), halo_prev(J_W), halo_next(J_W),
                  tile(I_W), halo_prev(I_W), halo_next(I_W),
                  tile(4 * DECAY_LORA),
                  _const_spec((CONV_W, J_W)), _const_spec((CONV_W, J_W)), _const_spec((CONV_W, I_W)),
                  _const_spec((2, J_W)), _const_spec((2, 2, 2 * DECAY_LORA, J_W)),
                  _const_spec((2, J_W)), _const_spec((2, 2, 2 * ICLR_LORA, J_W)),
                  _const_spec((3, J_W)), _const_spec((LANES, LANES)), _const_spec((LANES, LANES)),
                  _const_spec((2, tm, tm))],
        out_specs=[pl.BlockSpec((1, tm, SUBLANES, LANES), lambda i, j: (i, j, 0, 0)), tile(I_W),
                   dir_row_tile(), dir_row_tile(), dir_tile(J_W)],
        compiler_params=_cparams(("arbitrary", "arbitrary")),
        name="rwkv_prep",
    )(r, r, r, k, k, k, v, v, v, lora, cwr, cwk, cwv, w0, w2, a0, a2, vec, sjj, sji, tri)


N_ROWVEC = 4
N_PAIRS = N_ROWVEC // 2
N_DIR_REFS = N_PAIRS + 2
SCAN_PASSES = 1
DECAY_PASSES = 3
N_SG = 3
SG_PARTS = (((0, 0, 6, 0), (1, 0, 2, 6)),
            ((1, 2, 4, 0), (2, 0, 4, 4)),
            ((2, 4, 2, 0), (3, 0, 6, 2)))
SG_MIXED = 1


def _sg_expand_matrices():
    m = np.zeros((N_SG, 2 * LANES, E_W), np.float32)
    for sg, parts in enumerate(SG_PARTS):
        for half, (_, h0, nh, slot0) in enumerate(parts):
            for jl in range(4):
                for part in range(J_PARTS):
                    for h in range(h0, h0 + nh):
                        for rep in range(I_REP):
                            m[sg, half * LANES + jl * 32 + part * 8 + h,
                              jl * 128 + part * 32 + (slot0 + h - h0) * I_REP + rep] = 1.0
    return m


def _scan_kernel(*refs, t_chunk, n_batch):
    ins = refs[:2 * N_DIR_REFS]
    r2_ref, s0_ref = refs[2 * N_DIR_REFS:2 * N_DIR_REFS + 2]
    y_refs = refs[2 * N_DIR_REFS + 2:2 * N_DIR_REFS + 4]
    sout_ref = refs[2 * N_DIR_REFS + 4]
    e_ref, s_ref, ybuf_ref, yrev_ref = refs[2 * N_DIR_REFS + 5:]
    step_id = pl.program_id(0)
    assert n_batch == 2
    last = t_chunk - 1

    @pl.when(step_id == 0)
    def _():
        s_ref[...] = s0_ref[...]

    def src(x, s):
        return ins[(s // n_batch) * N_DIR_REFS + x]

    def expand(sg, first, second, passes):
        lhs = jnp.concatenate([first, second], axis=2)
        lhs = lhs.reshape(lhs.shape[0] * SUBLANES, 2 * LANES)
        return _split_dot(lhs, r2_ref[sg], passes).reshape(first.shape[0], SUBLANES, E_W)

    for sg, parts in enumerate(SG_PARTS):
        for pair in range(N_PAIRS):
            halves = []
            for s, _, _, _ in parts:
                ref = src(pair, s)
                if sg == SG_MIXED and s >= n_batch:
                    halves.append(jnp.concatenate([ref[0, s % n_batch, t:t + 1] for t in reversed(range(t_chunk))],
                                                  axis=0))
                else:
                    halves.append(ref[0, s % n_batch])
            e_ref[sg, pair] = expand(sg, halves[0], halves[1], SCAN_PASSES)

    lane = lax.broadcasted_iota(jnp.int32, (SUBLANES, LANES), 1)
    q_even = (lane // 32) % 2 == 0
    slot = (lane // I_REP) % HEAD_SLOTS

    def shifted(x, slots):
        return x if slots == 0 else pltpu.roll(x, (slots * I_REP) % LANES, 1)

    def allparts(p):
        return (p + pltpu.roll(p, 64, 1)) + (pltpu.roll(p, 32, 1) + pltpu.roll(p, 96, 1))

    def step(s, sg):
        (sa_src, ha, _, slot_a), (sb_src, hb, _, slot_b) = SG_PARTS[sg]
        ta = s if sa_src < n_batch else last - s
        tb = s if sb_src < n_batch else last - s
        te = s if sg == SG_MIXED else ta
        vp = jnp.where(slot < slot_b, shifted(src(N_PAIRS + 1, sa_src)[sa_src % n_batch, ta], slot_a - ha),
                       shifted(src(N_PAIRS + 1, sb_src)[sb_src % n_batch, tb], slot_b - hb))
        vr = pltpu.roll(vp, 32, 1)
        v = (jnp.where(q_even, vp, vr), jnp.where(q_even, vr, vp))
        g = sg
        t = te

        def row(x, jj):
            r0 = (x % 2) * 4 + jj // 4
            return e_ref[sg, x // 2, te, r0:r0 + 1, (jj % 4) * LANES:(jj % 4 + 1) * LANES]

        def add(acc, ih, jj, term):
            k = (ih, jj % 4)
            acc[k] = term if k not in acc else acc[k] + term

        def total(acc, ih):
            return allparts((acc[ih, 0] + acc[ih, 1]) + (acc[ih, 2] + acc[ih, 3]))

        acc = {}
        for jj in range(JJ):
            a = row(0, jj)
            for ih in range(I_HI):
                add(acc, ih, jj, s_ref[g, ih, jj] * a)
        sa = [total(acc, ih) for ih in range(I_HI)]
        acc = {}
        for jj in range(JJ):
            b, k, rr = row(1, jj), row(2, jj), row(3, jj)
            for ih in range(I_HI):
                new = (s_ref[g, ih, jj] + v[ih] * k) + sa[ih] * b
                s_ref[g, ih, jj] = new
                add(acc, ih, jj, new * rr)
        y = [total(acc, ih) for ih in range(I_HI)]
        packed = jnp.where(q_even, y[0], y[1])
        ybuf_ref[g, t] = packed
        if sg == SG_MIXED:
            yrev_ref[last - s] = packed

    def one_token(s, carry):
        for sg in range(N_SG):
            step(s, sg)
        return carry

    lax.fori_loop(0, t_chunk, one_token, 0, unroll=SCAN_UNROLL)

    for s in range(2 * n_batch):
        out = jnp.zeros((t_chunk, SUBLANES, LANES), F32)
        for sg, parts in enumerate(SG_PARTS):
            for s2, h0, nh, slot0 in parts:
                if s2 == s:
                    buf = yrev_ref[...] if (sg == SG_MIXED and s >= n_batch) else ybuf_ref[sg]
                    if h0 != slot0:
                        buf = pltpu.roll(buf, ((h0 - slot0) * I_REP) % LANES, 2)
                    out = jnp.where(jnp.logical_and(slot >= h0, slot < h0 + nh)[None], buf, out)
        y_refs[s // n_batch][s % n_batch] = out

    for sg, parts in enumerate(SG_PARTS):
        halves = []
        toks = []
        for s, _, _, _ in parts:
            tile0, tok = (t_chunk - SUBLANES, SUBLANES - 1) if s < n_batch else (0, 0)
            g8 = src(N_PAIRS, s)[0, s % n_batch, tile0:tile0 + SUBLANES, :]
            halves.append(_lane_blocks_to_rows(jnp.concatenate([g8, g8], axis=1)))
            toks.append(tok)
        res = expand(sg, halves[0], halves[1], DECAY_PASSES)
        first_rows, second_rows = res[toks[0]], res[toks[1]]
        for jj in range(JJ):
            r0 = jj // 4
            blk = slice((jj % 4) * LANES, (jj % 4 + 1) * LANES)
            scale = jnp.where(slot[0:1] < parts[1][3], first_rows[r0:r0 + 1, blk], second_rows[r0:r0 + 1, blk])
            for ih in range(I_HI):
                s_ref[sg, ih, jj] = s_ref[sg, ih, jj] * scale

    @pl.when(step_id == pl.num_programs(0) - 1)
    def _():
        sout_ref[...] = s_ref[...]


def _rwkv_scan(vt, ab, kr, gm, state0, r2):
    b, n = vt.shape[:2]
    t = SCAN_T
    nc = n // t
    in_specs = []
    args = []
    out_specs = []
    for d in range(2):
        def chunk(i, d=d):
            return i if d == 0 else nc - 1 - i

        pair_spec = pl.BlockSpec((1, b, t, SUBLANES, LANES), lambda i, d=d, chunk=chunk: (d, 0, chunk(i), 0, 0))
        gm_spec = pl.BlockSpec((1, b, t, J_W), lambda i, d=d, chunk=chunk: (d, 0, chunk(i), 0))
        tile_spec = pl.BlockSpec((b, t, SUBLANES, LANES), lambda i, chunk=chunk: (0, chunk(i), 0, 0))
        in_specs += [pair_spec, pair_spec, gm_spec, tile_spec]
        args += [ab, kr, gm, vt]
        out_specs.append(tile_spec)
    state_shape = (N_SG, I_HI, JJ, SUBLANES, LANES)
    in_specs += [_const_spec((N_SG, 2 * LANES, E_W)), _const_spec(state_shape)]
    args += [r2, state0]
    out_specs.append(_const_spec(state_shape))
    y_shape = jax.ShapeDtypeStruct((b, n, SUBLANES, LANES), F32)
    return pl.pallas_call(
        functools.partial(_scan_kernel, t_chunk=t, n_batch=b),
        out_shape=[y_shape, y_shape, jax.ShapeDtypeStruct(state_shape, F32)],
        grid=(nc,),
        in_specs=in_specs,
        out_specs=out_specs,
        scratch_shapes=[pltpu.VMEM((N_SG, N_PAIRS, t, SUBLANES, E_W), F32),
                        pltpu.VMEM(state_shape, F32),
                        pltpu.VMEM((N_SG, t, SUBLANES, LANES), F32),
                        pltpu.VMEM((t, SUBLANES, LANES), F32)],
        compiler_params=_cparams(("arbitrary",)),
        name="rwkv_scan",
    )(*args)


def _mix_out_kernel(x_ref, m_ref, g_ref, f_ref, a_ref, yf_ref, yb_ref, bonus_ref, zg_ref,
                    g2_ref, ln_ref, sii_ref, wf_ref, wa_ref, wr_ref, o_ref):
    x = x_ref[0]
    y = pltpu.einshape("tkl->t(kl)", yf_ref[0] + yb_ref[0])
    sii = sii_ref[...]
    nblk = I_W // LANES
    cnt = float(2 * HEAD_DIM)

    def fold(z):
        acc = z[:, 0:LANES]
        for c in range(1, nblk):
            acc = acc + z[:, c * LANES:(c + 1) * LANES]
        return acc

    def spread(z):
        return jnp.concatenate([z] * nblk, axis=1)

    mu = _split_dot(fold(y), sii, 2) * (1.0 / cnt)
    dlt = y - spread(mu)
    var = _split_dot(fold(dlt * dlt), sii, 2) * (1.0 / cnt)
    ln = ln_ref[...]
    yn = dlt * spread(lax.rsqrt(var + GN_EPS)) * ln[0:1] + ln[1:2]
    gate = _dot(_sigmoid(zg_ref[0]).astype(BF16), g2_ref[...])
    rw = ((yn + bonus_ref[0]) * gate).astype(BF16)
    o = _dot(f_ref[0].astype(BF16), wf_ref[...]) + _dot(a_ref[0].astype(BF16), wa_ref[...]) + _dot(rw, wr_ref[...])
    o_ref[0] = x + m_ref[0] * _rms(o, g_ref[...])


def _mix_out(x, gate_row, g3, fo, ao, y_f, y_b, bonus, zg, g2p, ln, sii, wf, wa, wr, tm):
    b, n, d = x.shape

    def tile(w):
        return pl.BlockSpec((1, tm, w), lambda i, j: (i, j, 0))

    return pl.pallas_call(
        _mix_out_kernel,
        out_shape=jax.ShapeDtypeStruct(x.shape, F32),
        grid=(b, n // tm),
        in_specs=[tile(d), pl.BlockSpec((1, 1, d), lambda i, j: (i, 0, 0)), _const_spec((1, d)),
                  tile(FOURIER_W), tile(ATTN_W),
                  pl.BlockSpec((1, tm, SUBLANES, LANES), lambda i, j: (i, j, 0, 0)),
                  pl.BlockSpec((1, tm, SUBLANES, LANES), lambda i, j: (i, j, 0, 0)),
                  tile(I_W), tile(GATE_LORA),
                  _const_spec((GATE_LORA, I_W)), _const_spec((2, I_W)), _const_spec((LANES, LANES)),
                  _const_spec((FOURIER_W, d)), _const_spec((ATTN_W, d)), _const_spec((I_W, d))],
        out_specs=tile(d),
        compiler_params=_cparams(("arbitrary", "arbitrary")),
        name="mix_out",
    )(x, gate_row, g3, fo, ao, y_f, y_b, bonus, zg, g2p, ln, sii, wf, wa, wr)


def _rope_tables(n_tokens):
    rows_n = n_tokens // GRID_W
    row = jnp.repeat(jnp.arange(rows_n), GRID_W).astype(F32)
    colp = jnp.tile(jnp.arange(GRID_W), rows_n).astype(F32)
    n_freq = HEAD_DIM // 4
    inv = ROPE_BASE ** (-jnp.arange(n_freq, dtype=F32) / n_freq)
    ang = jnp.concatenate([row[:, None] * inv, colp[:, None] * inv], axis=-1)
    cos = jnp.tile(jnp.cos(ang), (1, 4))
    sin = jnp.tile(jnp.sin(ang), (1, 4))
    return cos, sin


def _rot_cols(w):
    d, n = w.shape
    w4 = w.reshape(d, n // HEAD_DIM, 2, HEAD_DIM // 2)
    return jnp.stack([-w4[:, :, 1], w4[:, :, 0]], axis=2).reshape(d, n)


def _layer_weights(li, p):
    f = p["ffn1_wo"].shape[1]
    fp = -(-f // (2 * LANES)) * (2 * LANES)
    out = {}
    for name in ("ffn1", "ffn2"):
        wi = p[name + "_wi"][li]
        wo = p[name + "_wo"][li]
        out[name] = (jnp.pad(wi[:, :f], ((0, 0), (0, fp - f))).astype(BF16),
                     jnp.pad(wi[:, f:], ((0, 0), (0, fp - f))).astype(BF16),
                     jnp.pad(wo, ((0, fp - f), (0, 0))).astype(BF16))
    w_in = p["mix_w_in"][li]
    offs = np.cumsum([0, FOURIER_W, ATTN_W, KV_W, KV_W, RWKV_W, RWKV_W, RWKV_W,
                      DECAY_LORA, DECAY_LORA, ICLR_LORA, ICLR_LORA, GATE_LORA])
    part = [w_in[:, offs[i]:offs[i + 1]] for i in range(12)]
    wf, wq, wk, wv, wrr, wrk, wrv = part[:7]
    swap = lambda w: jnp.concatenate([w[:, HEAD_DIM:], w[:, :HEAD_DIM]], axis=1)
    cols = {"f": wf, "q": wq, "qr": _rot_cols(wq),
            "k": jnp.concatenate([wk, swap(wk)], axis=1),
            "kr": jnp.concatenate([_rot_cols(wk), swap(_rot_cols(wk))], axis=1),
            "v": jnp.concatenate([wv, swap(wv)], axis=1),
            "r": _take_cols(wrr, J_IDX), "rk": _take_cols(wrk, J_IDX), "rv": _take_cols(wrv, I_IDX),
            "lora": jnp.concatenate(part[7:11], axis=1), "g": part[11]}
    out["w_all"] = jnp.concatenate([cols[n] for n, _ in MIX_COLS], axis=1).astype(BF16)
    conv = p["rwkv_conv"][li]
    out["cwr"] = _take_cols(conv[:, :RWKV_W], J_IDX)
    out["cwk"] = _take_cols(conv[:, RWKV_W:2 * RWKV_W], J_IDX)
    out["cwv"] = _take_cols(conv[:, 2 * RWKV_W:], I_IDX)
    def per_direction(w2):
        w2 = _take_cols(w2, J_IDX)
        z = jnp.zeros_like(w2[0])
        w2 = jnp.stack([jnp.concatenate([w2[0], z], axis=0), jnp.concatenate([z, w2[1]], axis=0)])
        hi = w2.astype(BF16)
        return jnp.stack([hi, (w2 - hi.astype(F32)).astype(BF16)])

    out["w0"] = _take_cols(p["rwkv_w0"][li], J_IDX)
    out["w2"] = per_direction(p["rwkv_w2"][li])
    out["a0"] = _take_cols(p["rwkv_a0"][li], J_IDX)
    out["a2"] = per_direction(p["rwkv_a2"][li])
    out["vec"] = _take_cols(jnp.stack([p["rwkv_k_k"][li], p["rwkv_k_a"][li], p["rwkv_r_k"][li].reshape(-1)]), J_IDX)
    out["g2"] = _take_cols(p["rwkv_g2"][li], I_IDX).astype(BF16)
    out["ln"] = _take_cols(jnp.stack([p["rwkv_ln_g"][li], p["rwkv_ln_b"][li]]), I_IDX)
    w_out = p["mix_w_out"][li]
    out["wo_f"] = w_out[:FOURIER_W].astype(BF16)
    out["wo_a"] = w_out[FOURIER_W:FOURIER_W + ATTN_W].astype(BF16)
    out["wo_r"] = _take_rows(w_out[FOURIER_W + ATTN_W:], I_IDX_ONCE).astype(BF16)
    sink = p["attn_sink"][li]
    out["sink"] = jnp.concatenate([sink, jnp.zeros((8 - ATTN_HEADS,), F32)])
    return out


def kernel(x, c, ctx, c_ctx, mod_w, mod_b, norm_g, ffn1_wi, ffn1_wo, mix_w_in, mix_w_out, attn_sink,
           rwkv_conv, rwkv_w0, rwkv_w2, rwkv_a0, rwkv_a2, rwkv_g2, rwkv_k_k, rwkv_k_a, rwkv_r_k,
           rwkv_ln_g, rwkv_ln_b, ffn2_wi, ffn2_wo):
    p = dict(ffn1_wi=ffn1_wi, ffn1_wo=ffn1_wo, ffn2_wi=ffn2_wi, ffn2_wo=ffn2_wo, mix_w_in=mix_w_in,
             mix_w_out=mix_w_out, attn_sink=attn_sink, rwkv_conv=rwkv_conv, rwkv_w0=rwkv_w0, rwkv_w2=rwkv_w2,
             rwkv_a0=rwkv_a0, rwkv_a2=rwkv_a2, rwkv_g2=rwkv_g2, rwkv_k_k=rwkv_k_k, rwkv_k_a=rwkv_k_a,
             rwkv_r_k=rwkv_r_k, rwkv_ln_g=rwkv_ln_g, rwkv_ln_b=rwkv_ln_b)
    b, s, d = x.shape
    n_c = ctx.shape[1]
    depth = mod_w.shape[0]
    assert b + 1 <= 8 and s % WINDOW == 0 and n_c % WINDOW == 0 and WINDOW % SCAN_T == 0
    tm_l = 256
    tm_c = min(256, n_c)

    cond8 = jnp.zeros((8, d), F32).at[:b].set(c).at[b].set(c_ctx)
    mod = _ada_mod(cond8, mod_w, mod_b).reshape(depth, 8, N_MOD, d)
    cos_l, sin_l = _rope_tables(s)
    cos_c = jnp.ones((n_c, LANES), F32)
    sin_c = jnp.zeros((n_c, LANES), F32)
    r2 = jnp.asarray(_sg_expand_matrices(), BF16)
    sjj, sji, sii = (jnp.asarray(m, BF16) for m in _seg_matrices())
    zero_state = jnp.zeros((N_SG, I_HI, JJ, SUBLANES, LANES), F32)

    xl, xc = x, ctx
    for li in range(depth):
        need_ctx_out = li < depth - 1
        w = _layer_weights(li, p)
        ml = mod[li, :b]
        mc = jnp.broadcast_to(mod[li, b:b + 1], (b, N_MOD, d))
        g = norm_g[li]
        xl = _ffn_half(xl, ml[:, 0:3], g[0:2], *w["ffn1"], tm_l)
        xc = _ffn_half(xc, mc[:, 0:3], g[0:2], *w["ffn1"], tm_c)

        def mixer_in(xx, mm, cos, sin, tm):
            return _mix_in(xx, mm[:, 3:6], g[2:3], w["w_all"], cos, sin, tm)

        fl, ql, kl, vl, rl, rkl, rvl, loral, zgl = mixer_in(xl, ml, cos_l, sin_l, tm_l)
        fc, qc, kc, vc, rc_, rkc, rvc, lorac, zgc = mixer_in(xc, mc, cos_c, sin_c, tm_c)

        attn_l = _attention(ql, kl, vl, kc, vc, w["sink"], True)

        def prep(r_, k_, v_, lora_, tm):
            return _rwkv_prep(r_, k_, v_, lora_, w["cwr"], w["cwk"], w["cwv"], w["w0"], w["w2"],
                              w["a0"], w["a2"], w["vec"], sjj, sji, tm)

        pc = prep(rc_, rkc, rvc, lorac, tm_c)
        plat = prep(rl, rkl, rvl, loral, tm_l)
        ycf, ycb, state_c = _rwkv_scan(pc[0], *pc[2:], zero_state, r2)
        ylf, ylb, _ = _rwkv_scan(plat[0], *plat[2:], state_c, r2)

        def mixer_out(xx, mm, fo, ao, y_f, y_b, bonus, zg, tm):
            return _mix_out(xx, mm[:, 5:6], g[3:4], fo, ao, y_f, y_b, bonus, zg, w["g2"], w["ln"], sii,
                            w["wo_f"], w["wo_a"], w["wo_r"], tm)

        xl = mixer_out(xl, ml, _fourier_mix(fl), attn_l, ylf, ylb, plat[1], zgl, tm_l)
        if need_ctx_out:
            attn_c = _attention(qc, None, None, kc, vc, w["sink"], False)
            xc = mixer_out(xc, mc, _fourier_mix(fc), attn_c, ycf, ycb, pc[1], zgc, tm_c)
            xc = _ffn_half(xc, mc[:, 6:9], g[4:6], *w["ffn2"], tm_c)
        xl = _ffn_half(xl, ml[:, 6:9], g[4:6], *w["ffn2"], tm_l)
    return xl
```

```python
import functools

import numpy as np
import jax
import jax.numpy as jnp
from jax import lax
from jax.experimental import pallas as pl
from jax.experimental.pallas import tpu as pltpu

F32 = jnp.float32
BF16 = jnp.bfloat16

HEAD_DIM = 64
GRID_W = 64
FOURIER_W = 256
FOURIER_GROUPS = 4
ATTN_HEADS = 6
ATTN_KV_HEADS = 2
ATTN_GROUP = ATTN_HEADS // ATTN_KV_HEADS
ATTN_W = ATTN_HEADS * HEAD_DIM
KV_W = ATTN_KV_HEADS * HEAD_DIM
RWKV_HEADS = 6
RWKV_W = RWKV_HEADS * HEAD_DIM
WINDOW = 128
ROPE_BASE = 10000.0
DECAY_LORA = 64
ICLR_LORA = 64
GATE_LORA = 128
CONV_W = 3
N_MOD = 9
NORM_EPS = 1e-6
GN_EPS = 64e-5

LANES = 128
SUBLANES = 8
VMEM_LIMIT = 56 * 1024 * 1024

HEAD_SLOTS = 8
J_PARTS = 4
I_REP = 4
JJ = HEAD_DIM // J_PARTS
I_HI = HEAD_DIM // (SUBLANES * I_REP)
J_W = 4 * LANES
E_W = 4 * LANES
I_W = SUBLANES * LANES
SCAN_T = 64
SCAN_UNROLL = 32
FOURIER_N1 = 64
FOURIER_DIRECT_MAX = 256
FOURIER_K1_BLOCK = 4


def _j_layout():
    idx = np.full((J_W,), -1, np.int64)
    for jh in range(4):
        for jl in range(4):
            for part in range(J_PARTS):
                for h in range(RWKV_HEADS):
                    idx[jh * 128 + jl * 32 + part * 8 + h] = h * HEAD_DIM + part * JJ + jh * 4 + jl
    return idx


def _i_layout(dup):
    idx = np.full((I_W,), -1, np.int64)
    for isub in range(8):
        for q in range(4 if dup else 2):
            for h in range(RWKV_HEADS):
                for il in range(I_REP):
                    idx[isub * 128 + q * 32 + h * 4 + il] = h * HEAD_DIM + (q % 2) * 32 + isub * 4 + il
    return idx


J_IDX = _j_layout()
I_IDX = _i_layout(True)
I_IDX_ONCE = _i_layout(False)


def _take_cols(w, idx):
    g = jnp.take(w, jnp.asarray(np.maximum(idx, 0)), axis=-1)
    return g * jnp.asarray((idx >= 0).astype(np.float32))


def _take_rows(w, idx):
    g = jnp.take(w, jnp.asarray(np.maximum(idx, 0)), axis=0)
    return g * jnp.asarray((idx >= 0).astype(np.float32))[:, None]


def _expand_matrix():
    m = np.zeros((LANES, E_W), np.float32)
    for jl in range(4):
        for part in range(J_PARTS):
            for h in range(HEAD_SLOTS):
                for rep in range(I_REP):
                    m[jl * 32 + part * 8 + h, jl * 128 + part * 32 + h * I_REP + rep] = 1.0
    return m


def _seg_matrices():
    hj = np.arange(LANES) % 8
    hi = (np.arange(LANES) // I_REP) % 8
    jj = (hj[:, None] == hj[None, :]).astype(np.float32)
    ji = (hj[:, None] == hi[None, :]).astype(np.float32)
    ii = (hi[:, None] == hi[None, :]).astype(np.float32)
    return jj, ji, ii


def _cparams(sem, vmem=VMEM_LIMIT):
    return pltpu.CompilerParams(dimension_semantics=sem, vmem_limit_bytes=vmem)


def _const_spec(shape):
    nd = len(shape)
    return pl.BlockSpec(shape, lambda *_: (0,) * nd)


def _resident_spec(shape):
    nd = len(shape)
    return pl.BlockSpec(shape, lambda *_: (0,) * nd, pipeline_mode=pl.Buffered(1))


def _sigmoid(x):
    return 1.0 / (1.0 + jnp.exp(-x))


def _rms(x, g):
    return x * lax.rsqrt(jnp.mean(x * x, axis=-1, keepdims=True) + NORM_EPS) * g


def _dot(a, b):
    return jnp.dot(a, b, preferred_element_type=F32)


def _split_dot(x, m, passes):
    acc = None
    rem = x
    for p in range(passes):
        piece = rem.astype(BF16)
        d = _dot(piece, m)
        acc = d if acc is None else acc + d
        if p + 1 < passes:
            rem = rem - piece.astype(F32)
    return acc


def _ada_kernel(c_ref, w_ref, b_ref, o_ref):
    c = c_ref[...]
    s = c * _sigmoid(c)
    o_ref[0] = _dot(s.astype(BF16), w_ref[0].astype(BF16)) + b_ref[0]


def _ada_mod(cond8, mod_w, mod_b):
    nl, d, nw = mod_w.shape
    tn = nw // 8
    return pl.pallas_call(
        _ada_kernel,
        out_shape=jax.ShapeDtypeStruct((nl, 8, nw), F32),
        grid=(nl, nw // tn),
        in_specs=[pl.BlockSpec((8, d), lambda l, j: (0, 0)),
                  pl.BlockSpec((1, d, tn), lambda l, j: (l, 0, j)),
                  pl.BlockSpec((1, 1, tn), lambda l, j: (l, 0, j))],
        out_specs=pl.BlockSpec((1, 8, tn), lambda l, j: (l, 0, j)),
        compiler_params=_cparams(("arbitrary", "arbitrary")),
        name="ada_mod",
    )(cond8, mod_w, mod_b.reshape(nl, 1, nw))


def _ffn_kernel(x_ref, m_ref, g_ref, wg_ref, wu_ref, wo_ref, o_ref):
    x = x_ref[0]
    m = m_ref[0]
    g = g_ref[...]
    h = _rms(x, g[0:1]) * (1.0 + m[1:2]) + m[0:1]
    hb = h.astype(BF16)
    gate = _dot(hb, wg_ref[...])
    up = _dot(hb, wu_ref[...])
    act = (gate * _sigmoid(gate) * up).astype(BF16)
    y = _dot(act, wo_ref[...])
    o_ref[0] = x + 0.5 * m[2:3] * _rms(y, g[1:2])


def _ffn_half(x, m3, g2, wg, wu, wo, tm):
    b, n, d = x.shape
    fp = wg.shape[1]
    return pl.pallas_call(
        _ffn_kernel,
        out_shape=jax.ShapeDtypeStruct(x.shape, F32),
        grid=(b, n // tm),
        in_specs=[pl.BlockSpec((1, tm, d), lambda i, j: (i, j, 0)),
                  pl.BlockSpec((1, 3, d), lambda i, j: (i, 0, 0)),
                  _const_spec((2, d)),
                  _resident_spec((d, fp)), _resident_spec((d, fp)), _resident_spec((fp, d))],
        out_specs=pl.BlockSpec((1, tm, d), lambda i, j: (i, j, 0)),
        compiler_params=_cparams(("arbitrary", "arbitrary")),
        name="ffn_half",
    )(x, m3, g2, wg, wu, wo)


MIX_COLS = (("f", FOURIER_W), ("q", ATTN_W), ("qr", ATTN_W), ("k", 2 * KV_W), ("kr", 2 * KV_W),
            ("v", 2 * KV_W), ("r", J_W), ("rk", J_W), ("rv", I_W), ("lora", 4 * DECAY_LORA), ("g", GATE_LORA))
MIX_OFF = {}
_o = 0
for _n, _w in MIX_COLS:
    MIX_OFF[_n] = (_o, _o + _w)
    _o += _w
MIX_TOTAL = _o


def _mix_in_kernel(x_ref, m_ref, g_ref, w_ref, cos_ref, sin_ref,
                   f_ref, q_ref, k_ref, v_ref, r_ref, rk_ref, rv_ref, lora_ref, zg_ref):
    x = x_ref[0]
    m = m_ref[0]
    h = _rms(x, g_ref[...]) * (1.0 + m[1:2]) + m[0:1]
    z = _dot(h.astype(BF16), w_ref[...])

    def col(name):
        lo, hi = MIX_OFF[name]
        return z[:, lo:hi]

    cos = cos_ref[...]
    sin = sin_ref[...]
    cos3 = jnp.concatenate([cos, cos, cos], axis=1)
    sin3 = jnp.concatenate([sin, sin, sin], axis=1)
    cos2 = jnp.concatenate([cos, cos], axis=1)
    sin2 = jnp.concatenate([sin, sin], axis=1)
    f_ref[0] = col("f")
    q_ref[0] = ((col("q") * cos3 + col("qr") * sin3) * (HEAD_DIM ** -0.5)).astype(BF16)
    k_ref[0] = (col("k") * cos2 + col("kr") * sin2).astype(BF16)
    v_ref[0] = col("v").astype(BF16)
    r_ref[0] = col("r")
    rk_ref[0] = col("rk")
    rv_ref[0] = col("rv")
    lora_ref[0] = col("lora")
    zg_ref[0] = col("g")


def _mix_in(x, m3, g1, w_all, cos, sin, tm):
    b, n, d = x.shape
    widths = (FOURIER_W, ATTN_W, 2 * KV_W, 2 * KV_W, J_W, J_W, I_W, 4 * DECAY_LORA, GATE_LORA)
    dtypes = (F32, BF16, BF16, BF16, F32, F32, F32, F32, F32)
    return pl.pallas_call(
        _mix_in_kernel,
        out_shape=[jax.ShapeDtypeStruct((b, n, w), dt) for w, dt in zip(widths, dtypes)],
        grid=(b, n // tm),
        in_specs=[pl.BlockSpec((1, tm, d), lambda i, j: (i, j, 0)),
                  pl.BlockSpec((1, 3, d), lambda i, j: (i, 0, 0)),
                  _const_spec((1, d)),
                  _const_spec((d, MIX_TOTAL)),
                  pl.BlockSpec((tm, LANES), lambda i, j: (j, 0)),
                  pl.BlockSpec((tm, LANES), lambda i, j: (j, 0))],
        out_specs=[pl.BlockSpec((1, tm, w), lambda i, j: (i, j, 0)) for w in widths],
        compiler_params=_cparams(("arbitrary", "arbitrary")),
        name="mix_in",
    )(x, m3, g1, w_all, cos, sin)


def _fourier_stage1_kernel(l_ref, x_ref, o_ref):
    o_ref[0] = _dot(l_ref[...], x_ref[0].astype(BF16))


def _fourier_stage2_kernel(a_ref, mc_ref, ms_ref, cc_ref, sc_ref, o_ref, *, scale):
    w = cc_ref.shape[0]
    for j in range(a_ref.shape[2]):
        ar = a_ref[0, 0, j].astype(BF16)
        ai = a_ref[0, 1, j].astype(BF16)
        mc = mc_ref[j]
        ms = ms_ref[j]
        gr = _dot(mc, ar) + _dot(ms, ai)
        gi = _dot(mc, ai) - _dot(ms, ar)
        o_ref[0, :, j * w:(j + 1) * w] = (_dot(gr.astype(BF16), cc_ref[...]) + _dot(gi.astype(BF16), sc_ref[...])) * scale


def _fourier_small_kernel(z_ref, cn_ref, sn_ref, cc_ref, sc_ref, o_ref, *, scale):
    z = z_ref[0].astype(BF16)
    t1 = _dot(z, cc_ref[...]).astype(BF16)
    t2 = _dot(z, sc_ref[...]).astype(BF16)
    o_ref[0] = (_dot(cn_ref[...], t1) - _dot(sn_ref[...], t2)) * scale


def _channel_dft():
    gw = FOURIER_W // FOURIER_GROUPS
    c = np.arange(FOURIER_W)
    same = (c[:, None] // gw) == (c[None, :] // gw)
    ang = 2.0 * np.pi * ((c[:, None] % gw) * (c[None, :] % gw) % gw) / gw
    return (np.cos(ang) * same).astype(np.float32), (np.sin(ang) * same).astype(np.float32)


def _fourier_mix(z):
    b, n, w = z.shape
    gw = w // FOURIER_GROUPS
    scale = float(1.0 / np.sqrt(n * gw))
    cc, sc = _channel_dft()
    cc = jnp.asarray(cc, BF16)
    sc = jnp.asarray(sc, BF16)
    if n <= FOURIER_DIRECT_MAX:
        p = np.arange(n)
        ang = 2.0 * np.pi * ((p[:, None] * p[None, :]) % n) / n
        return pl.pallas_call(
            functools.partial(_fourier_small_kernel, scale=scale),
            out_shape=jax.ShapeDtypeStruct((b, n, w), F32),
            grid=(b,),
            in_specs=[pl.BlockSpec((1, n, w), lambda i: (i, 0, 0)),
                      _const_spec((n, n)), _const_spec((n, n)), _const_spec((w, w)), _const_spec((w, w))],
            out_specs=pl.BlockSpec((1, n, w), lambda i: (i, 0, 0)),
            compiler_params=_cparams(("arbitrary",)),
            name="fourier_small",
        )(z, jnp.asarray(np.cos(ang), BF16), jnp.asarray(np.sin(ang), BF16), cc, sc)
    n1 = FOURIER_N1
    n2 = n // n1
    k1 = np.arange(n1)
    ang1 = 2.0 * np.pi * ((k1[:, None] * k1[None, :]) % n1) / n1
    lhs1 = jnp.asarray(np.concatenate([np.cos(ang1), -np.sin(ang1)], axis=0), BF16)
    tc = min(n2 * w, 4096)
    a = pl.pallas_call(
        _fourier_stage1_kernel,
        out_shape=jax.ShapeDtypeStruct((b, 2 * n1, n2 * w), F32),
        grid=(b, (n2 * w) // tc),
        in_specs=[_const_spec((2 * n1, n1)),
                  pl.BlockSpec((1, n1, tc), lambda i, j: (i, 0, j))],
        out_specs=pl.BlockSpec((1, 2 * n1, tc), lambda i, j: (i, 0, j)),
        compiler_params=_cparams(("arbitrary", "arbitrary")),
        name="fourier_stage1",
    )(lhs1, z.reshape(b, n1, n2 * w))
    k2 = np.arange(n2)
    freq = (k1[:, None, None] + n1 * k2[None, :, None]) * k2[None, None, :]
    ang2 = 2.0 * np.pi * (freq % n) / n
    out = pl.pallas_call(
        functools.partial(_fourier_stage2_kernel, scale=scale),
        out_shape=jax.ShapeDtypeStruct((b, n2, n1 * w), F32),
        grid=(b, n1 // FOURIER_K1_BLOCK),
        in_specs=[pl.BlockSpec((1, 2, FOURIER_K1_BLOCK, n2, w), lambda i, j: (i, 0, j, 0, 0)),
                  pl.BlockSpec((FOURIER_K1_BLOCK, n2, n2), lambda i, j: (j, 0, 0)),
                  pl.BlockSpec((FOURIER_K1_BLOCK, n2, n2), lambda i, j: (j, 0, 0)),
                  _const_spec((w, w)), _const_spec((w, w))],
        out_specs=pl.BlockSpec((1, n2, FOURIER_K1_BLOCK * w), lambda i, j: (i, 0, j)),
        compiler_params=_cparams(("arbitrary", "arbitrary")),
        name="fourier_stage2",
    )(a.reshape(b, 2, n1, n2, w), jnp.asarray(np.cos(ang2), BF16), jnp.asarray(np.sin(ang2), BF16), cc, sc)
    return out.reshape(b, n, w)


NEG = -1e30


def _attn_kernel(*refs, window):
    if window:
        (sink_ref, q_ref, kp_ref, kc_ref, kn_ref, vp_ref, vc_ref, vn_ref, kx_ref, vx_ref, o_ref, s_ref) = refs
    else:
        (sink_ref, q_ref, kx_ref, vx_ref, o_ref, s_ref) = refs
    i = pl.program_id(1)
    nb = pl.num_programs(1)
    tq = q_ref.shape[1]
    lane = lax.broadcasted_iota(jnp.int32, (1, KV_W), 1)
    low = lane < HEAD_DIM
    if window:
        k_all = jnp.concatenate([kp_ref[0], kc_ref[0], kn_ref[0], kx_ref[0]], axis=0)
        v_all = jnp.concatenate([vp_ref[0], vc_ref[0], vn_ref[0], vx_ref[0]], axis=0)
        row = lax.broadcasted_iota(jnp.int32, (tq, WINDOW), 0)
        col = lax.broadcasted_iota(jnp.int32, (tq, WINDOW), 1)
        prev_ok = jnp.logical_and(col >= row, i > 0)
        next_ok = jnp.logical_and(col <= row, i < nb - 1)
        mask = jnp.concatenate([jnp.where(prev_ok, 0.0, NEG), jnp.zeros((tq, WINDOW), F32),
                                jnp.where(next_ok, 0.0, NEG), jnp.zeros((tq, kx_ref.shape[1]), F32)], axis=1)
    else:
        k_all = kx_ref[0]
        v_all = vx_ref[0]
        mask = None
    zero = jnp.zeros((), BF16)
    one = jnp.ones((), BF16)
    def operands(h):
        slot = h % 2
        swap = slot != h // ATTN_GROUP
        keep = (lane >= HEAD_DIM) if slot else low
        kh = jnp.where(keep, k_all[:, KV_W:] if swap else k_all[:, :KV_W], zero)
        vh = jnp.where(keep, v_all[:, KV_W:] if swap else v_all[:, :KV_W], one)
        return kh, vh

    def scores(h):
        qp = q_ref[0, :, (h // 2) * LANES:(h // 2 + 1) * LANES]
        s = lax.dot_general(qp, operands(h)[0], (((1,), (1,)), ((), ())), preferred_element_type=F32)
        return s if mask is None else s + mask

    for h in range(ATTN_HEADS):
        s_ref[h] = scores(h)
    for pair in range(ATTN_HEADS // 2):
        outs = []
        sinks = []
        for slot in range(2):
            h = 2 * pair + slot
            s = s_ref[h]
            sk = sink_ref[h]
            mx = jnp.maximum(s.max(axis=-1, keepdims=True), sk)
            outs.append(_dot(jnp.exp(s - mx).astype(BF16), operands(h)[1]))
            sinks.append(jnp.exp(sk - mx))
        num = jnp.where(low, outs[0], outs[1])
        den = pltpu.roll(jnp.where(low, outs[1], outs[0]), HEAD_DIM, 1) + jnp.where(low, sinks[0], sinks[1])
        o_ref[0, :, pair * LANES:(pair + 1) * LANES] = num / den


def _attention(q, k, v, kx, vx, sink8, window):
    b, n, _ = q.shape
    c = kx.shape[1]
    tq = WINDOW
    nb = n // tq
    smem = pl.BlockSpec(memory_space=pltpu.SMEM)
    qspec = pl.BlockSpec((1, tq, ATTN_W), lambda i, j: (i, j, 0))
    xspec = pl.BlockSpec((1, c, 2 * KV_W), lambda i, j: (i, 0, 0))
    if window:
        prev = pl.BlockSpec((1, tq, 2 * KV_W), lambda i, j: (i, jnp.maximum(j - 1, 0), 0))
        cur = pl.BlockSpec((1, tq, 2 * KV_W), lambda i, j: (i, j, 0))
        nxt = pl.BlockSpec((1, tq, 2 * KV_W), lambda i, j: (i, jnp.minimum(j + 1, nb - 1), 0))
        in_specs = [smem, qspec, prev, cur, nxt, prev, cur, nxt, xspec, xspec]
        args = (sink8, q, k, k, k, v, v, v, kx, vx)
    else:
        in_specs = [smem, qspec, xspec, xspec]
        args = (sink8, q, kx, vx)
    return pl.pallas_call(
        functools.partial(_attn_kernel, window=window),
        out_shape=jax.ShapeDtypeStruct((b, n, ATTN_W), F32),
        grid=(b, nb),
        in_specs=in_specs,
        out_specs=pl.BlockSpec((1, tq, ATTN_W), lambda i, j: (i, j, 0)),
        scratch_shapes=[pltpu.VMEM((ATTN_HEADS, tq, (3 * tq if window else 0) + c), F32)],
        compiler_params=_cparams(("arbitrary", "arbitrary")),
        name="window_attention" if window else "context_attention",
    )(*args)


def _lane_blocks_to_rows(x):
    return pltpu.einshape("t(kl)->tkl", x, l=LANES)


def _shift_rows(x, prev_row, next_row):
    n = x.shape[0]
    row = lax.broadcasted_iota(jnp.int32, (n, 1), 0)
    xp = jnp.where(row == 0, prev_row, pltpu.roll(x, 1, 0))
    xn = jnp.where(row == n - 1, next_row, pltpu.roll(x, n - 1, 0))
    return xp, xn


def _rwkv_prep_kernel(r_ref, rp_ref, rn_ref, k_ref, kp_ref, kn_ref, v_ref, vp_ref, vn_ref, lora_ref,
                      cwr_ref, cwk_ref, cwv_ref, w0_ref, w2_ref, a0_ref, a2_ref, vec_ref, sjj_ref, sji_ref, tri_ref,
                      vt_ref, bonus_ref, ab_ref, kr_ref, gm_ref):
    i = pl.program_id(1)
    first = (i > 0).astype(F32)
    last = (i < pl.num_programs(1) - 1).astype(F32)

    def conv(x_ref, p_ref, n_ref, cw_ref):
        x = x_ref[0]
        xp, xn = _shift_rows(x, p_ref[0, SUBLANES - 1:SUBLANES, :] * first, n_ref[0, 0:1, :] * last)
        cw = cw_ref[...]
        return xp * cw[0:1] + x * cw[1:2] + xn * cw[2:3]

    r = conv(r_ref, rp_ref, rn_ref, cwr_ref)
    k = conv(k_ref, kp_ref, kn_ref, cwk_ref)
    v = conv(v_ref, vp_ref, vn_ref, cwv_ref)
    vec = vec_ref[...]
    sjj = sjj_ref[...]
    sji = sji_ref[...]

    def fold(x):
        return (x[:, 0:LANES] + x[:, LANES:2 * LANES]) + (x[:, 2 * LANES:3 * LANES] + x[:, 3 * LANES:4 * LANES])

    kk = k * vec[0:1]
    ss = _split_dot(fold(kk * kk), sjj, 2)
    inv = lax.rsqrt(jnp.maximum(ss, 1e-24))
    kk = kk * jnp.concatenate([inv, inv, inv, inv], axis=1)
    rk = _split_dot(fold(r * k * vec[2:3]), sji, 2)
    vt_ref[0] = _lane_blocks_to_rows(v)
    bonus_ref[0] = v * jnp.concatenate([rk] * SUBLANES, axis=1)
    lora = lora_ref[0]
    zw = jnp.tanh(lora[:, :2 * DECAY_LORA])
    za = lora[:, 2 * DECAY_LORA:]
    zw_hi = zw.astype(BF16)
    zw_lo = (zw - zw_hi.astype(F32)).astype(BF16)
    za_hi = za.astype(BF16)
    za_lo = (za - za_hi.astype(F32)).astype(BF16)

    def lora(x_hi, x_lo, w_ref, d):
        return _dot(x_hi, w_ref[0, d]) + (_dot(x_lo, w_ref[0, d]) + _dot(x_hi, w_ref[1, d]))

    for d in range(2):
        wpre = w0_ref[d:d + 1, :] + lora(zw_hi, zw_lo, w2_ref, d)
        apre = a0_ref[d:d + 1, :] + lora(za_hi, za_lo, a2_ref, d)
        a = _sigmoid(apre)
        log_decay = -np.float32(np.exp(-0.5)) * _sigmoid(wpre)
        tri = tri_ref[d]
        hi = log_decay.astype(BF16)
        rest = log_decay - hi.astype(F32)
        mid = rest.astype(BF16)
        low = (rest - mid.astype(F32)).astype(BF16)
        log_g = _dot(tri, hi) + _dot(tri, mid) + _dot(tri, low)
        g = jnp.exp(log_g)
        g_inv = jnp.exp(-log_g)
        a_t = -kk * jnp.exp(log_g - log_decay)
        b_t = kk * a * g_inv
        k_t = k * (1.0 + (a - 1.0) * vec[1:2]) * g_inv
        r_t = r * g
        ab_ref[d, 0] = _lane_blocks_to_rows(jnp.concatenate([a_t, b_t], axis=1))
        kr_ref[d, 0] = _lane_blocks_to_rows(jnp.concatenate([k_t, r_t], axis=1))
        gm_ref[d, 0] = g


def _rwkv_prep(r, k, v, lora, cwr, cwk, cwv, w0, w2, a0, a2, vec, sjj, sji, tm):
    b, n, _ = r.shape
    nh = n // SUBLANES

    def tile(w):
        return pl.BlockSpec((1, tm, w), lambda i, j: (i, j, 0))

    def halo_prev(w):
        return pl.BlockSpec((1, SUBLANES, w), lambda i, j: (i, jnp.maximum(j * (tm // SUBLANES) - 1, 0), 0))

    def halo_next(w):
        return pl.BlockSpec((1, SUBLANES, w), lambda i, j: (i, jnp.minimum((j + 1) * (tm // SUBLANES), nh - 1), 0))

    def dir_tile(w):
        return pl.BlockSpec((2, 1, tm, w), lambda i, j: (0, i, j, 0))

    t = np.arange(tm)
    same = (t[:, None] // SCAN_T) == (t[None, :] // SCAN_T)
    tri = jnp.asarray(np.stack([same & (t[None, :] <= t[:, None]), same & (t[None, :] >= t[:, None])]), BF16)
    def row_tiles(lead):
        return jax.ShapeDtypeStruct(lead + (n, SUBLANES, LANES), F32)

    def dir_row_tile():
        return pl.BlockSpec((2, 1, tm, SUBLANES, LANES), lambda i, j: (0, i, j, 0, 0))

    out_shape = [row_tiles((b,)), jax.ShapeDtypeStruct((b, n, I_W), F32), row_tiles((2, b)), row_tiles((2, b)),
                 jax.ShapeDtypeStruct((2, b, n, J_W), F32)]
    row_tile = pl.BlockSpec((1, tm, SUBLANES, LANES), lambda i, j: (i, j, 0, 0))
    return pl.pallas_call(
        _rwkv_prep_kernel,
        out_shape=out_shape,
        grid=(b, n // tm),
        in_specs=[tile(J_W), halo_prev(J_W), halo_next(J_W),
                  tile(J_W), halo_prev(J_W), halo_next(J_W),
                  tile(I_W), halo_prev(I_W), halo_next(I_W),
                  tile(4 * DECAY_LORA),
                  _const_spec((CONV_W, J_W)), _const_spec((CONV_W, J_W)), _const_spec((CONV_W, I_W)),
                  _const_spec((2, J_W)), _const_spec((2, 2, 2 * DECAY_LORA, J_W)),
                  _const_spec((2, J_W)), _const_spec((2, 2, 2 * ICLR_LORA, J_W)),
                  _const_spec((3, J_W)), _const_spec((LANES, LANES)), _const_spec((LANES, LANES)),
                  _const_spec((2, tm, tm))],
        out_specs=[row_tile, tile(I_W), dir_row_tile(), dir_row_tile(), dir_tile(J_W)],
        compiler_params=_cparams(("arbitrary", "arbitrary")),
        name="rwkv_prep",
    )(r, r, r, k, k, k, v, v, v, lora, cwr, cwk, cwv, w0, w2, a0, a2, vec, sjj, sji, tri)


N_ROWVEC = 4
N_PAIRS = N_ROWVEC // 2
N_DIR_REFS = N_PAIRS + 2
SCAN_PASSES = 1
DECAY_PASSES = 3
N_SG = 3
SG_PARTS = (((0, 0, 6, 0), (1, 0, 2, 6)),
            ((1, 2, 4, 0), (2, 0, 4, 4)),
            ((2, 4, 2, 0), (3, 0, 6, 2)))
SG_MIXED = 1


def _sg_expand_matrices():
    m = np.zeros((N_SG, 2 * LANES, E_W), np.float32)
    for sg, parts in enumerate(SG_PARTS):
        for half, (_, h0, nh, slot0) in enumerate(parts):
            for jl in range(4):
                for part in range(J_PARTS):
                    for h in range(h0, h0 + nh):
                        for rep in range(I_REP):
                            m[sg, half * LANES + jl * 32 + part * 8 + h,
                              jl * 128 + part * 32 + (slot0 + h - h0) * I_REP + rep] = 1.0
    return m


def _scan_kernel(*refs, t_chunk, n_batch):
    ins = refs[:2 * N_DIR_REFS]
    r2_ref, s0_ref = refs[2 * N_DIR_REFS:2 * N_DIR_REFS + 2]
    y_refs = refs[2 * N_DIR_REFS + 2:2 * N_DIR_REFS + 4]
    sout_ref = refs[2 * N_DIR_REFS + 4]
    e_ref, s_ref, ybuf_ref, yrev_ref = refs[2 * N_DIR_REFS + 5:]
    step_id = pl.program_id(0)
    assert n_batch == 2
    last = t_chunk - 1

    @pl.when(step_id == 0)
    def _():
        s_ref[...] = s0_ref[...]

    def src(x, s):
        return ins[(s // n_batch) * N_DIR_REFS + x]

    def expand(sg, first, second, passes):
        lhs = jnp.concatenate([first, second], axis=2)
        lhs = lhs.reshape(lhs.shape[0] * SUBLANES, 2 * LANES)
        return _split_dot(lhs, r2_ref[sg], passes).reshape(first.shape[0], SUBLANES, E_W)

    for sg, parts in enumerate(SG_PARTS):
        for pair in range(N_PAIRS):
            halves = []
            for s, _, _, _ in parts:
                ref = src(pair, s)
                if sg == SG_MIXED and s >= n_batch:
                    halves.append(jnp.concatenate([ref[0, s % n_batch, t:t + 1] for t in reversed(range(t_chunk))],
                                                  axis=0))
                else:
                    halves.append(ref[0, s % n_batch])
            e_ref[sg, pair] = expand(sg, halves[0], halves[1], SCAN_PASSES)

    lane = lax.broadcasted_iota(jnp.int32, (SUBLANES, LANES), 1)
    q_even = (lane // 32) % 2 == 0
    slot = (lane // I_REP) % HEAD_SLOTS

    def shifted(x, slots):
        return x if slots == 0 else pltpu.roll(x, (slots * I_REP) % LANES, 1)

    def allparts(p):
        return (p + pltpu.roll(p, 64, 1)) + (pltpu.roll(p, 32, 1) + pltpu.roll(p, 96, 1))

    def step(s, sg):
        (sa_src, ha, _, slot_a), (sb_src, hb, _, slot_b) = SG_PARTS[sg]
        ta = s if sa_src < n_batch else last - s
        tb = s if sb_src < n_batch else last - s
        te = s if sg == SG_MIXED else ta
        vp = jnp.where(slot < slot_b, shifted(src(N_PAIRS + 1, sa_src)[sa_src % n_batch, ta], slot_a - ha),
                       shifted(src(N_PAIRS + 1, sb_src)[sb_src % n_batch, tb], slot_b - hb))
        vr = pltpu.roll(vp, 32, 1)
        v = (jnp.where(q_even, vp, vr), jnp.where(q_even, vr, vp))

        def row(x, jj):
            r0 = (x % 2) * 4 + jj // 4
            return e_ref[sg, x // 2, te, r0:r0 + 1, (jj % 4) * LANES:(jj % 4 + 1) * LANES]

        def add(acc, ih, jj, term):
            k = (ih, jj % 4)
            acc[k] = term if k not in acc else acc[k] + term

        def total(acc, ih):
            return allparts((acc[ih, 0] + acc[ih, 1]) + (acc[ih, 2] + acc[ih, 3]))

        acc = {}
        for jj in range(JJ):
            a = row(0, jj)
            for ih in range(I_HI):
                add(acc, ih, jj, s_ref[sg, ih, jj] * a)
        sa = [total(acc, ih) for ih in range(I_HI)]
        acc = {}
        for jj in range(JJ):
            b, k, rr = row(1, jj), row(2, jj), row(3, jj)
            for ih in range(I_HI):
                new = (s_ref[sg, ih, jj] + v[ih] * k) + sa[ih] * b
                s_ref[sg, ih, jj] = new
                add(acc, ih, jj, new * rr)
        y = [total(acc, ih) for ih in range(I_HI)]
        packed = jnp.where(q_even, y[0], y[1])
        ybuf_ref[sg, te] = packed
        if sg == SG_MIXED:
            yrev_ref[last - s] = packed

    def one_token(s, carry):
        for sg in range(N_SG):
            step(s, sg)
        return carry

    lax.fori_loop(0, t_chunk, one_token, 0, unroll=SCAN_UNROLL)

    for s in range(2 * n_batch):
        out = jnp.zeros((t_chunk, SUBLANES, LANES), F32)
        for sg, parts in enumerate(SG_PARTS):
            for s2, h0, nh, slot0 in parts:
                if s2 == s:
                    buf = yrev_ref[...] if (sg == SG_MIXED and s >= n_batch) else ybuf_ref[sg]
                    if h0 != slot0:
                        buf = pltpu.roll(buf, ((h0 - slot0) * I_REP) % LANES, 2)
                    out = jnp.where(jnp.logical_and(slot >= h0, slot < h0 + nh)[None], buf, out)
        y_refs[s // n_batch][s % n_batch] = out

    for sg, parts in enumerate(SG_PARTS):
        halves = []
        toks = []
        for s, _, _, _ in parts:
            tile0, tok = (t_chunk - SUBLANES, SUBLANES - 1) if s < n_batch else (0, 0)
            g8 = src(N_PAIRS, s)[0, s % n_batch, tile0:tile0 + SUBLANES, :]
            halves.append(_lane_blocks_to_rows(jnp.concatenate([g8, g8], axis=1)))
            toks.append(tok)
        res = expand(sg, halves[0], halves[1], DECAY_PASSES)
        first_rows, second_rows = res[toks[0]], res[toks[1]]
        for jj in range(JJ):
            r0 = jj // 4
            blk = slice((jj % 4) * LANES, (jj % 4 + 1) * LANES)
            scale = jnp.where(slot[0:1] < parts[1][3], first_rows[r0:r0 + 1, blk], second_rows[r0:r0 + 1, blk])
            for ih in range(I_HI):
                s_ref[sg, ih, jj] = s_ref[sg, ih, jj] * scale

    @pl.when(step_id == pl.num_programs(0) - 1)
    def _():
        sout_ref[...] = s_ref[...]


def _rwkv_scan(vt, ab, kr, gm, state0, r2):
    b, n = vt.shape[:2]
    t = SCAN_T
    nc = n // t
    in_specs = []
    args = []
    out_specs = []
    for d in range(2):
        def chunk(i, d=d):
            return i if d == 0 else nc - 1 - i

        pair_spec = pl.BlockSpec((1, b, t, SUBLANES, LANES), lambda i, d=d, chunk=chunk: (d, 0, chunk(i), 0, 0))
        gm_spec = pl.BlockSpec((1, b, t, J_W), lambda i, d=d, chunk=chunk: (d, 0, chunk(i), 0))
        tile_spec = pl.BlockSpec((b, t, SUBLANES, LANES), lambda i, chunk=chunk: (0, chunk(i), 0, 0))
        in_specs += [pair_spec, pair_spec, gm_spec, tile_spec]
        args += [ab, kr, gm, vt]
        out_specs.append(tile_spec)
    state_shape = (N_SG, I_HI, JJ, SUBLANES, LANES)
    in_specs += [_const_spec((N_SG, 2 * LANES, E_W)), _const_spec(state_shape)]
    args += [r2, state0]
    out_specs.append(_const_spec(state_shape))
    y_shape = jax.ShapeDtypeStruct((b, n, SUBLANES, LANES), F32)
    return pl.pallas_call(
        functools.partial(_scan_kernel, t_chunk=t, n_batch=b),
        out_shape=[y_shape, y_shape, jax.ShapeDtypeStruct(state_shape, F32)],
        grid=(nc,),
        in_specs=in_specs,
        out_specs=out_specs,
        scratch_shapes=[pltpu.VMEM((N_SG, N_PAIRS, t, SUBLANES, E_W), F32),
                        pltpu.VMEM(state_shape, F32),
                        pltpu.VMEM((N_SG, t, SUBLANES, LANES), F32),
                        pltpu.VMEM((t, SUBLANES, LANES), F32)],
        compiler_params=_cparams(("arbitrary",)),
        name="rwkv_scan",
    )(*args)


def _mix_out_kernel(x_ref, m_ref, g_ref, f_ref, a_ref, yf_ref, yb_ref, bonus_ref, zg_ref,
                    g2_ref, ln_ref, sii_ref, wf_ref, wa_ref, wr_ref, o_ref):
    x = x_ref[0]
    y = pltpu.einshape("tkl->t(kl)", yf_ref[0] + yb_ref[0])
    sii = sii_ref[...]
    nblk = I_W // LANES
    cnt = float(2 * HEAD_DIM)

    def fold(z):
        acc = z[:, 0:LANES]
        for c in range(1, nblk):
            acc = acc + z[:, c * LANES:(c + 1) * LANES]
        return acc

    def spread(z):
        return jnp.concatenate([z] * nblk, axis=1)

    mu = _split_dot(fold(y), sii, 2) * (1.0 / cnt)
    dlt = y - spread(mu)
    var = _split_dot(fold(dlt * dlt), sii, 2) * (1.0 / cnt)
    ln = ln_ref[...]
    yn = dlt * spread(lax.rsqrt(var + GN_EPS)) * ln[0:1] + ln[1:2]
    gate = _dot(_sigmoid(zg_ref[0]).astype(BF16), g2_ref[...])
    rw = ((yn + bonus_ref[0]) * gate).astype(BF16)
    o = _dot(f_ref[0].astype(BF16), wf_ref[...]) + _dot(a_ref[0].astype(BF16), wa_ref[...]) + _dot(rw, wr_ref[...])
    o_ref[0] = x + m_ref[0] * _rms(o, g_ref[...])


def _mix_out(x, gate_row, g3, fo, ao, y_f, y_b, bonus, zg, g2p, ln, sii, wf, wa, wr, tm):
    b, n, d = x.shape

    def tile(w):
        return pl.BlockSpec((1, tm, w), lambda i, j: (i, j, 0))

    return pl.pallas_call(
        _mix_out_kernel,
        out_shape=jax.ShapeDtypeStruct(x.shape, F32),
        grid=(b, n // tm),
        in_specs=[tile(d), pl.BlockSpec((1, 1, d), lambda i, j: (i, 0, 0)), _const_spec((1, d)),
                  tile(FOURIER_W), tile(ATTN_W),
                  pl.BlockSpec((1, tm, SUBLANES, LANES), lambda i, j: (i, j, 0, 0)),
                  pl.BlockSpec((1, tm, SUBLANES, LANES), lambda i, j: (i, j, 0, 0)),
                  tile(I_W), tile(GATE_LORA),
                  _const_spec((GATE_LORA, I_W)), _const_spec((2, I_W)), _const_spec((LANES, LANES)),
                  _const_spec((FOURIER_W, d)), _const_spec((ATTN_W, d)), _const_spec((I_W, d))],
        out_specs=tile(d),
        compiler_params=_cparams(("arbitrary", "arbitrary")),
        name="mix_out",
    )(x, gate_row, g3, fo, ao, y_f, y_b, bonus, zg, g2p, ln, sii, wf, wa, wr)


def _rope_tables(n_tokens):
    rows_n = n_tokens // GRID_W
    row = jnp.repeat(jnp.arange(rows_n), GRID_W).astype(F32)
    colp = jnp.tile(jnp.arange(GRID_W), rows_n).astype(F32)
    n_freq = HEAD_DIM // 4
    inv = ROPE_BASE ** (-jnp.arange(n_freq, dtype=F32) / n_freq)
    ang = jnp.concatenate([row[:, None] * inv, colp[:, None] * inv], axis=-1)
    cos = jnp.tile(jnp.cos(ang), (1, 4))
    sin = jnp.tile(jnp.sin(ang), (1, 4))
    return cos, sin


def _rot_cols(w):
    d, n = w.shape
    w4 = w.reshape(d, n // HEAD_DIM, 2, HEAD_DIM // 2)
    return jnp.stack([-w4[:, :, 1], w4[:, :, 0]], axis=2).reshape(d, n)


def _layer_weights(li, p):
    f = p["ffn1_wo"].shape[1]
    fp = -(-f // (2 * LANES)) * (2 * LANES)
    out = {}
    for name in ("ffn1", "ffn2"):
        wi = p[name + "_wi"][li]
        wo = p[name + "_wo"][li]
        out[name] = (jnp.pad(wi[:, :f], ((0, 0), (0, fp - f))).astype(BF16),
                     jnp.pad(wi[:, f:], ((0, 0), (0, fp - f))).astype(BF16),
                     jnp.pad(wo, ((0, fp - f), (0, 0))).astype(BF16))
    w_in = p["mix_w_in"][li]
    offs = np.cumsum([0, FOURIER_W, ATTN_W, KV_W, KV_W, RWKV_W, RWKV_W, RWKV_W,
                      DECAY_LORA, DECAY_LORA, ICLR_LORA, ICLR_LORA, GATE_LORA])
    part = [w_in[:, offs[i]:offs[i + 1]] for i in range(12)]
    wf, wq, wk, wv, wrr, wrk, wrv = part[:7]
    swap = lambda w: jnp.concatenate([w[:, HEAD_DIM:], w[:, :HEAD_DIM]], axis=1)
    cols = {"f": wf, "q": wq, "qr": _rot_cols(wq),
            "k": jnp.concatenate([wk, swap(wk)], axis=1),
            "kr": jnp.concatenate([_rot_cols(wk), swap(_rot_cols(wk))], axis=1),
            "v": jnp.concatenate([wv, swap(wv)], axis=1),
            "r": _take_cols(wrr, J_IDX), "rk": _take_cols(wrk, J_IDX), "rv": _take_cols(wrv, I_IDX),
            "lora": jnp.concatenate(part[7:11], axis=1), "g": part[11]}
    out["w_all"] = jnp.concatenate([cols[n] for n, _ in MIX_COLS], axis=1).astype(BF16)
    conv = p["rwkv_conv"][li]
    out["cwr"] = _take_cols(conv[:, :RWKV_W], J_IDX)
    out["cwk"] = _take_cols(conv[:, RWKV_W:2 * RWKV_W], J_IDX)
    out["cwv"] = _take_cols(conv[:, 2 * RWKV_W:], I_IDX)
    def per_direction(w2):
        w2 = _take_cols(w2, J_IDX)
        z = jnp.zeros_like(w2[0])
        w2 = jnp.stack([jnp.concatenate([w2[0], z], axis=0), jnp.concatenate([z, w2[1]], axis=0)])
        hi = w2.astype(BF16)
        return jnp.stack([hi, (w2 - hi.astype(F32)).astype(BF16)])

    out["w0"] = _take_cols(p["rwkv_w0"][li], J_IDX)
    out["w2"] = per_direction(p["rwkv_w2"][li])
    out["a0"] = _take_cols(p["rwkv_a0"][li], J_IDX)
    out["a2"] = per_direction(p["rwkv_a2"][li])
    out["vec"] = _take_cols(jnp.stack([p["rwkv_k_k"][li], p["rwkv_k_a"][li], p["rwkv_r_k"][li].reshape(-1)]), J_IDX)
    out["g2"] = _take_cols(p["rwkv_g2"][li], I_IDX).astype(BF16)
    out["ln"] = _take_cols(jnp.stack([p["rwkv_ln_g"][li], p["rwkv_ln_b"][li]]), I_IDX)
    w_out = p["mix_w_out"][li]
    out["wo_f"] = w_out[:FOURIER_W].astype(BF16)
    out["wo_a"] = w_out[FOURIER_W:FOURIER_W + ATTN_W].astype(BF16)
    out["wo_r"] = _take_rows(w_out[FOURIER_W + ATTN_W:], I_IDX_ONCE).astype(BF16)
    sink = p["attn_sink"][li]
    out["sink"] = jnp.concatenate([sink, jnp.zeros((8 - ATTN_HEADS,), F32)])
    return out


def kernel(x, c, ctx, c_ctx, mod_w, mod_b, norm_g, ffn1_wi, ffn1_wo, mix_w_in, mix_w_out, attn_sink,
           rwkv_conv, rwkv_w0, rwkv_w2, rwkv_a0, rwkv_a2, rwkv_g2, rwkv_k_k, rwkv_k_a, rwkv_r_k,
           rwkv_ln_g, rwkv_ln_b, ffn2_wi, ffn2_wo):
    p = dict(ffn1_wi=ffn1_wi, ffn1_wo=ffn1_wo, ffn2_wi=ffn2_wi, ffn2_wo=ffn2_wo, mix_w_in=mix_w_in,
             mix_w_out=mix_w_out, attn_sink=attn_sink, rwkv_conv=rwkv_conv, rwkv_w0=rwkv_w0, rwkv_w2=rwkv_w2,
             rwkv_a0=rwkv_a0, rwkv_a2=rwkv_a2, rwkv_g2=rwkv_g2, rwkv_k_k=rwkv_k_k, rwkv_k_a=rwkv_k_a,
             rwkv_r_k=rwkv_r_k, rwkv_ln_g=rwkv_ln_g, rwkv_ln_b=rwkv_ln_b)
    b, s, d = x.shape
    n_c = ctx.shape[1]
    depth = mod_w.shape[0]
    assert b + 1 <= 8 and s % WINDOW == 0 and n_c % WINDOW == 0 and WINDOW % SCAN_T == 0
    tm_l = 256
    tm_ffn = 512 if s % 512 == 0 else tm_l
    tm_c = min(256, n_c)

    cond8 = jnp.zeros((8, d), F32).at[:b].set(c).at[b].set(c_ctx)
    mod = _ada_mod(cond8, mod_w, mod_b).reshape(depth, 8, N_MOD, d)
    cos_l, sin_l = _rope_tables(s)
    cos_c = jnp.ones((n_c, LANES), F32)
    sin_c = jnp.zeros((n_c, LANES), F32)
    r2 = jnp.asarray(_sg_expand_matrices(), BF16)
    sjj, sji, sii = (jnp.asarray(m, BF16) for m in _seg_matrices())
    zero_state = jnp.zeros((N_SG, I_HI, JJ, SUBLANES, LANES), F32)

    xl, xc = x, ctx
    for li in range(depth):
        need_ctx_out = li < depth - 1
        w = _layer_weights(li, p)
        ml = mod[li, :b]
        mc = jnp.broadcast_to(mod[li, b:b + 1], (b, N_MOD, d))
        g = norm_g[li]
        xl = _ffn_half(xl, ml[:, 0:3], g[0:2], *w["ffn1"], tm_ffn)
        xc = _ffn_half(xc, mc[:, 0:3], g[0:2], *w["ffn1"], tm_c)

        def mixer_in(xx, mm, cos, sin, tm):
            return _mix_in(xx, mm[:, 3:6], g[2:3], w["w_all"], cos, sin, tm)

        fl, ql, kl, vl, rl, rkl, rvl, loral, zgl = mixer_in(xl, ml, cos_l, sin_l, tm_l)
        fc, qc, kc, vc, rc_, rkc, rvc, lorac, zgc = mixer_in(xc, mc, cos_c, sin_c, tm_c)

        attn_l = _attention(ql, kl, vl, kc, vc, w["sink"], True)

        def prep(r_, k_, v_, lora_, tm):
            return _rwkv_prep(r_, k_, v_, lora_, w["cwr"], w["cwk"], w["cwv"], w["w0"], w["w2"],
                              w["a0"], w["a2"], w["vec"], sjj, sji, tm)

        pc = prep(rc_, rkc, rvc, lorac, tm_c)
        plat = prep(rl, rkl, rvl, loral, tm_l)
        ycf, ycb, state_c = _rwkv_scan(pc[0], *pc[2:], zero_state, r2)
        ylf, ylb, _ = _rwkv_scan(plat[0], *plat[2:], state_c, r2)

        def mixer_out(xx, mm, fo, ao, y_f, y_b, bonus, zg, tm):
            return _mix_out(xx, mm[:, 5:6], g[3:4], fo, ao, y_f, y_b, bonus, zg, w["g2"], w["ln"], sii,
                            w["wo_f"], w["wo_a"], w["wo_r"], tm)

        xl = mixer_out(xl, ml, _fourier_mix(fl), attn_l, ylf, ylb, plat[1], zgl, tm_l)
        if need_ctx_out:
            attn_c = _attention(qc, None, None, kc, vc, w["sink"], False)
            xc = mixer_out(xc, mc, _fourier_mix(fc), attn_c, ycf, ycb, pc[1], zgc, tm_c)
            xc = _ffn_half(xc, mc[:, 6:9], g[4:6], *w["ffn2"], tm_c)
        xl = _ffn_half(xl, ml[:, 6:9], g[4:6], *w["ffn2"], tm_ffn)
    return xl
```

```python
import functools

import numpy as np
import jax
import jax.numpy as jnp
from jax import lax
from jax.experimental import pallas as pl
from jax.experimental.pallas import tpu as pltpu

F32 = jnp.float32
BF16 = jnp.bfloat16

HEAD_DIM = 64
GRID_W = 64
FOURIER_W = 256
FOURIER_GROUPS = 4
ATTN_HEADS = 6
ATTN_KV_HEADS = 2
ATTN_GROUP = ATTN_HEADS // ATTN_KV_HEADS
ATTN_W = ATTN_HEADS * HEAD_DIM
KV_W = ATTN_KV_HEADS * HEAD_DIM
RWKV_HEADS = 6
RWKV_W = RWKV_HEADS * HEAD_DIM
WINDOW = 128
ROPE_BASE = 10000.0
DECAY_LORA = 64
ICLR_LORA = 64
GATE_LORA = 128
CONV_W = 3
N_MOD = 9
NORM_EPS = 1e-6
GN_EPS = 64e-5

LANES = 128
SUBLANES = 8
VMEM_LIMIT = 56 * 1024 * 1024

HEAD_SLOTS = 8
J_PARTS = 4
I_REP = 4
JJ = HEAD_DIM // J_PARTS
I_HI = HEAD_DIM // (SUBLANES * I_REP)
J_W = 4 * LANES
E_W = 4 * LANES
I_W = SUBLANES * LANES
SCAN_T = 64
SCAN_UNROLL = 32
FOURIER_N1 = 64
FOURIER_DIRECT_MAX = 256
FOURIER_K1_BLOCK = 4


def _j_layout():
    idx = np.full((J_W,), -1, np.int64)
    for jh in range(4):
        for jl in range(4):
            for part in range(J_PARTS):
                for h in range(RWKV_HEADS):
                    idx[jh * 128 + jl * 32 + part * 8 + h] = h * HEAD_DIM + part * JJ + jh * 4 + jl
    return idx


def _i_layout_compact():
    idx = np.full((I_W // 2,), -1, np.int64)
    for isub in range(8):
        for q in range(2):
            for h in range(RWKV_HEADS):
                for il in range(I_REP):
                    idx[isub * 64 + q * 32 + h * 4 + il] = h * HEAD_DIM + q * 32 + isub * 4 + il
    return idx


J_IDX = _j_layout()
I_IDX_COMPACT = _i_layout_compact()


def _take_cols(w, idx):
    g = jnp.take(w, jnp.asarray(np.maximum(idx, 0)), axis=-1)
    return g * jnp.asarray((idx >= 0).astype(np.float32))


def _take_rows(w, idx):
    g = jnp.take(w, jnp.asarray(np.maximum(idx, 0)), axis=0)
    return g * jnp.asarray((idx >= 0).astype(np.float32))[:, None]


def _seg_matrices():
    hj = np.arange(LANES) % 8
    hi = (np.arange(LANES) // I_REP) % 8
    jj = (hj[:, None] == hj[None, :]).astype(np.float32)
    ji = (hj[:, None] == hi[None, :]).astype(np.float32)
    ii = (hi[:, None] == hi[None, :]).astype(np.float32)
    return jj, ji, ii


def _cparams(sem, vmem=VMEM_LIMIT):
    return pltpu.CompilerParams(dimension_semantics=sem, vmem_limit_bytes=vmem)


def _const_spec(shape):
    nd = len(shape)
    return pl.BlockSpec(shape, lambda *_: (0,) * nd)


def _resident_spec(shape):
    nd = len(shape)
    return pl.BlockSpec(shape, lambda *_: (0,) * nd, pipeline_mode=pl.Buffered(1))


def _sigmoid(x):
    return 1.0 / (1.0 + jnp.exp(-x))


def _rms(x, g):
    return x * lax.rsqrt(jnp.mean(x * x, axis=-1, keepdims=True) + NORM_EPS) * g


def _dot(a, b):
    return jnp.dot(a, b, preferred_element_type=F32)


def _split_dot(x, m, passes):
    acc = None
    rem = x
    for p in range(passes):
        piece = rem.astype(BF16)
        d = _dot(piece, m)
        acc = d if acc is None else acc + d
        if p + 1 < passes:
            rem = rem - piece.astype(F32)
    return acc


def _ada_kernel(c_ref, w_ref, b_ref, o_ref):
    c = c_ref[...]
    s = c * _sigmoid(c)
    o_ref[0] = _dot(s.astype(BF16), w_ref[0].astype(BF16)) + b_ref[0]


def _ada_mod(cond8, mod_w, mod_b):
    nl, d, nw = mod_w.shape
    tn = nw // 8
    return pl.pallas_call(
        _ada_kernel,
        out_shape=jax.ShapeDtypeStruct((nl, 8, nw), F32),
        grid=(nl, nw // tn),
        in_specs=[pl.BlockSpec((8, d), lambda l, j: (0, 0)),
                  pl.BlockSpec((1, d, tn), lambda l, j: (l, 0, j)),
                  pl.BlockSpec((1, 1, tn), lambda l, j: (l, 0, j))],
        out_specs=pl.BlockSpec((1, 8, tn), lambda l, j: (l, 0, j)),
        compiler_params=_cparams(("arbitrary", "arbitrary")),
        name="ada_mod",
    )(cond8, mod_w, mod_b.reshape(nl, 1, nw))


def _ffn_kernel(x_ref, m_ref, g_ref, wi_ref, wo_ref, o_ref):
    x = x_ref[0]
    m = m_ref[0]
    g = g_ref[...]
    h = _rms(x, g[0:1]) * (1.0 + m[1:2]) + m[0:1]
    hb = h.astype(BF16)
    fp = wo_ref.shape[0]
    gate = _dot(hb, wi_ref[:, :fp])
    up = _dot(hb, wi_ref[:, fp:])
    act = (gate * _sigmoid(gate) * up).astype(BF16)
    y = _dot(act, wo_ref[...])
    o_ref[0] = x + 0.5 * m[2:3] * _rms(y, g[1:2])


def _ffn_half(x, m3, g2, wi, wo, tm):
    b, n, d = x.shape
    fp = wo.shape[0]
    return pl.pallas_call(
        _ffn_kernel,
        out_shape=jax.ShapeDtypeStruct(x.shape, F32),
        grid=(b, n // tm),
        in_specs=[pl.BlockSpec((1, tm, d), lambda i, j: (i, j, 0)),
                  pl.BlockSpec((1, 3, d), lambda i, j: (i, 0, 0)),
                  _const_spec((2, d)),
                  _resident_spec((d, 2 * fp)), _resident_spec((fp, d))],
        out_specs=pl.BlockSpec((1, tm, d), lambda i, j: (i, j, 0)),
        compiler_params=_cparams(("arbitrary", "arbitrary")),
        name="ffn_half",
    )(x, m3, g2, wi, wo)


MIX_COLS = (("f", FOURIER_W), ("q", ATTN_W), ("qr", ATTN_W), ("k", KV_W), ("kr", KV_W),
            ("v", KV_W), ("r", J_W), ("rk", J_W), ("rv", I_W // 2), ("lora", 4 * DECAY_LORA), ("g", GATE_LORA))
MIX_OFF = {}
_o = 0
for _n, _w in MIX_COLS:
    MIX_OFF[_n] = (_o, _o + _w)
    _o += _w
MIX_TOTAL = _o


def _mix_in_kernel(x_ref, m_ref, g_ref, w_ref, cos_ref, sin_ref,
                   f_ref, q_ref, k_ref, v_ref, r_ref, rk_ref, rv_ref, lora_ref, zg_ref):
    x = x_ref[0]
    m = m_ref[0]
    h = _rms(x, g_ref[...]) * (1.0 + m[1:2]) + m[0:1]
    z = _dot(h.astype(BF16), w_ref[...])

    def col(name):
        lo, hi = MIX_OFF[name]
        return z[:, lo:hi]

    cos = cos_ref[...]
    sin = sin_ref[...]
    cos3 = jnp.concatenate([cos, cos, cos], axis=1)
    sin3 = jnp.concatenate([sin, sin, sin], axis=1)

    def with_swapped(x):
        return jnp.concatenate([x, pltpu.roll(x, HEAD_DIM, 1)], axis=1).astype(BF16)

    f_ref[0] = col("f")
    q_ref[0] = ((col("q") * cos3 + col("qr") * sin3) * (HEAD_DIM ** -0.5)).astype(BF16)
    k_ref[0] = with_swapped(col("k") * cos + col("kr") * sin)
    v_ref[0] = with_swapped(col("v"))
    r_ref[0] = col("r")
    rk_ref[0] = col("rk")
    rv_ref[0] = col("rv")
    lora_ref[0] = col("lora")
    zg_ref[0] = col("g")


def _mix_in(x, m3, g1, w_all, cos, sin, tm):
    b, n, d = x.shape
    widths = (FOURIER_W, ATTN_W, 2 * KV_W, 2 * KV_W, J_W, J_W, I_W // 2, 4 * DECAY_LORA, GATE_LORA)
    dtypes = (F32, BF16, BF16, BF16, F32, F32, F32, F32, F32)
    return pl.pallas_call(
        _mix_in_kernel,
        out_shape=[jax.ShapeDtypeStruct((b, n, w), dt) for w, dt in zip(widths, dtypes)],
        grid=(b, n // tm),
        in_specs=[pl.BlockSpec((1, tm, d), lambda i, j: (i, j, 0)),
                  pl.BlockSpec((1, 3, d), lambda i, j: (i, 0, 0)),
                  _const_spec((1, d)),
                  _const_spec((d, MIX_TOTAL)),
                  pl.BlockSpec((tm, LANES), lambda i, j: (j, 0)),
                  pl.BlockSpec((tm, LANES), lambda i, j: (j, 0))],
        out_specs=[pl.BlockSpec((1, tm, w), lambda i, j: (i, j, 0)) for w in widths],
        compiler_params=_cparams(("arbitrary", "arbitrary")),
        name="mix_in",
    )(x, m3, g1, w_all, cos, sin)


def _fourier_stage1_kernel(l_ref, x_ref, o_ref):
    o_ref[0] = _dot(l_ref[...], x_ref[0].astype(BF16))


def _fourier_stage2_kernel(a_ref, mc_ref, ms_ref, cc_ref, sc_ref, o_ref, *, scale):
    w = cc_ref.shape[0]
    for j in range(a_ref.shape[2]):
        ar = a_ref[0, 0, j].astype(BF16)
        ai = a_ref[0, 1, j].astype(BF16)
        mc = mc_ref[j]
        ms = ms_ref[j]
        gr = _dot(mc, ar) + _dot(ms, ai)
        gi = _dot(mc, ai) - _dot(ms, ar)
        o_ref[0, :, j * w:(j + 1) * w] = (_dot(gr.astype(BF16), cc_ref[...]) + _dot(gi.astype(BF16), sc_ref[...])) * scale


def _fourier_small_kernel(z_ref, cn_ref, sn_ref, cc_ref, sc_ref, o_ref, *, scale):
    z = z_ref[0].astype(BF16)
    t1 = _dot(z, cc_ref[...]).astype(BF16)
    t2 = _dot(z, sc_ref[...]).astype(BF16)
    o_ref[0] = (_dot(cn_ref[...], t1) - _dot(sn_ref[...], t2)) * scale


def _channel_dft():
    gw = FOURIER_W // FOURIER_GROUPS
    c = np.arange(FOURIER_W)
    same = (c[:, None] // gw) == (c[None, :] // gw)
    ang = 2.0 * np.pi * ((c[:, None] % gw) * (c[None, :] % gw) % gw) / gw
    return (np.cos(ang) * same).astype(np.float32), (np.sin(ang) * same).astype(np.float32)


def _fourier_mix(z):
    b, n, w = z.shape
    gw = w // FOURIER_GROUPS
    scale = float(1.0 / np.sqrt(n * gw))
    cc, sc = _channel_dft()
    cc = jnp.asarray(cc, BF16)
    sc = jnp.asarray(sc, BF16)
    if n <= FOURIER_DIRECT_MAX:
        p = np.arange(n)
        ang = 2.0 * np.pi * ((p[:, None] * p[None, :]) % n) / n
        return pl.pallas_call(
            functools.partial(_fourier_small_kernel, scale=scale),
            out_shape=jax.ShapeDtypeStruct((b, n, w), F32),
            grid=(b,),
            in_specs=[pl.BlockSpec((1, n, w), lambda i: (i, 0, 0)),
                      _const_spec((n, n)), _const_spec((n, n)), _const_spec((w, w)), _const_spec((w, w))],
            out_specs=pl.BlockSpec((1, n, w), lambda i: (i, 0, 0)),
            compiler_params=_cparams(("arbitrary",)),
            name="fourier_small",
        )(z, jnp.asarray(np.cos(ang), BF16), jnp.asarray(np.sin(ang), BF16), cc, sc)
    n1 = FOURIER_N1
    n2 = n // n1
    k1 = np.arange(n1)
    ang1 = 2.0 * np.pi * ((k1[:, None] * k1[None, :]) % n1) / n1
    lhs1 = jnp.asarray(np.concatenate([np.cos(ang1), -np.sin(ang1)], axis=0), BF16)
    tc = min(n2 * w, 4096)
    a = pl.pallas_call(
        _fourier_stage1_kernel,
        out_shape=jax.ShapeDtypeStruct((b, 2 * n1, n2 * w), F32),
        grid=(b, (n2 * w) // tc),
        in_specs=[_const_spec((2 * n1, n1)),
                  pl.BlockSpec((1, n1, tc), lambda i, j: (i, 0, j))],
        out_specs=pl.BlockSpec((1, 2 * n1, tc), lambda i, j: (i, 0, j)),
        compiler_params=_cparams(("arbitrary", "arbitrary")),
        name="fourier_stage1",
    )(lhs1, z.reshape(b, n1, n2 * w))
    k2 = np.arange(n2)
    freq = (k1[:, None, None] + n1 * k2[None, :, None]) * k2[None, None, :]
    ang2 = 2.0 * np.pi * (freq % n) / n
    out = pl.pallas_call(
        functools.partial(_fourier_stage2_kernel, scale=scale),
        out_shape=jax.ShapeDtypeStruct((b, n2, n1 * w), F32),
        grid=(b, n1 // FOURIER_K1_BLOCK),
        in_specs=[pl.BlockSpec((1, 2, FOURIER_K1_BLOCK, n2, w), lambda i, j: (i, 0, j, 0, 0)),
                  pl.BlockSpec((FOURIER_K1_BLOCK, n2, n2), lambda i, j: (j, 0, 0)),
                  pl.BlockSpec((FOURIER_K1_BLOCK, n2, n2), lambda i, j: (j, 0, 0)),
                  _const_spec((w, w)), _const_spec((w, w))],
        out_specs=pl.BlockSpec((1, n2, FOURIER_K1_BLOCK * w), lambda i, j: (i, 0, j)),
        compiler_params=_cparams(("arbitrary", "arbitrary")),
        name="fourier_stage2",
    )(a.reshape(b, 2, n1, n2, w), jnp.asarray(np.cos(ang2), BF16), jnp.asarray(np.sin(ang2), BF16), cc, sc)
    return out.reshape(b, n, w)


NEG = -1e30


def _attn_kernel(*refs, window):
    if window:
        (sink_ref, q_ref, kp_ref, kc_ref, kn_ref, vp_ref, vc_ref, vn_ref, kx_ref, vx_ref, o_ref, s_ref) = refs
    else:
        (sink_ref, q_ref, kx_ref, vx_ref, o_ref, s_ref) = refs
    i = pl.program_id(1)
    nb = pl.num_programs(1)
    tq = q_ref.shape[1]
    lane = lax.broadcasted_iota(jnp.int32, (1, KV_W), 1)
    low = lane < HEAD_DIM
    if window:
        k_all = jnp.concatenate([kp_ref[0], kc_ref[0], kn_ref[0], kx_ref[0]], axis=0)
        v_all = jnp.concatenate([vp_ref[0], vc_ref[0], vn_ref[0], vx_ref[0]], axis=0)
        row = lax.broadcasted_iota(jnp.int32, (tq, WINDOW), 0)
        col = lax.broadcasted_iota(jnp.int32, (tq, WINDOW), 1)
        prev_ok = jnp.logical_and(col >= row, i > 0)
        next_ok = jnp.logical_and(col <= row, i < nb - 1)
        mask = jnp.concatenate([jnp.where(prev_ok, 0.0, NEG), jnp.zeros((tq, WINDOW), F32),
                                jnp.where(next_ok, 0.0, NEG), jnp.zeros((tq, kx_ref.shape[1]), F32)], axis=1)
    else:
        k_all = kx_ref[0]
        v_all = vx_ref[0]
        mask = None
    zero = jnp.zeros((), BF16)
    one = jnp.ones((), BF16)
    def operands(h):
        slot = h % 2
        swap = slot != h // ATTN_GROUP
        keep = (lane >= HEAD_DIM) if slot else low
        kh = jnp.where(keep, k_all[:, KV_W:] if swap else k_all[:, :KV_W], zero)
        vh = jnp.where(keep, v_all[:, KV_W:] if swap else v_all[:, :KV_W], one)
        return kh, vh

    def scores(h):
        qp = q_ref[0, :, (h // 2) * LANES:(h // 2 + 1) * LANES]
        s = lax.dot_general(qp, operands(h)[0], (((1,), (1,)), ((), ())), preferred_element_type=F32)
        return s if mask is None else s + mask

    for h in range(ATTN_HEADS):
        s_ref[h] = scores(h)
    for pair in range(ATTN_HEADS // 2):
        outs = []
        sinks = []
        for slot in range(2):
            h = 2 * pair + slot
            s = s_ref[h]
            sk = sink_ref[h]
            mx = jnp.maximum(s.max(axis=-1, keepdims=True), sk)
            outs.append(_dot(jnp.exp(s - mx).astype(BF16), operands(h)[1]))
            sinks.append(jnp.exp(sk - mx))
        num = jnp.where(low, outs[0], outs[1])
        den = pltpu.roll(jnp.where(low, outs[1], outs[0]), HEAD_DIM, 1) + jnp.where(low, sinks[0], sinks[1])
        o_ref[0, :, pair * LANES:(pair + 1) * LANES] = num / den


def _attention(q, k, v, kx, vx, sink8, window):
    b, n, _ = q.shape
    c = kx.shape[1]
    tq = WINDOW
    nb = n // tq
    smem = pl.BlockSpec(memory_space=pltpu.SMEM)
    qspec = pl.BlockSpec((1, tq, ATTN_W), lambda i, j: (i, j, 0))
    xspec = pl.BlockSpec((1, c, 2 * KV_W), lambda i, j: (i, 0, 0))
    if window:
        prev = pl.BlockSpec((1, tq, 2 * KV_W), lambda i, j: (i, jnp.maximum(j - 1, 0), 0))
        cur = pl.BlockSpec((1, tq, 2 * KV_W), lambda i, j: (i, j, 0))
        nxt = pl.BlockSpec((1, tq, 2 * KV_W), lambda i, j: (i, jnp.minimum(j + 1, nb - 1), 0))
        in_specs = [smem, qspec, prev, cur, nxt, prev, cur, nxt, xspec, xspec]
        args = (sink8, q, k, k, k, v, v, v, kx, vx)
    else:
        in_specs = [smem, qspec, xspec, xspec]
        args = (sink8, q, kx, vx)
    return pl.pallas_call(
        functools.partial(_attn_kernel, window=window),
        out_shape=jax.ShapeDtypeStruct((b, n, ATTN_W), F32),
        grid=(b, nb),
        in_specs=in_specs,
        out_specs=pl.BlockSpec((1, tq, ATTN_W), lambda i, j: (i, j, 0)),
        scratch_shapes=[pltpu.VMEM((ATTN_HEADS, tq, (3 * tq if window else 0) + c), F32)],
        compiler_params=_cparams(("arbitrary", "arbitrary")),
        name="window_attention" if window else "context_attention",
    )(*args)


def _lane_blocks_to_rows(x):
    return pltpu.einshape("t(kl)->tkl", x, l=LANES)


def _shift_rows(x, prev_row, next_row):
    n = x.shape[0]
    row = lax.broadcasted_iota(jnp.int32, (n, 1), 0)
    xp = jnp.where(row == 0, prev_row, pltpu.roll(x, 1, 0))
    xn = jnp.where(row == n - 1, next_row, pltpu.roll(x, n - 1, 0))
    return xp, xn


def _rwkv_prep_kernel(r_ref, rp_ref, rn_ref, k_ref, kp_ref, kn_ref, v_ref, vp_ref, vn_ref, lora_ref,
                      cwr_ref, cwk_ref, cwv_ref, w0_ref, w2_ref, a0_ref, a2_ref, vec_ref, sjj_ref, sji_ref, tri_ref,
                      vt_ref, bonus_ref, ab_ref, kr_ref, gm_ref):
    i = pl.program_id(1)
    first = (i > 0).astype(F32)
    last = (i < pl.num_programs(1) - 1).astype(F32)

    def conv(x_ref, p_ref, n_ref, cw_ref):
        x = x_ref[0]
        xp, xn = _shift_rows(x, p_ref[0, SUBLANES - 1:SUBLANES, :] * first, n_ref[0, 0:1, :] * last)
        cw = cw_ref[...]
        return xp * cw[0:1] + x * cw[1:2] + xn * cw[2:3]

    r = conv(r_ref, rp_ref, rn_ref, cwr_ref)
    k = conv(k_ref, kp_ref, kn_ref, cwk_ref)
    vh = conv(v_ref, vp_ref, vn_ref, cwv_ref)
    low = lax.broadcasted_iota(jnp.int32, (1, LANES), 1) < LANES // 2
    blocks = []
    for c in range(vh.shape[1] // LANES):
        xc = vh[:, c * LANES:(c + 1) * LANES]
        xr = pltpu.roll(xc, LANES // 2, 1)
        blocks += [jnp.where(low, xc, xr), jnp.where(low, xr, xc)]
    v = jnp.concatenate(blocks, axis=1)
    vec = vec_ref[...]
    sjj = sjj_ref[...]
    sji = sji_ref[...]

    def fold(x):
        return (x[:, 0:LANES] + x[:, LANES:2 * LANES]) + (x[:, 2 * LANES:3 * LANES] + x[:, 3 * LANES:4 * LANES])

    kk = k * vec[0:1]
    ss = _split_dot(fold(kk * kk), sjj, 2)
    inv = lax.rsqrt(jnp.maximum(ss, 1e-24))
    kk = kk * jnp.concatenate([inv, inv, inv, inv], axis=1)
    rk = _split_dot(fold(r * k * vec[2:3]), sji, 2)
    vt_ref[0] = _lane_blocks_to_rows(v)
    bonus_ref[0] = vh * jnp.concatenate([rk] * (vh.shape[1] // LANES), axis=1)
    lora = lora_ref[0]
    zw = jnp.tanh(lora[:, :2 * DECAY_LORA])
    za = lora[:, 2 * DECAY_LORA:]
    zw_hi = zw.astype(BF16)
    zw_lo = (zw - zw_hi.astype(F32)).astype(BF16)
    za_hi = za.astype(BF16)
    za_lo = (za - za_hi.astype(F32)).astype(BF16)

    def lora(x_hi, x_lo, w_ref, d):
        return _dot(x_hi, w_ref[0, d]) + (_dot(x_lo, w_ref[0, d]) + _dot(x_hi, w_ref[1, d]))

    for d in range(2):
        wpre = w0_ref[d:d + 1, :] + lora(zw_hi, zw_lo, w2_ref, d)
        apre = a0_ref[d:d + 1, :] + lora(za_hi, za_lo, a2_ref, d)
        a = _sigmoid(apre)
        log_decay = -np.float32(np.exp(-0.5)) * _sigmoid(wpre)
        tri = tri_ref[d]
        hi = log_decay.astype(BF16)
        rest = log_decay - hi.astype(F32)
        mid = rest.astype(BF16)
        low = (rest - mid.astype(F32)).astype(BF16)
        log_g = _dot(tri, hi) + _dot(tri, mid) + _dot(tri, low)
        g = jnp.exp(log_g)
        g_inv = jnp.exp(-log_g)
        a_t = -kk * jnp.exp(log_g - log_decay)
        b_t = kk * a * g_inv
        k_t = k * (1.0 + (a - 1.0) * vec[1:2]) * g_inv
        r_t = r * g
        ab_ref[d, 0] = _lane_blocks_to_rows(jnp.concatenate([a_t, b_t], axis=1))
        kr_ref[d, 0] = _lane_blocks_to_rows(jnp.concatenate([k_t, r_t], axis=1))
        gm_ref[d, 0] = g


def _rwkv_prep(r, k, v, lora, cwr, cwk, cwv, w0, w2, a0, a2, vec, sjj, sji, tm):
    b, n, _ = r.shape
    nh = n // SUBLANES

    def tile(w):
        return pl.BlockSpec((1, tm, w), lambda i, j: (i, j, 0))

    def halo_prev(w):
        return pl.BlockSpec((1, SUBLANES, w), lambda i, j: (i, jnp.maximum(j * (tm // SUBLANES) - 1, 0), 0))

    def halo_next(w):
        return pl.BlockSpec((1, SUBLANES, w), lambda i, j: (i, jnp.minimum((j + 1) * (tm // SUBLANES), nh - 1), 0))

    def dir_tile(w):
        return pl.BlockSpec((2, 1, tm, w), lambda i, j: (0, i, j, 0))

    t = np.arange(tm)
    same = (t[:, None] // SCAN_T) == (t[None, :] // SCAN_T)
    tri = jnp.asarray(np.stack([same & (t[None, :] <= t[:, None]), same & (t[None, :] >= t[:, None])]), BF16)
    def row_tiles(lead):
        return jax.ShapeDtypeStruct(lead + (n, SUBLANES, LANES), F32)

    def dir_row_tile():
        return pl.BlockSpec((2, 1, tm, SUBLANES, LANES), lambda i, j: (0, i, j, 0, 0))

    out_shape = [row_tiles((b,)), jax.ShapeDtypeStruct((b, n, I_W // 2), F32), row_tiles((2, b)), row_tiles((2, b)),
                 jax.ShapeDtypeStruct((2, b, n, J_W), F32)]
    row_tile = pl.BlockSpec((1, tm, SUBLANES, LANES), lambda i, j: (i, j, 0, 0))
    return pl.pallas_call(
        _rwkv_prep_kernel,
        out_shape=out_shape,
        grid=(b, n // tm),
        in_specs=[tile(J_W), halo_prev(J_W), halo_next(J_W),
                  tile(J_W), halo_prev(J_W), halo_next(J_W),
                  tile(I_W // 2), halo_prev(I_W // 2), halo_next(I_W // 2),
                  tile(4 * DECAY_LORA),
                  _const_spec((CONV_W, J_W)), _const_spec((CONV_W, J_W)), _const_spec((CONV_W, I_W // 2)),
                  _const_spec((2, J_W)), _const_spec((2, 2, 2 * DECAY_LORA, J_W)),
                  _const_spec((2, J_W)), _const_spec((2, 2, 2 * ICLR_LORA, J_W)),
                  _const_spec((3, J_W)), _const_spec((LANES, LANES)), _const_spec((LANES, LANES)),
                  _const_spec((2, tm, tm))],
        out_specs=[row_tile, tile(I_W // 2), dir_row_tile(), dir_row_tile(), dir_tile(J_W)],
        compiler_params=_cparams(("arbitrary", "arbitrary")),
        name="rwkv_prep",
    )(r, r, r, k, k, k, v, v, v, lora, cwr, cwk, cwv, w0, w2, a0, a2, vec, sjj, sji, tri)


N_ROWVEC = 4
N_PAIRS = N_ROWVEC // 2
N_DIR_REFS = N_PAIRS + 2
SCAN_PASSES = 1
DECAY_PASSES = 3
N_SG = 3
SG_PARTS = (((0, 0, 6, 0), (1, 0, 2, 6)),
            ((1, 2, 4, 0), (2, 0, 4, 4)),
            ((2, 4, 2, 0), (3, 0, 6, 2)))
SG_MIXED = 1


def _sg_expand_matrices():
    m = np.zeros((N_SG, 2 * LANES, E_W), np.float32)
    for sg, parts in enumerate(SG_PARTS):
        for half, (_, h0, nh, slot0) in enumerate(parts):
            for jl in range(4):
                for part in range(J_PARTS):
                    for h in range(h0, h0 + nh):
                        for rep in range(I_REP):
                            m[sg, half * LANES + jl * 32 + part * 8 + h,
                              jl * 128 + part * 32 + (slot0 + h - h0) * I_REP + rep] = 1.0
    return m


def _scan_kernel(*refs, t_chunk, n_batch):
    ins = refs[:2 * N_DIR_REFS]
    r2_ref, s0_ref = refs[2 * N_DIR_REFS:2 * N_DIR_REFS + 2]
    y_refs = refs[2 * N_DIR_REFS + 2:2 * N_DIR_REFS + 4]
    sout_ref = refs[2 * N_DIR_REFS + 4]
    e_ref, s_ref, ybuf_ref, yrev_ref = refs[2 * N_DIR_REFS + 5:]
    step_id = pl.program_id(0)
    assert n_batch == 2
    last = t_chunk - 1

    @pl.when(step_id == 0)
    def _():
        s_ref[...] = s0_ref[...]

    def src(x, s):
        return ins[(s // n_batch) * N_DIR_REFS + x]

    def expand(sg, first, second, passes):
        lhs = jnp.concatenate([first, second], axis=2)
        lhs = lhs.reshape(lhs.shape[0] * SUBLANES, 2 * LANES)
        return _split_dot(lhs, r2_ref[sg], passes).reshape(first.shape[0], SUBLANES, E_W)

    for sg, parts in enumerate(SG_PARTS):
        for pair in range(N_PAIRS):
            halves = []
            for s, _, _, _ in parts:
                ref = src(pair, s)
                if sg == SG_MIXED and s >= n_batch:
                    halves.append(jnp.concatenate([ref[0, s % n_batch, t:t + 1] for t in reversed(range(t_chunk))],
                                                  axis=0))
                else:
                    halves.append(ref[0, s % n_batch])
            e_ref[sg, pair] = expand(sg, halves[0], halves[1], SCAN_PASSES)

    lane = lax.broadcasted_iota(jnp.int32, (SUBLANES, LANES), 1)
    q_even = (lane // 32) % 2 == 0
    slot = (lane // I_REP) % HEAD_SLOTS

    def shifted(x, slots):
        return x if slots == 0 else pltpu.roll(x, (slots * I_REP) % LANES, 1)

    def allparts(p):
        return (p + pltpu.roll(p, 64, 1)) + (pltpu.roll(p, 32, 1) + pltpu.roll(p, 96, 1))

    def step(s, sg):
        (sa_src, ha, _, slot_a), (sb_src, hb, _, slot_b) = SG_PARTS[sg]
        ta = s if sa_src < n_batch else last - s
        tb = s if sb_src < n_batch else last - s
        te = s if sg == SG_MIXED else ta
        vp = jnp.where(slot < slot_b, shifted(src(N_PAIRS + 1, sa_src)[sa_src % n_batch, ta], slot_a - ha),
                       shifted(src(N_PAIRS + 1, sb_src)[sb_src % n_batch, tb], slot_b - hb))
        vr = pltpu.roll(vp, 32, 1)
        v = (jnp.where(q_even, vp, vr), jnp.where(q_even, vr, vp))

        def row(x, jj):
            r0 = (x % 2) * 4 + jj // 4
            return e_ref[sg, x // 2, te, r0:r0 + 1, (jj % 4) * LANES:(jj % 4 + 1) * LANES]

        def add(acc, ih, jj, term):
            k = (ih, jj % 4)
            acc[k] = term if k not in acc else acc[k] + term

        def total(acc, ih):
            return allparts((acc[ih, 0] + acc[ih, 1]) + (acc[ih, 2] + acc[ih, 3]))

        acc = {}
        for jj in range(JJ):
            a = row(0, jj)
            for ih in range(I_HI):
                add(acc, ih, jj, s_ref[sg, ih, jj] * a)
        sa = [total(acc, ih) for ih in range(I_HI)]
        acc = {}
        for jj in range(JJ):
            b, k, rr = row(1, jj), row(2, jj), row(3, jj)
            for ih in range(I_HI):
                new = (s_ref[sg, ih, jj] + v[ih] * k) + sa[ih] * b
                s_ref[sg, ih, jj] = new
                add(acc, ih, jj, new * rr)
        y = [total(acc, ih) for ih in range(I_HI)]
        packed = jnp.where(q_even, y[0], y[1])
        ybuf_ref[sg, te] = packed
        if sg == SG_MIXED:
            yrev_ref[last - s] = packed

    def one_token(s, carry):
        for sg in range(N_SG):
            step(s, sg)
        return carry

    lax.fori_loop(0, t_chunk, one_token, 0, unroll=SCAN_UNROLL)

    for s in range(2 * n_batch):
        out = jnp.zeros((t_chunk, SUBLANES, LANES), F32)
        for sg, parts in enumerate(SG_PARTS):
            for s2, h0, nh, slot0 in parts:
                if s2 == s:
                    buf = yrev_ref[...] if (sg == SG_MIXED and s >= n_batch) else ybuf_ref[sg]
                    if h0 != slot0:
                        buf = pltpu.roll(buf, ((h0 - slot0) * I_REP) % LANES, 2)
                    out = jnp.where(jnp.logical_and(slot >= h0, slot < h0 + nh)[None], buf, out)
        y_refs[s // n_batch][s % n_batch] = out

    for sg, parts in enumerate(SG_PARTS):
        halves = []
        toks = []
        for s, _, _, _ in parts:
            tile0, tok = (t_chunk - SUBLANES, SUBLANES - 1) if s < n_batch else (0, 0)
            g8 = src(N_PAIRS, s)[0, s % n_batch, tile0:tile0 + SUBLANES, :]
            halves.append(_lane_blocks_to_rows(jnp.concatenate([g8, g8], axis=1)))
            toks.append(tok)
        res = expand(sg, halves[0], halves[1], DECAY_PASSES)
        first_rows, second_rows = res[toks[0]], res[toks[1]]
        for jj in range(JJ):
            r0 = jj // 4
            blk = slice((jj % 4) * LANES, (jj % 4 + 1) * LANES)
            scale = jnp.where(slot[0:1] < parts[1][3], first_rows[r0:r0 + 1, blk], second_rows[r0:r0 + 1, blk])
            for ih in range(I_HI):
                s_ref[sg, ih, jj] = s_ref[sg, ih, jj] * scale

    @pl.when(step_id == pl.num_programs(0) - 1)
    def _():
        sout_ref[...] = s_ref[...]


def _rwkv_scan(vt, ab, kr, gm, state0, r2):
    b, n = vt.shape[:2]
    t = SCAN_T
    nc = n // t
    in_specs = []
    args = []
    out_specs = []
    for d in range(2):
        def chunk(i, d=d):
            return i if d == 0 else nc - 1 - i

        pair_spec = pl.BlockSpec((1, b, t, SUBLANES, LANES), lambda i, d=d, chunk=chunk: (d, 0, chunk(i), 0, 0))
        gm_spec = pl.BlockSpec((1, b, t, J_W), lambda i, d=d, chunk=chunk: (d, 0, chunk(i), 0))
        tile_spec = pl.BlockSpec((b, t, SUBLANES, LANES), lambda i, chunk=chunk: (0, chunk(i), 0, 0))
        in_specs += [pair_spec, pair_spec, gm_spec, tile_spec]
        args += [ab, kr, gm, vt]
        out_specs.append(tile_spec)
    state_shape = (N_SG, I_HI, JJ, SUBLANES, LANES)
    in_specs += [_const_spec((N_SG, 2 * LANES, E_W)), _const_spec(state_shape)]
    args += [r2, state0]
    out_specs.append(_const_spec(state_shape))
    y_shape = jax.ShapeDtypeStruct((b, n, SUBLANES, LANES), F32)
    return pl.pallas_call(
        functools.partial(_scan_kernel, t_chunk=t, n_batch=b),
        out_shape=[y_shape, y_shape, jax.ShapeDtypeStruct(state_shape, F32)],
        grid=(nc,),
        in_specs=in_specs,
        out_specs=out_specs,
        scratch_shapes=[pltpu.VMEM((N_SG, N_PAIRS, t, SUBLANES, E_W), F32),
                        pltpu.VMEM(state_shape, F32),
                        pltpu.VMEM((N_SG, t, SUBLANES, LANES), F32),
                        pltpu.VMEM((t, SUBLANES, LANES), F32)],
        compiler_params=_cparams(("arbitrary",)),
        name="rwkv_scan",
    )(*args)


def _mix_out_kernel(x_ref, m_ref, g_ref, f_ref, a_ref, yf_ref, yb_ref, bonus_ref, zg_ref,
                    g2_ref, ln_ref, sii_ref, wf_ref, wa_ref, wr_ref, o_ref):
    x = x_ref[0]
    yy = pltpu.einshape("tkl->t(kl)", yf_ref[0] + yb_ref[0])
    low = lax.broadcasted_iota(jnp.int32, (1, LANES), 1) < LANES // 2
    y = jnp.concatenate([jnp.where(low, yy[:, 2 * c * LANES:(2 * c + 1) * LANES],
                                   pltpu.roll(yy[:, (2 * c + 1) * LANES:(2 * c + 2) * LANES], LANES // 2, 1))
                         for c in range(I_W // (2 * LANES))], axis=1)
    sii = sii_ref[...]
    nblk = I_W // (2 * LANES)
    cnt = float(HEAD_DIM)

    def fold(z):
        acc = z[:, 0:LANES]
        for c in range(1, nblk):
            acc = acc + z[:, c * LANES:(c + 1) * LANES]
        return acc

    def spread(z):
        return jnp.concatenate([z] * nblk, axis=1)

    mu = _split_dot(fold(y), sii, 2) * (1.0 / cnt)
    dlt = y - spread(mu)
    var = _split_dot(fold(dlt * dlt), sii, 2) * (1.0 / cnt)
    ln = ln_ref[...]
    yn = dlt * spread(lax.rsqrt(var + GN_EPS)) * ln[0:1] + ln[1:2]
    gate = _dot(_sigmoid(zg_ref[0]).astype(BF16), g2_ref[...])
    rw = ((yn + bonus_ref[0]) * gate).astype(BF16)
    o = _dot(f_ref[0].astype(BF16), wf_ref[...]) + _dot(a_ref[0].astype(BF16), wa_ref[...]) + _dot(rw, wr_ref[...])
    o_ref[0] = x + m_ref[0] * _rms(o, g_ref[...])


def _mix_out(x, gate_row, g3, fo, ao, y_f, y_b, bonus, zg, g2p, ln, sii, wf, wa, wr, tm):
    b, n, d = x.shape

    def tile(w):
        return pl.BlockSpec((1, tm, w), lambda i, j: (i, j, 0))

    return pl.pallas_call(
        _mix_out_kernel,
        out_shape=jax.ShapeDtypeStruct(x.shape, F32),
        grid=(b, n // tm),
        in_specs=[tile(d), pl.BlockSpec((1, 1, d), lambda i, j: (i, 0, 0)), _const_spec((1, d)),
                  tile(FOURIER_W), tile(ATTN_W),
                  pl.BlockSpec((1, tm, SUBLANES, LANES), lambda i, j: (i, j, 0, 0)),
                  pl.BlockSpec((1, tm, SUBLANES, LANES), lambda i, j: (i, j, 0, 0)),
                  tile(I_W // 2), tile(GATE_LORA),
                  _const_spec((GATE_LORA, I_W // 2)), _const_spec((2, I_W // 2)), _const_spec((LANES, LANES)),
                  _const_spec((FOURIER_W, d)), _const_spec((ATTN_W, d)), _const_spec((I_W // 2, d))],
        out_specs=tile(d),
        compiler_params=_cparams(("arbitrary", "arbitrary")),
        name="mix_out",
    )(x, gate_row, g3, fo, ao, y_f, y_b, bonus, zg, g2p, ln, sii, wf, wa, wr)


def _rope_tables(n_tokens):
    rows_n = n_tokens // GRID_W
    row = jnp.repeat(jnp.arange(rows_n), GRID_W).astype(F32)
    colp = jnp.tile(jnp.arange(GRID_W), rows_n).astype(F32)
    n_freq = HEAD_DIM // 4
    inv = ROPE_BASE ** (-jnp.arange(n_freq, dtype=F32) / n_freq)
    ang = jnp.concatenate([row[:, None] * inv, colp[:, None] * inv], axis=-1)
    cos = jnp.tile(jnp.cos(ang), (1, 4))
    sin = jnp.tile(jnp.sin(ang), (1, 4))
    return cos, sin


def _rot_cols(w):
    d, n = w.shape
    w4 = w.reshape(d, n // HEAD_DIM, 2, HEAD_DIM // 2)
    return jnp.stack([-w4[:, :, 1], w4[:, :, 0]], axis=2).reshape(d, n)


def _layer_weights(li, p):
    f = p["ffn1_wo"].shape[1]
    fp = -(-f // (2 * LANES)) * (2 * LANES)
    out = {}
    for name in ("ffn1", "ffn2"):
        wi = p[name + "_wi"][li]
        wo = p[name + "_wo"][li]
        d = wi.shape[0]
        out[name] = (jnp.pad(wi.reshape(d, 2, f), ((0, 0), (0, 0), (0, fp - f))).reshape(d, 2 * fp).astype(BF16),
                     jnp.pad(wo, ((0, fp - f), (0, 0))).astype(BF16))
    w_in = p["mix_w_in"][li]
    offs = np.cumsum([0, FOURIER_W, ATTN_W, KV_W, KV_W, RWKV_W, RWKV_W, RWKV_W,
                      DECAY_LORA, DECAY_LORA, ICLR_LORA, ICLR_LORA, GATE_LORA])
    part = [w_in[:, offs[i]:offs[i + 1]] for i in range(12)]
    wf, wq, wk, wv, wrr, wrk, wrv = part[:7]
    cols = {"f": wf, "q": wq, "qr": _rot_cols(wq), "k": wk, "kr": _rot_cols(wk), "v": wv,
            "r": _take_cols(wrr, J_IDX), "rk": _take_cols(wrk, J_IDX), "rv": _take_cols(wrv, I_IDX_COMPACT),
            "lora": jnp.concatenate(part[7:11], axis=1), "g": part[11]}
    out["w_all"] = jnp.concatenate([cols[n] for n, _ in MIX_COLS], axis=1).astype(BF16)
    conv = p["rwkv_conv"][li]
    out["cwr"] = _take_cols(conv[:, :RWKV_W], J_IDX)
    out["cwk"] = _take_cols(conv[:, RWKV_W:2 * RWKV_W], J_IDX)
    out["cwv"] = _take_cols(conv[:, 2 * RWKV_W:], I_IDX_COMPACT)
    def per_direction(w2):
        w2 = _take_cols(w2, J_IDX)
        z = jnp.zeros_like(w2[0])
        w2 = jnp.stack([jnp.concatenate([w2[0], z], axis=0), jnp.concatenate([z, w2[1]], axis=0)])
        hi = w2.astype(BF16)
        return jnp.stack([hi, (w2 - hi.astype(F32)).astype(BF16)])

    out["w0"] = _take_cols(p["rwkv_w0"][li], J_IDX)
    out["w2"] = per_direction(p["rwkv_w2"][li])
    out["a0"] = _take_cols(p["rwkv_a0"][li], J_IDX)
    out["a2"] = per_direction(p["rwkv_a2"][li])
    out["vec"] = _take_cols(jnp.stack([p["rwkv_k_k"][li], p["rwkv_k_a"][li], p["rwkv_r_k"][li].reshape(-1)]), J_IDX)
    out["g2"] = _take_cols(p["rwkv_g2"][li], I_IDX_COMPACT).astype(BF16)
    out["ln"] = _take_cols(jnp.stack([p["rwkv_ln_g"][li], p["rwkv_ln_b"][li]]), I_IDX_COMPACT)
    w_out = p["mix_w_out"][li]
    out["wo_f"] = w_out[:FOURIER_W].astype(BF16)
    out["wo_a"] = w_out[FOURIER_W:FOURIER_W + ATTN_W].astype(BF16)
    out["wo_r"] = _take_rows(w_out[FOURIER_W + ATTN_W:], I_IDX_COMPACT).astype(BF16)
    sink = p["attn_sink"][li]
    out["sink"] = jnp.concatenate([sink, jnp.zeros((8 - ATTN_HEADS,), F32)])
    return out


def kernel(x, c, ctx, c_ctx, mod_w, mod_b, norm_g, ffn1_wi, ffn1_wo, mix_w_in, mix_w_out, attn_sink,
           rwkv_conv, rwkv_w0, rwkv_w2, rwkv_a0, rwkv_a2, rwkv_g2, rwkv_k_k, rwkv_k_a, rwkv_r_k,
           rwkv_ln_g, rwkv_ln_b, ffn2_wi, ffn2_wo):
    p = dict(ffn1_wi=ffn1_wi, ffn1_wo=ffn1_wo, ffn2_wi=ffn2_wi, ffn2_wo=ffn2_wo, mix_w_in=mix_w_in,
             mix_w_out=mix_w_out, attn_sink=attn_sink, rwkv_conv=rwkv_conv, rwkv_w0=rwkv_w0, rwkv_w2=rwkv_w2,
             rwkv_a0=rwkv_a0, rwkv_a2=rwkv_a2, rwkv_g2=rwkv_g2, rwkv_k_k=rwkv_k_k, rwkv_k_a=rwkv_k_a,
             rwkv_r_k=rwkv_r_k, rwkv_ln_g=rwkv_ln_g, rwkv_ln_b=rwkv_ln_b)
    b, s, d = x.shape
    n_c = ctx.shape[1]
    depth = mod_w.shape[0]
    assert b + 1 <= 8 and s % WINDOW == 0 and n_c % WINDOW == 0 and WINDOW % SCAN_T == 0
    tm_l = 256
    tm_ffn = 512 if s % 512 == 0 else tm_l
    tm_c = min(256, n_c)

    cond8 = jnp.zeros((8, d), F32).at[:b].set(c).at[b].set(c_ctx)
    mod = _ada_mod(cond8, mod_w, mod_b).reshape(depth, 8, N_MOD, d)
    cos_l, sin_l = _rope_tables(s)
    cos_c = jnp.ones((n_c, LANES), F32)
    sin_c = jnp.zeros((n_c, LANES), F32)
    r2 = jnp.asarray(_sg_expand_matrices(), BF16)
    sjj, sji, sii = (jnp.asarray(m, BF16) for m in _seg_matrices())
    zero_state = jnp.zeros((N_SG, I_HI, JJ, SUBLANES, LANES), F32)

    xl, xc = x, ctx
    for li in range(depth):
        need_ctx_out = li < depth - 1
        w = _layer_weights(li, p)
        ml = mod[li, :b]
        mc = jnp.broadcast_to(mod[li, b:b + 1], (b, N_MOD, d))
        g = norm_g[li]
        xl = _ffn_half(xl, ml[:, 0:3], g[0:2], *w["ffn1"], tm_ffn)
        xc = _ffn_half(xc, mc[:, 0:3], g[0:2], *w["ffn1"], tm_c)

        def mixer_in(xx, mm, cos, sin, tm):
            return _mix_in(xx, mm[:, 3:6], g[2:3], w["w_all"], cos, sin, tm)

        fl, ql, kl, vl, rl, rkl, rvl, loral, zgl = mixer_in(xl, ml, cos_l, sin_l, tm_l)
        fc, qc, kc, vc, rc_, rkc, rvc, lorac, zgc = mixer_in(xc, mc, cos_c, sin_c, tm_c)

        attn_l = _attention(ql, kl, vl, kc, vc, w["sink"], True)

        def prep(r_, k_, v_, lora_, tm):
            return _rwkv_prep(r_, k_, v_, lora_, w["cwr"], w["cwk"], w["cwv"], w["w0"], w["w2"],
                              w["a0"], w["a2"], w["vec"], sjj, sji, tm)

        pc = prep(rc_, rkc, rvc, lorac, tm_c)
        plat = prep(rl, rkl, rvl, loral, tm_l)
        ycf, ycb, state_c = _rwkv_scan(pc[0], *pc[2:], zero_state, r2)
        ylf, ylb, _ = _rwkv_scan(plat[0], *plat[2:], state_c, r2)

        def mixer_out(xx, mm, fo, ao, y_f, y_b, bonus, zg, tm):
            return _mix_out(xx, mm[:, 5:6], g[3:4], fo, ao, y_f, y_b, bonus, zg, w["g2"], w["ln"], sii,
                            w["wo_f"], w["wo_a"], w["wo_r"], tm)

        xl = mixer_out(xl, ml, _fourier_mix(fl), attn_l, ylf, ylb, plat[1], zgl, tm_l)
        if need_ctx_out:
            attn_c = _attention(qc, None, None, kc, vc, w["sink"], False)
            xc = mixer_out(xc, mc, _fourier_mix(fc), attn_c, ycf, ycb, pc[1], zgc, tm_c)
            xc = _ffn_half(xc, mc[:, 6:9], g[4:6], *w["ffn2"], tm_c)
        xl = _ffn_half(xl, ml[:, 6:9], g[4:6], *w["ffn2"], tm_ffn)
    return xl
```

```python
import functools

import numpy as np
import jax
import jax.numpy as jnp
from jax import lax
from jax.experimental import pallas as pl
from jax.experimental.pallas import tpu as pltpu

F32 = jnp.float32
BF16 = jnp.bfloat16

HEAD_DIM = 64
GRID_W = 64
FOURIER_W = 256
FOURIER_GROUPS = 4
ATTN_HEADS = 6
ATTN_KV_HEADS = 2
ATTN_GROUP = ATTN_HEADS // ATTN_KV_HEADS
ATTN_W = ATTN_HEADS * HEAD_DIM
KV_W = ATTN_KV_HEADS * HEAD_DIM
RWKV_HEADS = 6
RWKV_W = RWKV_HEADS * HEAD_DIM
WINDOW = 128
ROPE_BASE = 10000.0
DECAY_LORA = 64
ICLR_LORA = 64
GATE_LORA = 128
CONV_W = 3
N_MOD = 9
NORM_EPS = 1e-6
GN_EPS = 64e-5

LANES = 128
SUBLANES = 8
VMEM_LIMIT = 56 * 1024 * 1024

HEAD_SLOTS = 8
J_PARTS = 4
I_REP = 4
JJ = HEAD_DIM // J_PARTS
I_HI = HEAD_DIM // (SUBLANES * I_REP)
PART_LANES = LANES // J_PARTS
J_W = 4 * LANES
E_W = 4 * LANES
I_W = SUBLANES * LANES
SCAN_T = 64
SCAN_UNROLL = 32
FOURIER_N1 = 64
FOURIER_DIRECT_MAX = 256
FOURIER_K1_BLOCK = 4


def _j_layout():
    idx = np.full((J_W,), -1, np.int64)
    for jh in range(4):
        for jl in range(4):
            for part in range(J_PARTS):
                for h in range(RWKV_HEADS):
                    idx[jh * 128 + jl * 32 + part * 8 + h] = h * HEAD_DIM + part * JJ + jh * 4 + jl
    return idx


def _i_layout_compact():
    idx = np.full((I_W // 2,), -1, np.int64)
    for isub in range(8):
        for q in range(2):
            for h in range(RWKV_HEADS):
                for il in range(I_REP):
                    idx[isub * 64 + q * 32 + h * 4 + il] = h * HEAD_DIM + q * 32 + isub * 4 + il
    return idx


J_IDX = _j_layout()
I_IDX_COMPACT = _i_layout_compact()


def _take_cols(w, idx):
    g = jnp.take(w, jnp.asarray(np.maximum(idx, 0)), axis=-1)
    return g * jnp.asarray((idx >= 0).astype(np.float32))


def _take_rows(w, idx):
    g = jnp.take(w, jnp.asarray(np.maximum(idx, 0)), axis=0)
    return g * jnp.asarray((idx >= 0).astype(np.float32))[:, None]


def _seg_matrices():
    hj = np.arange(LANES) % 8
    hi = (np.arange(LANES) // I_REP) % 8
    jj = (hj[:, None] == hj[None, :]).astype(np.float32)
    ji = (hj[:, None] == hi[None, :]).astype(np.float32)
    ii = (hi[:, None] == hi[None, :]).astype(np.float32)
    return jj, ji, ii


def _cparams(sem, vmem=VMEM_LIMIT):
    return pltpu.CompilerParams(dimension_semantics=sem, vmem_limit_bytes=vmem)


def _const_spec(shape):
    nd = len(shape)
    return pl.BlockSpec(shape, lambda *_: (0,) * nd)


def _resident_spec(shape):
    nd = len(shape)
    return pl.BlockSpec(shape, lambda *_: (0,) * nd, pipeline_mode=pl.Buffered(1))


def _sigmoid(x):
    return 1.0 / (1.0 + jnp.exp(-x))


def _rms(x, g):
    return x * lax.rsqrt(jnp.mean(x * x, axis=-1, keepdims=True) + NORM_EPS) * g


def _dot(a, b):
    return jnp.dot(a, b, preferred_element_type=F32)


def _split_dot(x, m, passes):
    acc = None
    rem = x
    for p in range(passes):
        piece = rem.astype(BF16)
        d = _dot(piece, m)
        acc = d if acc is None else acc + d
        if p + 1 < passes:
            rem = rem - piece.astype(F32)
    return acc


def _ada_kernel(c_ref, w_ref, b_ref, o_ref):
    c = c_ref[...]
    s = c * _sigmoid(c)
    o_ref[0] = _dot(s.astype(BF16), w_ref[0].astype(BF16)) + b_ref[0]


def _ada_mod(cond8, mod_w, mod_b):
    nl, d, nw = mod_w.shape
    tn = nw // 8
    return pl.pallas_call(
        _ada_kernel,
        out_shape=jax.ShapeDtypeStruct((nl, 8, nw), F32),
        grid=(nl, nw // tn),
        in_specs=[pl.BlockSpec((8, d), lambda l, j: (0, 0)),
                  pl.BlockSpec((1, d, tn), lambda l, j: (l, 0, j)),
                  pl.BlockSpec((1, 1, tn), lambda l, j: (l, 0, j))],
        out_specs=pl.BlockSpec((1, 8, tn), lambda l, j: (l, 0, j)),
        compiler_params=_cparams(("arbitrary", "arbitrary")),
        name="ada_mod",
    )(cond8, mod_w, mod_b.reshape(nl, 1, nw))


def _ffn_kernel(x_ref, m_ref, g_ref, wi_ref, wo_ref, o_ref):
    x = x_ref[0]
    m = m_ref[0]
    g = g_ref[...]
    h = _rms(x, g[0:1]) * (1.0 + m[1:2]) + m[0:1]
    hb = h.astype(BF16)
    fp = wo_ref.shape[0]
    gate = _dot(hb, wi_ref[:, :fp])
    up = _dot(hb, wi_ref[:, fp:])
    act = (gate * _sigmoid(gate) * up).astype(BF16)
    y = _dot(act, wo_ref[...])
    o_ref[0] = x + 0.5 * m[2:3] * _rms(y, g[1:2])


def _ffn_half(x, m3, g2, wi, wo, tm):
    b, n, d = x.shape
    fp = wo.shape[0]
    return pl.pallas_call(
        _ffn_kernel,
        out_shape=jax.ShapeDtypeStruct(x.shape, F32),
        grid=(b, n // tm),
        in_specs=[pl.BlockSpec((1, tm, d), lambda i, j: (i, j, 0)),
                  pl.BlockSpec((1, 3, d), lambda i, j: (i, 0, 0)),
                  _const_spec((2, d)),
                  _resident_spec((d, 2 * fp)), _resident_spec((fp, d))],
        out_specs=pl.BlockSpec((1, tm, d), lambda i, j: (i, j, 0)),
        compiler_params=_cparams(("arbitrary", "arbitrary")),
        name="ffn_half",
    )(x, m3, g2, wi, wo)


MIX_COLS = (("f", FOURIER_W), ("q", ATTN_W), ("qr", ATTN_W), ("k", KV_W), ("kr", KV_W),
            ("v", KV_W), ("r", J_W), ("rk", J_W), ("rv", I_W // 2), ("lora", 4 * DECAY_LORA), ("g", GATE_LORA))
MIX_OFF = {}
_o = 0
for _n, _w in MIX_COLS:
    MIX_OFF[_n] = (_o, _o + _w)
    _o += _w
MIX_TOTAL = _o


def _mix_in_kernel(x_ref, m_ref, g_ref, w_ref, cos_ref, sin_ref,
                   f_ref, q_ref, k_ref, v_ref, r_ref, rk_ref, rv_ref, lora_ref, zg_ref):
    x = x_ref[0]
    m = m_ref[0]
    h = _rms(x, g_ref[...]) * (1.0 + m[1:2]) + m[0:1]
    z = _dot(h.astype(BF16), w_ref[...])

    def col(name):
        lo, hi = MIX_OFF[name]
        return z[:, lo:hi]

    cos = cos_ref[...]
    sin = sin_ref[...]
    cos3 = jnp.concatenate([cos, cos, cos], axis=1)
    sin3 = jnp.concatenate([sin, sin, sin], axis=1)

    def with_swapped(x):
        return jnp.concatenate([x, pltpu.roll(x, HEAD_DIM, 1)], axis=1).astype(BF16)

    f_ref[0] = col("f")
    q_ref[0] = ((col("q") * cos3 + col("qr") * sin3) * (HEAD_DIM ** -0.5)).astype(BF16)
    k_ref[0] = with_swapped(col("k") * cos + col("kr") * sin)
    v_ref[0] = with_swapped(col("v"))
    r_ref[0] = col("r")
    rk_ref[0] = col("rk")
    rv_ref[0] = col("rv")
    lora_ref[0] = col("lora")
    zg_ref[0] = col("g")


def _mix_in(x, m3, g1, w_all, cos, sin, tm):
    b, n, d = x.shape
    widths = (FOURIER_W, ATTN_W, 2 * KV_W, 2 * KV_W, J_W, J_W, I_W // 2, 4 * DECAY_LORA, GATE_LORA)
    dtypes = (F32, BF16, BF16, BF16, F32, F32, F32, F32, F32)
    return pl.pallas_call(
        _mix_in_kernel,
        out_shape=[jax.ShapeDtypeStruct((b, n, w), dt) for w, dt in zip(widths, dtypes)],
        grid=(b, n // tm),
        in_specs=[pl.BlockSpec((1, tm, d), lambda i, j: (i, j, 0)),
                  pl.BlockSpec((1, 3, d), lambda i, j: (i, 0, 0)),
                  _const_spec((1, d)),
                  _const_spec((d, MIX_TOTAL)),
                  pl.BlockSpec((tm, LANES), lambda i, j: (j, 0)),
                  pl.BlockSpec((tm, LANES), lambda i, j: (j, 0))],
        out_specs=[pl.BlockSpec((1, tm, w), lambda i, j: (i, j, 0)) for w in widths],
        compiler_params=_cparams(("arbitrary", "arbitrary")),
        name="mix_in",
    )(x, m3, g1, w_all, cos, sin)


def _fourier_stage1_kernel(l_ref, x_ref, o_ref):
    o_ref[0] = _dot(l_ref[...], x_ref[0].astype(BF16))


def _fourier_stage2_kernel(a_ref, mc_ref, ms_ref, cc_ref, sc_ref, o_ref, *, scale):
    w = cc_ref.shape[0]
    for j in range(a_ref.shape[2]):
        ar = a_ref[0, 0, j].astype(BF16)
        ai = a_ref[0, 1, j].astype(BF16)
        mc = mc_ref[j]
        ms = ms_ref[j]
        gr = _dot(mc, ar) + _dot(ms, ai)
        gi = _dot(mc, ai) - _dot(ms, ar)
        o_ref[0, :, j * w:(j + 1) * w] = (_dot(gr.astype(BF16), cc_ref[...]) + _dot(gi.astype(BF16), sc_ref[...])) * scale


def _fourier_small_kernel(z_ref, cn_ref, sn_ref, cc_ref, sc_ref, o_ref, *, scale):
    z = z_ref[0].astype(BF16)
    t1 = _dot(z, cc_ref[...]).astype(BF16)
    t2 = _dot(z, sc_ref[...]).astype(BF16)
    o_ref[0] = (_dot(cn_ref[...], t1) - _dot(sn_ref[...], t2)) * scale


def _channel_dft():
    gw = FOURIER_W // FOURIER_GROUPS
    c = np.arange(FOURIER_W)
    same = (c[:, None] // gw) == (c[None, :] // gw)
    ang = 2.0 * np.pi * ((c[:, None] % gw) * (c[None, :] % gw) % gw) / gw
    return (np.cos(ang) * same).astype(np.float32), (np.sin(ang) * same).astype(np.float32)


def _fourier_mix(z):
    b, n, w = z.shape
    gw = w // FOURIER_GROUPS
    scale = float(1.0 / np.sqrt(n * gw))
    cc, sc = _channel_dft()
    cc = jnp.asarray(cc, BF16)
    sc = jnp.asarray(sc, BF16)
    if n <= FOURIER_DIRECT_MAX:
        p = np.arange(n)
        ang = 2.0 * np.pi * ((p[:, None] * p[None, :]) % n) / n
        return pl.pallas_call(
            functools.partial(_fourier_small_kernel, scale=scale),
            out_shape=jax.ShapeDtypeStruct((b, n, w), F32),
            grid=(b,),
            in_specs=[pl.BlockSpec((1, n, w), lambda i: (i, 0, 0)),
                      _const_spec((n, n)), _const_spec((n, n)), _const_spec((w, w)), _const_spec((w, w))],
            out_specs=pl.BlockSpec((1, n, w), lambda i: (i, 0, 0)),
            compiler_params=_cparams(("arbitrary",)),
            name="fourier_small",
        )(z, jnp.asarray(np.cos(ang), BF16), jnp.asarray(np.sin(ang), BF16), cc, sc)
    n1 = FOURIER_N1
    n2 = n // n1
    k1 = np.arange(n1)
    ang1 = 2.0 * np.pi * ((k1[:, None] * k1[None, :]) % n1) / n1
    lhs1 = jnp.asarray(np.concatenate([np.cos(ang1), -np.sin(ang1)], axis=0), BF16)
    tc = min(n2 * w, 4096)
    a = pl.pallas_call(
        _fourier_stage1_kernel,
        out_shape=jax.ShapeDtypeStruct((b, 2 * n1, n2 * w), F32),
        grid=(b, (n2 * w) // tc),
        in_specs=[_const_spec((2 * n1, n1)),
                  pl.BlockSpec((1, n1, tc), lambda i, j: (i, 0, j))],
        out_specs=pl.BlockSpec((1, 2 * n1, tc), lambda i, j: (i, 0, j)),
        compiler_params=_cparams(("arbitrary", "arbitrary")),
        name="fourier_stage1",
    )(lhs1, z.reshape(b, n1, n2 * w))
    k2 = np.arange(n2)
    freq = (k1[:, None, None] + n1 * k2[None, :, None]) * k2[None, None, :]
    ang2 = 2.0 * np.pi * (freq % n) / n
    out = pl.pallas_call(
        functools.partial(_fourier_stage2_kernel, scale=scale),
        out_shape=jax.ShapeDtypeStruct((b, n2, n1 * w), F32),
        grid=(b, n1 // FOURIER_K1_BLOCK),
        in_specs=[pl.BlockSpec((1, 2, FOURIER_K1_BLOCK, n2, w), lambda i, j: (i, 0, j, 0, 0)),
                  pl.BlockSpec((FOURIER_K1_BLOCK, n2, n2), lambda i, j: (j, 0, 0)),
                  pl.BlockSpec((FOURIER_K1_BLOCK, n2, n2), lambda i, j: (j, 0, 0)),
                  _const_spec((w, w)), _const_spec((w, w))],
        out_specs=pl.BlockSpec((1, n2, FOURIER_K1_BLOCK * w), lambda i, j: (i, 0, j)),
        compiler_params=_cparams(("arbitrary", "arbitrary")),
        name="fourier_stage2",
    )(a.reshape(b, 2, n1, n2, w), jnp.asarray(np.cos(ang2), BF16), jnp.asarray(np.sin(ang2), BF16), cc, sc)
    return out.reshape(b, n, w)


NEG = -1e30


def _attn_kernel(*refs, window):
    if window:
        (sink_ref, q_ref, kp_ref, kc_ref, kn_ref, vp_ref, vc_ref, vn_ref, kx_ref, vx_ref, o_ref, s_ref) = refs
    else:
        (sink_ref, q_ref, kx_ref, vx_ref, o_ref, s_ref) = refs
    i = pl.program_id(1)
    nb = pl.num_programs(1)
    tq = q_ref.shape[1]
    lane = lax.broadcasted_iota(jnp.int32, (1, KV_W), 1)
    low = lane < HEAD_DIM
    if window:
        k_all = jnp.concatenate([kp_ref[0], kc_ref[0], kn_ref[0], kx_ref[0]], axis=0)
        v_all = jnp.concatenate([vp_ref[0], vc_ref[0], vn_ref[0], vx_ref[0]], axis=0)
        row = lax.broadcasted_iota(jnp.int32, (tq, WINDOW), 0)
        col = lax.broadcasted_iota(jnp.int32, (tq, WINDOW), 1)
        prev_ok = jnp.logical_and(col >= row, i > 0)
        next_ok = jnp.logical_and(col <= row, i < nb - 1)
        mask = jnp.concatenate([jnp.where(prev_ok, 0.0, NEG), jnp.zeros((tq, WINDOW), F32),
                                jnp.where(next_ok, 0.0, NEG), jnp.zeros((tq, kx_ref.shape[1]), F32)], axis=1)
    else:
        k_all = kx_ref[0]
        v_all = vx_ref[0]
        mask = None
    zero = jnp.zeros((), BF16)
    one = jnp.ones((), BF16)
    def operands(h):
        slot = h % 2
        swap = slot != h // ATTN_GROUP
        keep = (lane >= HEAD_DIM) if slot else low
        kh = jnp.where(keep, k_all[:, KV_W:] if swap else k_all[:, :KV_W], zero)
        vh = jnp.where(keep, v_all[:, KV_W:] if swap else v_all[:, :KV_W], one)
        return kh, vh

    def scores(h):
        qp = q_ref[0, :, (h // 2) * LANES:(h // 2 + 1) * LANES]
        s = lax.dot_general(qp, operands(h)[0], (((1,), (1,)), ((), ())), preferred_element_type=F32)
        return s if mask is None else s + mask

    for h in range(ATTN_HEADS):
        s_ref[h] = scores(h)
    for pair in range(ATTN_HEADS // 2):
        outs = []
        sinks = []
        for slot in range(2):
            h = 2 * pair + slot
            s = s_ref[h]
            sk = sink_ref[h]
            mx = jnp.maximum(s.max(axis=-1, keepdims=True), sk)
            outs.append(_dot(jnp.exp(s - mx).astype(BF16), operands(h)[1]))
            sinks.append(jnp.exp(sk - mx))
        num = jnp.where(low, outs[0], outs[1])
        den = pltpu.roll(jnp.where(low, outs[1], outs[0]), HEAD_DIM, 1) + jnp.where(low, sinks[0], sinks[1])
        o_ref[0, :, pair * LANES:(pair + 1) * LANES] = num / den


def _attention(q, k, v, kx, vx, sink8, window):
    b, n, _ = q.shape
    c = kx.shape[1]
    tq = WINDOW
    nb = n // tq
    smem = pl.BlockSpec(memory_space=pltpu.SMEM)
    qspec = pl.BlockSpec((1, tq, ATTN_W), lambda i, j: (i, j, 0))
    xspec = pl.BlockSpec((1, c, 2 * KV_W), lambda i, j: (i, 0, 0))
    if window:
        prev = pl.BlockSpec((1, tq, 2 * KV_W), lambda i, j: (i, jnp.maximum(j - 1, 0), 0))
        cur = pl.BlockSpec((1, tq, 2 * KV_W), lambda i, j: (i, j, 0))
        nxt = pl.BlockSpec((1, tq, 2 * KV_W), lambda i, j: (i, jnp.minimum(j + 1, nb - 1), 0))
        in_specs = [smem, qspec, prev, cur, nxt, prev, cur, nxt, xspec, xspec]
        args = (sink8, q, k, k, k, v, v, v, kx, vx)
    else:
        in_specs = [smem, qspec, xspec, xspec]
        args = (sink8, q, kx, vx)
    return pl.pallas_call(
        functools.partial(_attn_kernel, window=window),
        out_shape=jax.ShapeDtypeStruct((b, n, ATTN_W), F32),
        grid=(b, nb),
        in_specs=in_specs,
        out_specs=pl.BlockSpec((1, tq, ATTN_W), lambda i, j: (i, j, 0)),
        scratch_shapes=[pltpu.VMEM((ATTN_HEADS, tq, (3 * tq if window else 0) + c), F32)],
        compiler_params=_cparams(("arbitrary", "arbitrary")),
        name="window_attention" if window else "context_attention",
    )(*args)


def _lane_blocks_to_rows(x):
    return pltpu.einshape("t(kl)->tkl", x, l=LANES)


def _shift_rows(x, prev_row, next_row):
    n = x.shape[0]
    row = lax.broadcasted_iota(jnp.int32, (n, 1), 0)
    xp = jnp.where(row == 0, prev_row, pltpu.roll(x, 1, 0))
    xn = jnp.where(row == n - 1, next_row, pltpu.roll(x, n - 1, 0))
    return xp, xn


def _rwkv_prep_kernel(r_ref, rp_ref, rn_ref, k_ref, kp_ref, kn_ref, v_ref, vp_ref, vn_ref, lora_ref,
                      cwr_ref, cwk_ref, cwv_ref, w0_ref, w2_ref, a0_ref, a2_ref, vec_ref, sjj_ref, sji_ref, tri_ref,
                      vt_ref, bonus_ref, ab_ref, kr_ref, gm_ref):
    i = pl.program_id(1)
    first = (i > 0).astype(F32)
    last = (i < pl.num_programs(1) - 1).astype(F32)

    def conv(x_ref, p_ref, n_ref, cw_ref):
        x = x_ref[0]
        xp, xn = _shift_rows(x, p_ref[0, SUBLANES - 1:SUBLANES, :] * first, n_ref[0, 0:1, :] * last)
        cw = cw_ref[...]
        return xp * cw[0:1] + x * cw[1:2] + xn * cw[2:3]

    r = conv(r_ref, rp_ref, rn_ref, cwr_ref)
    k = conv(k_ref, kp_ref, kn_ref, cwk_ref)
    vh = conv(v_ref, vp_ref, vn_ref, cwv_ref)
    low = lax.broadcasted_iota(jnp.int32, (1, LANES), 1) < LANES // 2
    blocks = []
    for c in range(vh.shape[1] // LANES):
        xc = vh[:, c * LANES:(c + 1) * LANES]
        xr = pltpu.roll(xc, LANES // 2, 1)
        blocks += [jnp.where(low, xc, xr), jnp.where(low, xr, xc)]
    v = jnp.concatenate(blocks, axis=1)
    vec = vec_ref[...]
    sjj = sjj_ref[...]
    sji = sji_ref[...]

    def fold(x):
        return (x[:, 0:LANES] + x[:, LANES:2 * LANES]) + (x[:, 2 * LANES:3 * LANES] + x[:, 3 * LANES:4 * LANES])

    kk = k * vec[0:1]
    ss = _split_dot(fold(kk * kk), sjj, 2)
    inv = lax.rsqrt(jnp.maximum(ss, 1e-24))
    kk = kk * jnp.concatenate([inv, inv, inv, inv], axis=1)
    rk = _split_dot(fold(r * k * vec[2:3]), sji, 2)
    vt_ref[0] = _lane_blocks_to_rows(v)
    bonus_ref[0] = vh * jnp.concatenate([rk] * (vh.shape[1] // LANES), axis=1)
    lora = lora_ref[0]
    zw = jnp.tanh(lora[:, :2 * DECAY_LORA])
    za = lora[:, 2 * DECAY_LORA:]
    zw_hi = zw.astype(BF16)
    zw_lo = (zw - zw_hi.astype(F32)).astype(BF16)
    za_hi = za.astype(BF16)
    za_lo = (za - za_hi.astype(F32)).astype(BF16)

    def lora(x_hi, x_lo, w_ref, d):
        return _dot(x_hi, w_ref[0, d]) + (_dot(x_lo, w_ref[0, d]) + _dot(x_hi, w_ref[1, d]))

    for d in range(2):
        wpre = w0_ref[d:d + 1, :] + lora(zw_hi, zw_lo, w2_ref, d)
        apre = a0_ref[d:d + 1, :] + lora(za_hi, za_lo, a2_ref, d)
        a = _sigmoid(apre)
        log_decay = -np.float32(np.exp(-0.5)) * _sigmoid(wpre)
        tri = tri_ref[d]
        hi = log_decay.astype(BF16)
        rest = log_decay - hi.astype(F32)
        mid = rest.astype(BF16)
        low = (rest - mid.astype(F32)).astype(BF16)
        log_g = _dot(tri, hi) + _dot(tri, mid) + _dot(tri, low)
        g = jnp.exp(log_g)
        g_inv = jnp.exp(-log_g)
        a_t = -kk * jnp.exp(log_g - log_decay)
        b_t = kk * a * g_inv
        k_t = k * (1.0 + (a - 1.0) * vec[1:2]) * g_inv
        r_t = r * g
        ab_ref[d, 0] = _lane_blocks_to_rows(jnp.concatenate([a_t, b_t], axis=1))
        kr_ref[d, 0] = _lane_blocks_to_rows(jnp.concatenate([k_t, r_t], axis=1))
        gm_ref[d, 0] = g


def _rwkv_prep(r, k, v, lora, cwr, cwk, cwv, w0, w2, a0, a2, vec, sjj, sji, tm):
    b, n, _ = r.shape
    nh = n // SUBLANES

    def tile(w):
        return pl.BlockSpec((1, tm, w), lambda i, j: (i, j, 0))

    def halo_prev(w):
        return pl.BlockSpec((1, SUBLANES, w), lambda i, j: (i, jnp.maximum(j * (tm // SUBLANES) - 1, 0), 0))

    def halo_next(w):
        return pl.BlockSpec((1, SUBLANES, w), lambda i, j: (i, jnp.minimum((j + 1) * (tm // SUBLANES), nh - 1), 0))

    def dir_tile(w):
        return pl.BlockSpec((2, 1, tm, w), lambda i, j: (0, i, j, 0))

    t = np.arange(tm)
    same = (t[:, None] // SCAN_T) == (t[None, :] // SCAN_T)
    tri = jnp.asarray(np.stack([same & (t[None, :] <= t[:, None]), same & (t[None, :] >= t[:, None])]), BF16)
    def row_tiles(lead):
        return jax.ShapeDtypeStruct(lead + (n, SUBLANES, LANES), F32)

    def dir_row_tile():
        return pl.BlockSpec((2, 1, tm, SUBLANES, LANES), lambda i, j: (0, i, j, 0, 0))

    out_shape = [row_tiles((b,)), jax.ShapeDtypeStruct((b, n, I_W // 2), F32), row_tiles((2, b)), row_tiles((2, b)),
                 jax.ShapeDtypeStruct((2, b, n, J_W), F32)]
    row_tile = pl.BlockSpec((1, tm, SUBLANES, LANES), lambda i, j: (i, j, 0, 0))
    return pl.pallas_call(
        _rwkv_prep_kernel,
        out_shape=out_shape,
        grid=(b, n // tm),
        in_specs=[tile(J_W), halo_prev(J_W), halo_next(J_W),
                  tile(J_W), halo_prev(J_W), halo_next(J_W),
                  tile(I_W // 2), halo_prev(I_W // 2), halo_next(I_W // 2),
                  tile(4 * DECAY_LORA),
                  _const_spec((CONV_W, J_W)), _const_spec((CONV_W, J_W)), _const_spec((CONV_W, I_W // 2)),
                  _const_spec((2, J_W)), _const_spec((2, 2, 2 * DECAY_LORA, J_W)),
                  _const_spec((2, J_W)), _const_spec((2, 2, 2 * ICLR_LORA, J_W)),
                  _const_spec((3, J_W)), _const_spec((LANES, LANES)), _const_spec((LANES, LANES)),
                  _const_spec((2, tm, tm))],
        out_specs=[row_tile, tile(I_W // 2), dir_row_tile(), dir_row_tile(), dir_tile(J_W)],
        compiler_params=_cparams(("arbitrary", "arbitrary")),
        name="rwkv_prep",
    )(r, r, r, k, k, k, v, v, v, lora, cwr, cwk, cwv, w0, w2, a0, a2, vec, sjj, sji, tri)


N_ROWVEC = 4
N_PAIRS = N_ROWVEC // 2
N_DIR_REFS = N_PAIRS + 2
SCAN_PASSES = 1
DECAY_PASSES = 3
N_SG = 3
SG_PARTS = (((0, 0, 6, 0), (1, 0, 2, 6)),
            ((1, 2, 4, 0), (2, 0, 4, 4)),
            ((2, 4, 2, 0), (3, 0, 6, 2)))
SG_MIXED = 1


def _sg_expand_matrices():
    m = np.zeros((N_SG, 2 * LANES, E_W), np.float32)
    for sg, parts in enumerate(SG_PARTS):
        for half, (_, h0, nh, slot0) in enumerate(parts):
            for jl in range(4):
                for part in range(J_PARTS):
                    for h in range(h0, h0 + nh):
                        for rep in range(I_REP):
                            m[sg, half * LANES + jl * 32 + part * 8 + h,
                              jl * 128 + part * 32 + (slot0 + h - h0) * I_REP + rep] = 1.0
    return m


def _scan_kernel(*refs, t_chunk, n_batch):
    ins = refs[:2 * N_DIR_REFS]
    r2_ref, s0_ref = refs[2 * N_DIR_REFS:2 * N_DIR_REFS + 2]
    y_refs = refs[2 * N_DIR_REFS + 2:2 * N_DIR_REFS + 4]
    sout_ref = refs[2 * N_DIR_REFS + 4]
    e_ref, s_ref, ybuf_ref, yrev_ref = refs[2 * N_DIR_REFS + 5:]
    step_id = pl.program_id(0)
    assert n_batch == 2
    last = t_chunk - 1

    @pl.when(step_id == 0)
    def _():
        s_ref[...] = s0_ref[...]

    def src(x, s):
        return ins[(s // n_batch) * N_DIR_REFS + x]

    def expand(sg, first, second, passes):
        lhs = jnp.concatenate([first, second], axis=2)
        lhs = lhs.reshape(lhs.shape[0] * SUBLANES, 2 * LANES)
        return _split_dot(lhs, r2_ref[sg], passes).reshape(first.shape[0], SUBLANES, E_W)

    for sg, parts in enumerate(SG_PARTS):
        for pair in range(N_PAIRS):
            halves = []
            for s, _, _, _ in parts:
                ref = src(pair, s)
                if sg == SG_MIXED and s >= n_batch:
                    halves.append(jnp.concatenate([ref[0, s % n_batch, t:t + 1] for t in reversed(range(t_chunk))],
                                                  axis=0))
                else:
                    halves.append(ref[0, s % n_batch])
            e_ref[sg, pair] = expand(sg, halves[0], halves[1], SCAN_PASSES)

    lane = lax.broadcasted_iota(jnp.int32, (SUBLANES, LANES), 1)
    q_even = (lane // PART_LANES) % 2 == 0
    slot = (lane // I_REP) % HEAD_SLOTS

    def shifted(x, slots):
        return x if slots == 0 else pltpu.roll(x, (slots * I_REP) % LANES, 1)

    def allparts(p):
        assert J_PARTS == 4
        return ((p + pltpu.roll(p, 2 * PART_LANES, 1))
                + (pltpu.roll(p, PART_LANES, 1) + pltpu.roll(p, 3 * PART_LANES, 1)))

    def step(s, sg):
        (sa_src, ha, _, slot_a), (sb_src, hb, _, slot_b) = SG_PARTS[sg]
        ta = s if sa_src < n_batch else last - s
        tb = s if sb_src < n_batch else last - s
        te = s if sg == SG_MIXED else ta
        vp = jnp.where(slot < slot_b, shifted(src(N_PAIRS + 1, sa_src)[sa_src % n_batch, ta], slot_a - ha),
                       shifted(src(N_PAIRS + 1, sb_src)[sb_src % n_batch, tb], slot_b - hb))
        vr = pltpu.roll(vp, PART_LANES, 1)
        v = (jnp.where(q_even, vp, vr), jnp.where(q_even, vr, vp))

        def row(x, jj):
            r0 = (x % 2) * 4 + jj // 4
            return e_ref[sg, x // 2, te, r0:r0 + 1, (jj % 4) * LANES:(jj % 4 + 1) * LANES]

        def add(acc, ih, jj, term):
            k = (ih, jj % 4)
            acc[k] = term if k not in acc else acc[k] + term

        def total(acc, ih):
            return allparts((acc[ih, 0] + acc[ih, 1]) + (acc[ih, 2] + acc[ih, 3]))

        acc = {}
        for jj in range(JJ):
            a = row(0, jj)
            for ih in range(I_HI):
                add(acc, ih, jj, s_ref[sg, ih, jj] * a)
        sa = [total(acc, ih) for ih in range(I_HI)]
        acc = {}
        for jj in range(JJ):
            b, k, rr = row(1, jj), row(2, jj), row(3, jj)
            for ih in range(I_HI):
                new = (s_ref[sg, ih, jj] + v[ih] * k) + sa[ih] * b
                s_ref[sg, ih, jj] = new
                add(acc, ih, jj, new * rr)
        y = [total(acc, ih) for ih in range(I_HI)]
        packed = jnp.where(q_even, y[0], y[1])
        ybuf_ref[sg, te] = packed
        if sg == SG_MIXED:
            yrev_ref[last - s] = packed

    def one_token(s, carry):
        for sg in range(N_SG):
            step(s, sg)
        return carry

    lax.fori_loop(0, t_chunk, one_token, 0, unroll=SCAN_UNROLL)

    for s in range(2 * n_batch):
        out = jnp.zeros((t_chunk, SUBLANES, LANES), F32)
        for sg, parts in enumerate(SG_PARTS):
            for s2, h0, nh, slot0 in parts:
                if s2 == s:
                    buf = yrev_ref[...] if (sg == SG_MIXED and s >= n_batch) else ybuf_ref[sg]
                    if h0 != slot0:
                        buf = pltpu.roll(buf, ((h0 - slot0) * I_REP) % LANES, 2)
                    out = jnp.where(jnp.logical_and(slot >= h0, slot < h0 + nh)[None], buf, out)
        y_refs[s // n_batch][s % n_batch] = out

    for sg, parts in enumerate(SG_PARTS):
        halves = []
        toks = []
        for s, _, _, _ in parts:
            tile0, tok = (t_chunk - SUBLANES, SUBLANES - 1) if s < n_batch else (0, 0)
            g8 = src(N_PAIRS, s)[0, s % n_batch, tile0:tile0 + SUBLANES, :]
            halves.append(_lane_blocks_to_rows(jnp.concatenate([g8, g8], axis=1)))
            toks.append(tok)
        res = expand(sg, halves[0], halves[1], DECAY_PASSES)
        first_rows, second_rows = res[toks[0]], res[toks[1]]
        for jj in range(JJ):
            r0 = jj // 4
            blk = slice((jj % 4) * LANES, (jj % 4 + 1) * LANES)
            scale = jnp.where(slot[0:1] < parts[1][3], first_rows[r0:r0 + 1, blk], second_rows[r0:r0 + 1, blk])
            for ih in range(I_HI):
                s_ref[sg, ih, jj] = s_ref[sg, ih, jj] * scale

    @pl.when(step_id == pl.num_programs(0) - 1)
    def _():
        sout_ref[...] = s_ref[...]


def _rwkv_scan(vt, ab, kr, gm, state0, r2):
    b, n = vt.shape[:2]
    t = SCAN_T
    nc = n // t
    in_specs = []
    args = []
    out_specs = []
    for d in range(2):
        def chunk(i, d=d):
            return i if d == 0 else nc - 1 - i

        pair_spec = pl.BlockSpec((1, b, t, SUBLANES, LANES), lambda i, d=d, chunk=chunk: (d, 0, chunk(i), 0, 0))
        gm_spec = pl.BlockSpec((1, b, t, J_W), lambda i, d=d, chunk=chunk: (d, 0, chunk(i), 0))
        tile_spec = pl.BlockSpec((b, t, SUBLANES, LANES), lambda i, chunk=chunk: (0, chunk(i), 0, 0))
        in_specs += [pair_spec, pair_spec, gm_spec, tile_spec]
        args += [ab, kr, gm, vt]
        out_specs.append(tile_spec)
    state_shape = (N_SG, I_HI, JJ, SUBLANES, LANES)
    in_specs += [_const_spec((N_SG, 2 * LANES, E_W)), _const_spec(state_shape)]
    args += [r2, state0]
    out_specs.append(_const_spec(state_shape))
    y_shape = jax.ShapeDtypeStruct((b, n, SUBLANES, LANES), F32)
    return pl.pallas_call(
        functools.partial(_scan_kernel, t_chunk=t, n_batch=b),
        out_shape=[y_shape, y_shape, jax.ShapeDtypeStruct(state_shape, F32)],
        grid=(nc,),
        in_specs=in_specs,
        out_specs=out_specs,
        scratch_shapes=[pltpu.VMEM((N_SG, N_PAIRS, t, SUBLANES, E_W), F32),
                        pltpu.VMEM(state_shape, F32),
                        pltpu.VMEM((N_SG, t, SUBLANES, LANES), F32),
                        pltpu.VMEM((t, SUBLANES, LANES), F32)],
        compiler_params=_cparams(("arbitrary",)),
        name="rwkv_scan",
    )(*args)


def _mix_out_kernel(x_ref, m_ref, g_ref, f_ref, a_ref, yf_ref, yb_ref, bonus_ref, zg_ref,
                    g2_ref, ln_ref, sii_ref, wf_ref, wa_ref, wr_ref, o_ref):
    x = x_ref[0]
    yy = pltpu.einshape("tkl->t(kl)", yf_ref[0] + yb_ref[0])
    low = lax.broadcasted_iota(jnp.int32, (1, LANES), 1) < LANES // 2
    y = jnp.concatenate([jnp.where(low, yy[:, 2 * c * LANES:(2 * c + 1) * LANES],
                                   pltpu.roll(yy[:, (2 * c + 1) * LANES:(2 * c + 2) * LANES], LANES // 2, 1))
                         for c in range(I_W // (2 * LANES))], axis=1)
    sii = sii_ref[...]
    nblk = I_W // (2 * LANES)
    cnt = float(HEAD_DIM)

    def fold(z):
        acc = z[:, 0:LANES]
        for c in range(1, nblk):
            acc = acc + z[:, c * LANES:(c + 1) * LANES]
        return acc

    def spread(z):
        return jnp.concatenate([z] * nblk, axis=1)

    mu = _split_dot(fold(y), sii, 2) * (1.0 / cnt)
    dlt = y - spread(mu)
    var = _split_dot(fold(dlt * dlt), sii, 2) * (1.0 / cnt)
    ln = ln_ref[...]
    yn = dlt * spread(lax.rsqrt(var + GN_EPS)) * ln[0:1] + ln[1:2]
    gate = _dot(_sigmoid(zg_ref[0]).astype(BF16), g2_ref[...])
    rw = ((yn + bonus_ref[0]) * gate).astype(BF16)
    o = _dot(f_ref[0].astype(BF16), wf_ref[...]) + _dot(a_ref[0].astype(BF16), wa_ref[...]) + _dot(rw, wr_ref[...])
    o_ref[0] = x + m_ref[0] * _rms(o, g_ref[...])


def _mix_out(x, gate_row, g3, fo, ao, y_f, y_b, bonus, zg, g2p, ln, sii, wf, wa, wr, tm):
    b, n, d = x.shape

    def tile(w):
        return pl.BlockSpec((1, tm, w), lambda i, j: (i, j, 0))

    return pl.pallas_call(
        _mix_out_kernel,
        out_shape=jax.ShapeDtypeStruct(x.shape, F32),
        grid=(b, n // tm),
        in_specs=[tile(d), pl.BlockSpec((1, 1, d), lambda i, j: (i, 0, 0)), _const_spec((1, d)),
                  tile(FOURIER_W), tile(ATTN_W),
                  pl.BlockSpec((1, tm, SUBLANES, LANES), lambda i, j: (i, j, 0, 0)),
                  pl.BlockSpec((1, tm, SUBLANES, LANES), lambda i, j: (i, j, 0, 0)),
                  tile(I_W // 2), tile(GATE_LORA),
                  _const_spec((GATE_LORA, I_W // 2)), _const_spec((2, I_W // 2)), _const_spec((LANES, LANES)),
                  _const_spec((FOURIER_W, d)), _const_spec((ATTN_W, d)), _const_spec((I_W // 2, d))],
        out_specs=tile(d),
        compiler_params=_cparams(("arbitrary", "arbitrary")),
        name="mix_out",
    )(x, gate_row, g3, fo, ao, y_f, y_b, bonus, zg, g2p, ln, sii, wf, wa, wr)


def _rope_tables(n_tokens):
    rows_n = n_tokens // GRID_W
    row = jnp.repeat(jnp.arange(rows_n), GRID_W).astype(F32)
    colp = jnp.tile(jnp.arange(GRID_W), rows_n).astype(F32)
    n_freq = HEAD_DIM // 4
    inv = ROPE_BASE ** (-jnp.arange(n_freq, dtype=F32) / n_freq)
    ang = jnp.concatenate([row[:, None] * inv, colp[:, None] * inv], axis=-1)
    cos = jnp.tile(jnp.cos(ang), (1, 4))
    sin = jnp.tile(jnp.sin(ang), (1, 4))
    return cos, sin


def _rot_cols(w):
    d, n = w.shape
    w4 = w.reshape(d, n // HEAD_DIM, 2, HEAD_DIM // 2)
    return jnp.stack([-w4[:, :, 1], w4[:, :, 0]], axis=2).reshape(d, n)


def _layer_weights(li, p):
    f = p["ffn1_wo"].shape[1]
    fp = -(-f // (2 * LANES)) * (2 * LANES)
    out = {}
    for name in ("ffn1", "ffn2"):
        wi = p[name + "_wi"][li]
        wo = p[name + "_wo"][li]
        gap = jnp.zeros((wi.shape[0], fp - f), BF16)
        out[name] = (jnp.concatenate([wi[:, :f].astype(BF16), gap, wi[:, f:].astype(BF16), gap], axis=1),
                     jnp.pad(wo, ((0, fp - f), (0, 0))).astype(BF16))
    w_in = p["mix_w_in"][li]
    offs = np.cumsum([0, FOURIER_W, ATTN_W, KV_W, KV_W, RWKV_W, RWKV_W, RWKV_W,
                      DECAY_LORA, DECAY_LORA, ICLR_LORA, ICLR_LORA, GATE_LORA])
    part = [w_in[:, offs[i]:offs[i + 1]] for i in range(12)]
    wf, wq, wk, wv, wrr, wrk, wrv = part[:7]
    cols = {"f": wf, "q": wq, "qr": _rot_cols(wq), "k": wk, "kr": _rot_cols(wk), "v": wv,
            "r": _take_cols(wrr, J_IDX), "rk": _take_cols(wrk, J_IDX), "rv": _take_cols(wrv, I_IDX_COMPACT),
            "lora": jnp.concatenate(part[7:11], axis=1), "g": part[11]}
    out["w_all"] = jnp.concatenate([cols[n] for n, _ in MIX_COLS], axis=1).astype(BF16)
    conv = p["rwkv_conv"][li]
    out["cwr"] = _take_cols(conv[:, :RWKV_W], J_IDX)
    out["cwk"] = _take_cols(conv[:, RWKV_W:2 * RWKV_W], J_IDX)
    out["cwv"] = _take_cols(conv[:, 2 * RWKV_W:], I_IDX_COMPACT)
    def per_direction(w2):
        w2 = _take_cols(w2, J_IDX)
        z = jnp.zeros_like(w2[0])
        w2 = jnp.stack([jnp.concatenate([w2[0], z], axis=0), jnp.concatenate([z, w2[1]], axis=0)])
        hi = w2.astype(BF16)
        return jnp.stack([hi, (w2 - hi.astype(F32)).astype(BF16)])

    out["w0"] = _take_cols(p["rwkv_w0"][li], J_IDX)
    out["w2"] = per_direction(p["rwkv_w2"][li])
    out["a0"] = _take_cols(p["rwkv_a0"][li], J_IDX)
    out["a2"] = per_direction(p["rwkv_a2"][li])
    out["vec"] = _take_cols(jnp.stack([p["rwkv_k_k"][li], p["rwkv_k_a"][li], p["rwkv_r_k"][li].reshape(-1)]), J_IDX)
    out["g2"] = _take_cols(p["rwkv_g2"][li], I_IDX_COMPACT).astype(BF16)
    out["ln"] = _take_cols(jnp.stack([p["rwkv_ln_g"][li], p["rwkv_ln_b"][li]]), I_IDX_COMPACT)
    w_out = p["mix_w_out"][li]
    out["wo_f"] = w_out[:FOURIER_W].astype(BF16)
    out["wo_a"] = w_out[FOURIER_W:FOURIER_W + ATTN_W].astype(BF16)
    out["wo_r"] = _take_rows(w_out[FOURIER_W + ATTN_W:], I_IDX_COMPACT).astype(BF16)
    sink = p["attn_sink"][li]
    out["sink"] = jnp.concatenate([sink, jnp.zeros((8 - ATTN_HEADS,), F32)])
    return out


def kernel(x, c, ctx, c_ctx, mod_w, mod_b, norm_g, ffn1_wi, ffn1_wo, mix_w_in, mix_w_out, attn_sink,
           rwkv_conv, rwkv_w0, rwkv_w2, rwkv_a0, rwkv_a2, rwkv_g2, rwkv_k_k, rwkv_k_a, rwkv_r_k,
           rwkv_ln_g, rwkv_ln_b, ffn2_wi, ffn2_wo):
    p = dict(ffn1_wi=ffn1_wi, ffn1_wo=ffn1_wo, ffn2_wi=ffn2_wi, ffn2_wo=ffn2_wo, mix_w_in=mix_w_in,
             mix_w_out=mix_w_out, attn_sink=attn_sink, rwkv_conv=rwkv_conv, rwkv_w0=rwkv_w0, rwkv_w2=rwkv_w2,
             rwkv_a0=rwkv_a0, rwkv_a2=rwkv_a2, rwkv_g2=rwkv_g2, rwkv_k_k=rwkv_k_k, rwkv_k_a=rwkv_k_a,
             rwkv_r_k=rwkv_r_k, rwkv_ln_g=rwkv_ln_g, rwkv_ln_b=rwkv_ln_b)
    b, s, d = x.shape
    n_c = ctx.shape[1]
    depth = mod_w.shape[0]
    assert b + 1 <= 8 and s % WINDOW == 0 and n_c % WINDOW == 0 and WINDOW % SCAN_T == 0
    tm_l = 256
    tm_ffn = 512 if s % 512 == 0 else tm_l
    tm_c = min(256, n_c)

    cond8 = jnp.zeros((8, d), F32).at[:b].set(c).at[b].set(c_ctx)
    mod = _ada_mod(cond8, mod_w, mod_b).reshape(depth, 8, N_MOD, d)
    cos_l, sin_l = _rope_tables(s)
    cos_c = jnp.ones((n_c, LANES), F32)
    sin_c = jnp.zeros((n_c, LANES), F32)
    r2 = jnp.asarray(_sg_expand_matrices(), BF16)
    sjj, sji, sii = (jnp.asarray(m, BF16) for m in _seg_matrices())
    zero_state = jnp.zeros((N_SG, I_HI, JJ, SUBLANES, LANES), F32)

    xl, xc = x, ctx
    for li in range(depth):
        need_ctx_out = li < depth - 1
        w = _layer_weights(li, p)
        ml = mod[li, :b]
        mc = jnp.broadcast_to(mod[li, b:b + 1], (b, N_MOD, d))
        g = norm_g[li]
        xl = _ffn_half(xl, ml[:, 0:3], g[0:2], *w["ffn1"], tm_ffn)
        xc = _ffn_half(xc, mc[:, 0:3], g[0:2], *w["ffn1"], tm_c)

        def mixer_in(xx, mm, cos, sin, tm):
            return _mix_in(xx, mm[:, 3:6], g[2:3], w["w_all"], cos, sin, tm)

        fl, ql, kl, vl, rl, rkl, rvl, loral, zgl = mixer_in(xl, ml, cos_l, sin_l, tm_l)
        fc, qc, kc, vc, rc_, rkc, rvc, lorac, zgc = mixer_in(xc, mc, cos_c, sin_c, tm_c)

        attn_l = _attention(ql, kl, vl, kc, vc, w["sink"], True)

        def prep(r_, k_, v_, lora_, tm):
            return _rwkv_prep(r_, k_, v_, lora_, w["cwr"], w["cwk"], w["cwv"], w["w0"], w["w2"],
                              w["a0"], w["a2"], w["vec"], sjj, sji, tm)

        pc = prep(rc_, rkc, rvc, lorac, tm_c)
        plat = prep(rl, rkl, rvl, loral, tm_l)
        ycf, ycb, state_c = _rwkv_scan(pc[0], *pc[2:], zero_state, r2)
        ylf, ylb, _ = _rwkv_scan(plat[0], *plat[2:], state_c, r2)

        def mixer_out(xx, mm, fo, ao, y_f, y_b, bonus, zg, tm):
            return _mix_out(xx, mm[:, 5:6], g[3:4], fo, ao, y_f, y_b, bonus, zg, w["g2"], w["ln"], sii,
                            w["wo_f"], w["wo_a"], w["wo_r"], tm)

        xl = mixer_out(xl, ml, _fourier_mix(fl), attn_l, ylf, ylb, plat[1], zgl, tm_l)
        if need_ctx_out:
            attn_c = _attention(qc, None, None, kc, vc, w["sink"], False)
            xc = mixer_out(xc, mc, _fourier_mix(fc), attn_c, ycf, ycb, pc[1], zgc, tm_c)
            xc = _ffn_half(xc, mc[:, 6:9], g[4:6], *w["ffn2"], tm_c)
        xl = _ffn_half(xl, ml[:, 6:9], g[4:6], *w["ffn2"], tm_ffn)
    return xl
```

```python
import functools

import numpy as np
import jax
import jax.numpy as jnp
from jax import lax
from jax.experimental import pallas as pl
from jax.experimental.pallas import tpu as pltpu

F32 = jnp.float32
BF16 = jnp.bfloat16

HEAD_DIM = 64
GRID_W = 64
FOURIER_W = 256
FOURIER_GROUPS = 4
ATTN_HEADS = 6
ATTN_KV_HEADS = 2
ATTN_GROUP = ATTN_HEADS // ATTN_KV_HEADS
ATTN_W = ATTN_HEADS * HEAD_DIM
KV_W = ATTN_KV_HEADS * HEAD_DIM
RWKV_HEADS = 6
RWKV_W = RWKV_HEADS * HEAD_DIM
WINDOW = 128
ROPE_BASE = 10000.0
DECAY_LORA = 64
ICLR_LORA = 64
GATE_LORA = 128
CONV_W = 3
N_MOD = 9
NORM_EPS = 1e-6
GN_EPS = 64e-5

LANES = 128
SUBLANES = 8
VMEM_LIMIT = 56 * 1024 * 1024

HEAD_SLOTS = 8
J_PARTS = 4
I_REP = 4
JJ = HEAD_DIM // J_PARTS
I_HI = HEAD_DIM // (SUBLANES * I_REP)
PART_LANES = LANES // J_PARTS
J_W = 4 * LANES
E_W = 4 * LANES
I_W = SUBLANES * LANES
SCAN_T = 64
SCAN_UNROLL = 32
FOURIER_N1 = 64
FOURIER_DIRECT_MAX = 256
FOURIER_K1_BLOCK = 4


def _j_layout():
    idx = np.full((J_W,), -1, np.int64)
    for jh in range(4):
        for jl in range(4):
            for part in range(J_PARTS):
                for h in range(RWKV_HEADS):
                    idx[jh * 128 + jl * 32 + part * 8 + h] = h * HEAD_DIM + part * JJ + jh * 4 + jl
    return idx


def _i_layout_compact():
    idx = np.full((I_W // 2,), -1, np.int64)
    for isub in range(8):
        for q in range(2):
            for h in range(RWKV_HEADS):
                for il in range(I_REP):
                    idx[isub * 64 + q * 32 + h * 4 + il] = h * HEAD_DIM + q * 32 + isub * 4 + il
    return idx


J_IDX = _j_layout()
I_IDX_COMPACT = _i_layout_compact()


def _take_cols(w, idx):
    g = jnp.take(w, jnp.asarray(np.maximum(idx, 0)), axis=-1)
    return g * jnp.asarray((idx >= 0).astype(np.float32))


def _take_rows(w, idx):
    g = jnp.take(w, jnp.asarray(np.maximum(idx, 0)), axis=0)
    return g * jnp.asarray((idx >= 0).astype(np.float32))[:, None]


def _seg_matrices():
    hj = np.arange(LANES) % 8
    hi = (np.arange(LANES) // I_REP) % 8
    jj = (hj[:, None] == hj[None, :]).astype(np.float32)
    ji = (hj[:, None] == hi[None, :]).astype(np.float32)
    ii = (hi[:, None] == hi[None, :]).astype(np.float32)
    return jj, ji, ii


def _cparams(sem, vmem=VMEM_LIMIT):
    return pltpu.CompilerParams(dimension_semantics=sem, vmem_limit_bytes=vmem)


def _const_spec(shape):
    nd = len(shape)
    return pl.BlockSpec(shape, lambda *_: (0,) * nd)


def _resident_spec(shape):
    nd = len(shape)
    return pl.BlockSpec(shape, lambda *_: (0,) * nd, pipeline_mode=pl.Buffered(1))


def _sigmoid(x):
    return 1.0 / (1.0 + jnp.exp(-x))


def _rms(x, g):
    return x * lax.rsqrt(jnp.mean(x * x, axis=-1, keepdims=True) + NORM_EPS) * g


def _dot(a, b):
    return jnp.dot(a, b, preferred_element_type=F32)


def _split_dot(x, m, passes):
    acc = None
    rem = x
    for p in range(passes):
        piece = rem.astype(BF16)
        d = _dot(piece, m)
        acc = d if acc is None else acc + d
        if p + 1 < passes:
            rem = rem - piece.astype(F32)
    return acc


def _ada_kernel(c_ref, w_ref, b_ref, o_ref):
    c = c_ref[...]
    s = c * _sigmoid(c)
    o_ref[0] = _dot(s.astype(BF16), w_ref[0].astype(BF16)) + b_ref[0]


def _ada_mod(cond8, mod_w, mod_b):
    nl, d, nw = mod_w.shape
    tn = nw // 8
    return pl.pallas_call(
        _ada_kernel,
        out_shape=jax.ShapeDtypeStruct((nl, 8, nw), F32),
        grid=(nl, nw // tn),
        in_specs=[pl.BlockSpec((8, d), lambda l, j: (0, 0)),
                  pl.BlockSpec((1, d, tn), lambda l, j: (l, 0, j)),
                  pl.BlockSpec((1, 1, tn), lambda l, j: (l, 0, j))],
        out_specs=pl.BlockSpec((1, 8, tn), lambda l, j: (l, 0, j)),
        compiler_params=_cparams(("arbitrary", "arbitrary")),
        name="ada_mod",
    )(cond8, mod_w, mod_b.reshape(nl, 1, nw))


def _ffn_kernel(x_ref, m_ref, g_ref, wi_ref, wo_ref, o_ref):
    x = x_ref[0]
    m = m_ref[0]
    g = g_ref[...]
    h = _rms(x, g[0:1]) * (1.0 + m[1:2]) + m[0:1]
    hb = h.astype(BF16)
    fp = wo_ref.shape[0]
    gate = _dot(hb, wi_ref[:, :fp])
    up = _dot(hb, wi_ref[:, fp:])
    act = (gate * _sigmoid(gate) * up).astype(BF16)
    y = _dot(act, wo_ref[...])
    o_ref[0] = x + 0.5 * m[2:3] * _rms(y, g[1:2])


def _ffn_half(x, m3, g2, wi, wo, tm):
    b, n, d = x.shape
    fp = wo.shape[0]
    return pl.pallas_call(
        _ffn_kernel,
        out_shape=jax.ShapeDtypeStruct(x.shape, F32),
        grid=(b, n // tm),
        in_specs=[pl.BlockSpec((1, tm, d), lambda i, j: (i, j, 0)),
                  pl.BlockSpec((1, 3, d), lambda i, j: (i, 0, 0)),
                  _const_spec((2, d)),
                  _resident_spec((d, 2 * fp)), _resident_spec((fp, d))],
        out_specs=pl.BlockSpec((1, tm, d), lambda i, j: (i, j, 0)),
        compiler_params=_cparams(("arbitrary", "arbitrary")),
        name="ffn_half",
    )(x, m3, g2, wi, wo)


MIX_COLS = (("f", FOURIER_W), ("q", ATTN_W), ("qr", ATTN_W), ("k", KV_W), ("kr", KV_W),
            ("v", KV_W), ("r", J_W), ("rk", J_W), ("rv", I_W // 2), ("lora", 4 * DECAY_LORA), ("g", GATE_LORA))
MIX_OFF = {}
_o = 0
for _n, _w in MIX_COLS:
    MIX_OFF[_n] = (_o, _o + _w)
    _o += _w
MIX_TOTAL = _o


def _mix_in_kernel(x_ref, m_ref, g_ref, w_ref, cos_ref, sin_ref,
                   f_ref, q_ref, k_ref, v_ref, r_ref, rk_ref, rv_ref, lora_ref, zg_ref):
    x = x_ref[0]
    m = m_ref[0]
    h = _rms(x, g_ref[...]) * (1.0 + m[1:2]) + m[0:1]
    z = _dot(h.astype(BF16), w_ref[...])

    def col(name):
        lo, hi = MIX_OFF[name]
        return z[:, lo:hi]

    cos = cos_ref[...]
    sin = sin_ref[...]
    cos3 = jnp.concatenate([cos, cos, cos], axis=1)
    sin3 = jnp.concatenate([sin, sin, sin], axis=1)

    def with_swapped(x):
        return jnp.concatenate([x, pltpu.roll(x, HEAD_DIM, 1)], axis=1).astype(BF16)

    f_ref[0] = col("f")
    q_ref[0] = ((col("q") * cos3 + col("qr") * sin3) * (HEAD_DIM ** -0.5)).astype(BF16)
    k_ref[0] = with_swapped(col("k") * cos + col("kr") * sin)
    v_ref[0] = with_swapped(col("v"))
    r_ref[0] = col("r")
    rk_ref[0] = col("rk")
    rv_ref[0] = col("rv")
    lora_ref[0] = col("lora")
    zg_ref[0] = col("g")


def _mix_in(x, m3, g1, w_all, cos, sin, tm):
    b, n, d = x.shape
    widths = (FOURIER_W, ATTN_W, 2 * KV_W, 2 * KV_W, J_W, J_W, I_W // 2, 4 * DECAY_LORA, GATE_LORA)
    dtypes = (F32, BF16, BF16, BF16, F32, F32, F32, F32, F32)
    return pl.pallas_call(
        _mix_in_kernel,
        out_shape=[jax.ShapeDtypeStruct((b, n, w), dt) for w, dt in zip(widths, dtypes)],
        grid=(b, n // tm),
        in_specs=[pl.BlockSpec((1, tm, d), lambda i, j: (i, j, 0)),
                  pl.BlockSpec((1, 3, d), lambda i, j: (i, 0, 0)),
                  _const_spec((1, d)),
                  _const_spec((d, MIX_TOTAL)),
                  pl.BlockSpec((tm, LANES), lambda i, j: (j, 0)),
                  pl.BlockSpec((tm, LANES), lambda i, j: (j, 0))],
        out_specs=[pl.BlockSpec((1, tm, w), lambda i, j: (i, j, 0)) for w in widths],
        compiler_params=_cparams(("arbitrary", "arbitrary")),
        name="mix_in",
    )(x, m3, g1, w_all, cos, sin)


def _fourier_stage1_kernel(l_ref, x_ref, o_ref):
    n1, rows, w = x_ref.shape[1:]
    res = _dot(l_ref[...], x_ref[0].reshape(n1 * rows, w).astype(BF16))
    o_ref[0] = res.reshape(2, n1, rows, w)


def _fourier_stage2_kernel(a_ref, mc_ref, ms_ref, cc_ref, sc_ref, o_ref, *, scale):
    w = cc_ref.shape[0]
    for j in range(a_ref.shape[2]):
        ar = a_ref[0, 0, j].astype(BF16)
        ai = a_ref[0, 1, j].astype(BF16)
        mc = mc_ref[j]
        ms = ms_ref[j]
        gr = _dot(mc, ar) + _dot(ms, ai)
        gi = _dot(mc, ai) - _dot(ms, ar)
        o_ref[0, :, j * w:(j + 1) * w] = (_dot(gr.astype(BF16), cc_ref[...]) + _dot(gi.astype(BF16), sc_ref[...])) * scale


def _fourier_small_kernel(z_ref, cn_ref, sn_ref, cc_ref, sc_ref, o_ref, *, scale):
    z = z_ref[0].astype(BF16)
    t1 = _dot(z, cc_ref[...]).astype(BF16)
    t2 = _dot(z, sc_ref[...]).astype(BF16)
    o_ref[0] = (_dot(cn_ref[...], t1) - _dot(sn_ref[...], t2)) * scale


def _channel_dft():
    gw = FOURIER_W // FOURIER_GROUPS
    c = np.arange(FOURIER_W)
    same = (c[:, None] // gw) == (c[None, :] // gw)
    ang = 2.0 * np.pi * ((c[:, None] % gw) * (c[None, :] % gw) % gw) / gw
    return (np.cos(ang) * same).astype(np.float32), (np.sin(ang) * same).astype(np.float32)


def _fourier_mix(z):
    b, n, w = z.shape
    gw = w // FOURIER_GROUPS
    scale = float(1.0 / np.sqrt(n * gw))
    cc, sc = _channel_dft()
    cc = jnp.asarray(cc, BF16)
    sc = jnp.asarray(sc, BF16)
    if n <= FOURIER_DIRECT_MAX:
        p = np.arange(n)
        ang = 2.0 * np.pi * ((p[:, None] * p[None, :]) % n) / n
        return pl.pallas_call(
            functools.partial(_fourier_small_kernel, scale=scale),
            out_shape=jax.ShapeDtypeStruct((b, n, w), F32),
            grid=(b,),
            in_specs=[pl.BlockSpec((1, n, w), lambda i: (i, 0, 0)),
                      _const_spec((n, n)), _const_spec((n, n)), _const_spec((w, w)), _const_spec((w, w))],
            out_specs=pl.BlockSpec((1, n, w), lambda i: (i, 0, 0)),
            compiler_params=_cparams(("arbitrary",)),
            name="fourier_small",
        )(z, jnp.asarray(np.cos(ang), BF16), jnp.asarray(np.sin(ang), BF16), cc, sc)
    n1 = FOURIER_N1
    n2 = n // n1
    k1 = np.arange(n1)
    ang1 = 2.0 * np.pi * ((k1[:, None] * k1[None, :]) % n1) / n1
    lhs1 = np.concatenate([np.cos(ang1), -np.sin(ang1)], axis=0)
    lhs1 = jnp.asarray(np.kron(lhs1, np.eye(SUBLANES)), BF16)
    a = pl.pallas_call(
        _fourier_stage1_kernel,
        out_shape=jax.ShapeDtypeStruct((b, 2, n1, n2, w), F32),
        grid=(b, n2 // SUBLANES),
        in_specs=[_const_spec((2 * n1 * SUBLANES, n1 * SUBLANES)),
                  pl.BlockSpec((1, n1, SUBLANES, w), lambda i, j: (i, 0, j, 0))],
        out_specs=pl.BlockSpec((1, 2, n1, SUBLANES, w), lambda i, j: (i, 0, 0, j, 0)),
        compiler_params=_cparams(("arbitrary", "arbitrary")),
        name="fourier_stage1",
    )(lhs1, z.reshape(b, n1, n2, w))
    k2 = np.arange(n2)
    freq = (k1[:, None, None] + n1 * k2[None, :, None]) * k2[None, None, :]
    ang2 = 2.0 * np.pi * (freq % n) / n
    out = pl.pallas_call(
        functools.partial(_fourier_stage2_kernel, scale=scale),
        out_shape=jax.ShapeDtypeStruct((b, n2, n1 * w), F32),
        grid=(b, n1 // FOURIER_K1_BLOCK),
        in_specs=[pl.BlockSpec((1, 2, FOURIER_K1_BLOCK, n2, w), lambda i, j: (i, 0, j, 0, 0)),
                  pl.BlockSpec((FOURIER_K1_BLOCK, n2, n2), lambda i, j: (j, 0, 0)),
                  pl.BlockSpec((FOURIER_K1_BLOCK, n2, n2), lambda i, j: (j, 0, 0)),
                  _const_spec((w, w)), _const_spec((w, w))],
        out_specs=pl.BlockSpec((1, n2, FOURIER_K1_BLOCK * w), lambda i, j: (i, 0, j)),
        compiler_params=_cparams(("arbitrary", "arbitrary")),
        name="fourier_stage2",
    )(a, jnp.asarray(np.cos(ang2), BF16), jnp.asarray(np.sin(ang2), BF16), cc, sc)
    return out.reshape(b, n, w)


NEG = -1e30


def _attn_kernel(*refs, window):
    if window:
        (sink_ref, q_ref, kp_ref, kc_ref, kn_ref, vp_ref, vc_ref, vn_ref, kx_ref, vx_ref, o_ref, s_ref) = refs
    else:
        (sink_ref, q_ref, kx_ref, vx_ref, o_ref, s_ref) = refs
    i = pl.program_id(1)
    nb = pl.num_programs(1)
    tq = q_ref.shape[1]
    lane = lax.broadcasted_iota(jnp.int32, (1, KV_W), 1)
    low = lane < HEAD_DIM
    if window:
        k_all = jnp.concatenate([kp_ref[0], kc_ref[0], kn_ref[0], kx_ref[0]], axis=0)
        v_all = jnp.concatenate([vp_ref[0], vc_ref[0], vn_ref[0], vx_ref[0]], axis=0)
        row = lax.broadcasted_iota(jnp.int32, (tq, WINDOW), 0)
        col = lax.broadcasted_iota(jnp.int32, (tq, WINDOW), 1)
        prev_ok = jnp.logical_and(col >= row, i > 0)
        next_ok = jnp.logical_and(col <= row, i < nb - 1)
        mask = jnp.concatenate([jnp.where(prev_ok, 0.0, NEG), jnp.zeros((tq, WINDOW), F32),
                                jnp.where(next_ok, 0.0, NEG), jnp.zeros((tq, kx_ref.shape[1]), F32)], axis=1)
    else:
        k_all = kx_ref[0]
        v_all = vx_ref[0]
        mask = None
    zero = jnp.zeros((), BF16)
    one = jnp.ones((), BF16)
    def operands(h):
        slot = h % 2
        swap = slot != h // ATTN_GROUP
        keep = (lane >= HEAD_DIM) if slot else low
        kh = jnp.where(keep, k_all[:, KV_W:] if swap else k_all[:, :KV_W], zero)
        vh = jnp.where(keep, v_all[:, KV_W:] if swap else v_all[:, :KV_W], one)
        return kh, vh

    def scores(h):
        qp = q_ref[0, :, (h // 2) * LANES:(h // 2 + 1) * LANES]
        s = lax.dot_general(qp, operands(h)[0], (((1,), (1,)), ((), ())), preferred_element_type=F32)
        return s if mask is None else s + mask

    for h in range(ATTN_HEADS):
        s_ref[h] = scores(h)
    for pair in range(ATTN_HEADS // 2):
        outs = []
        sinks = []
        for slot in range(2):
            h = 2 * pair + slot
            s = s_ref[h]
            sk = sink_ref[h]
            mx = jnp.maximum(s.max(axis=-1, keepdims=True), sk)
            outs.append(_dot(jnp.exp(s - mx).astype(BF16), operands(h)[1]))
            sinks.append(jnp.exp(sk - mx))
        num = jnp.where(low, outs[0], outs[1])
        den = pltpu.roll(jnp.where(low, outs[1], outs[0]), HEAD_DIM, 1) + jnp.where(low, sinks[0], sinks[1])
        o_ref[0, :, pair * LANES:(pair + 1) * LANES] = num / den


def _attention(q, k, v, kx, vx, sink8, window):
    b, n, _ = q.shape
    c = kx.shape[1]
    tq = WINDOW
    nb = n // tq
    smem = pl.BlockSpec(memory_space=pltpu.SMEM)
    qspec = pl.BlockSpec((1, tq, ATTN_W), lambda i, j: (i, j, 0))
    xspec = pl.BlockSpec((1, c, 2 * KV_W), lambda i, j: (i, 0, 0))
    if window:
        prev = pl.BlockSpec((1, tq, 2 * KV_W), lambda i, j: (i, jnp.maximum(j - 1, 0), 0))
        cur = pl.BlockSpec((1, tq, 2 * KV_W), lambda i, j: (i, j, 0))
        nxt = pl.BlockSpec((1, tq, 2 * KV_W), lambda i, j: (i, jnp.minimum(j + 1, nb - 1), 0))
        in_specs = [smem, qspec, prev, cur, nxt, prev, cur, nxt, xspec, xspec]
        args = (sink8, q, k, k, k, v, v, v, kx, vx)
    else:
        in_specs = [smem, qspec, xspec, xspec]
        args = (sink8, q, kx, vx)
    return pl.pallas_call(
        functools.partial(_attn_kernel, window=window),
        out_shape=jax.ShapeDtypeStruct((b, n, ATTN_W), F32),
        grid=(b, nb),
        in_specs=in_specs,
        out_specs=pl.BlockSpec((1, tq, ATTN_W), lambda i, j: (i, j, 0)),
        scratch_shapes=[pltpu.VMEM((ATTN_HEADS, tq, (3 * tq if window else 0) + c), F32)],
        compiler_params=_cparams(("arbitrary", "arbitrary")),
        name="window_attention" if window else "context_attention",
    )(*args)


def _lane_blocks_to_rows(x):
    return pltpu.einshape("t(kl)->tkl", x, l=LANES)


def _shift_rows(x, prev_row, next_row):
    n = x.shape[0]
    row = lax.broadcasted_iota(jnp.int32, (n, 1), 0)
    xp = jnp.where(row == 0, prev_row, pltpu.roll(x, 1, 0))
    xn = jnp.where(row == n - 1, next_row, pltpu.roll(x, n - 1, 0))
    return xp, xn


def _rwkv_prep_kernel(r_ref, rp_ref, rn_ref, k_ref, kp_ref, kn_ref, v_ref, vp_ref, vn_ref, lora_ref,
                      cwr_ref, cwk_ref, cwv_ref, w0_ref, w2_ref, a0_ref, a2_ref, vec_ref, sjj_ref, sji_ref, tri_ref,
                      vt_ref, bonus_ref, ab_ref, kr_ref, gm_ref):
    i = pl.program_id(1)
    first = (i > 0).astype(F32)
    last = (i < pl.num_programs(1) - 1).astype(F32)

    def conv(x_ref, p_ref, n_ref, cw_ref):
        x = x_ref[0]
        xp, xn = _shift_rows(x, p_ref[0, SUBLANES - 1:SUBLANES, :] * first, n_ref[0, 0:1, :] * last)
        cw = cw_ref[...]
        return xp * cw[0:1] + x * cw[1:2] + xn * cw[2:3]

    r = conv(r_ref, rp_ref, rn_ref, cwr_ref)
    k = conv(k_ref, kp_ref, kn_ref, cwk_ref)
    vh = conv(v_ref, vp_ref, vn_ref, cwv_ref)
    low = lax.broadcasted_iota(jnp.int32, (1, LANES), 1) < LANES // 2
    blocks = []
    for c in range(vh.shape[1] // LANES):
        xc = vh[:, c * LANES:(c + 1) * LANES]
        xr = pltpu.roll(xc, LANES // 2, 1)
        blocks += [jnp.where(low, xc, xr), jnp.where(low, xr, xc)]
    v = jnp.concatenate(blocks, axis=1)
    vec = vec_ref[...]
    sjj = sjj_ref[...]
    sji = sji_ref[...]

    def fold(x):
        return (x[:, 0:LANES] + x[:, LANES:2 * LANES]) + (x[:, 2 * LANES:3 * LANES] + x[:, 3 * LANES:4 * LANES])

    kk = k * vec[0:1]
    ss = _split_dot(fold(kk * kk), sjj, 2)
    inv = lax.rsqrt(jnp.maximum(ss, 1e-24))
    kk = kk * jnp.concatenate([inv, inv, inv, inv], axis=1)
    rk = _split_dot(fold(r * k * vec[2:3]), sji, 2)
    vt_ref[0] = _lane_blocks_to_rows(v)
    bonus_ref[0] = vh * jnp.concatenate([rk] * (vh.shape[1] // LANES), axis=1)
    lora = lora_ref[0]
    zw = jnp.tanh(lora[:, :2 * DECAY_LORA])
    za = lora[:, 2 * DECAY_LORA:]
    zw_hi = zw.astype(BF16)
    zw_lo = (zw - zw_hi.astype(F32)).astype(BF16)
    za_hi = za.astype(BF16)
    za_lo = (za - za_hi.astype(F32)).astype(BF16)

    def lora(x_hi, x_lo, w_ref, d):
        return _dot(x_hi, w_ref[0, d]) + (_dot(x_lo, w_ref[0, d]) + _dot(x_hi, w_ref[1, d]))

    for d in range(2):
        wpre = w0_ref[d:d + 1, :] + lora(zw_hi, zw_lo, w2_ref, d)
        apre = a0_ref[d:d + 1, :] + lora(za_hi, za_lo, a2_ref, d)
        a = _sigmoid(apre)
        log_decay = -np.float32(np.exp(-0.5)) * _sigmoid(wpre)
        tri = tri_ref[d]
        hi = log_decay.astype(BF16)
        rest = log_decay - hi.astype(F32)
        mid = rest.astype(BF16)
        low = (rest - mid.astype(F32)).astype(BF16)
        log_g = _dot(tri, hi) + _dot(tri, mid) + _dot(tri, low)
        g = jnp.exp(log_g)
        g_inv = jnp.exp(-log_g)
        a_t = -kk * jnp.exp(log_g - log_decay)
        b_t = kk * a * g_inv
        k_t = k * (1.0 + (a - 1.0) * vec[1:2]) * g_inv
        r_t = r * g
        ab_ref[d, 0] = _lane_blocks_to_rows(jnp.concatenate([a_t, b_t], axis=1))
        kr_ref[d, 0] = _lane_blocks_to_rows(jnp.concatenate([k_t, r_t], axis=1))
        gm_ref[d, 0] = g


def _rwkv_prep(r, k, v, lora, cwr, cwk, cwv, w0, w2, a0, a2, vec, sjj, sji, tm):
    b, n, _ = r.shape
    nh = n // SUBLANES

    def tile(w):
        return pl.BlockSpec((1, tm, w), lambda i, j: (i, j, 0))

    def halo_prev(w):
        return pl.BlockSpec((1, SUBLANES, w), lambda i, j: (i, jnp.maximum(j * (tm // SUBLANES) - 1, 0), 0))

    def halo_next(w):
        return pl.BlockSpec((1, SUBLANES, w), lambda i, j: (i, jnp.minimum((j + 1) * (tm // SUBLANES), nh - 1), 0))

    def dir_tile(w):
        return pl.BlockSpec((2, 1, tm, w), lambda i, j: (0, i, j, 0))

    t = np.arange(tm)
    same = (t[:, None] // SCAN_T) == (t[None, :] // SCAN_T)
    tri = jnp.asarray(np.stack([same & (t[None, :] <= t[:, None]), same & (t[None, :] >= t[:, None])]), BF16)
    def row_tiles(lead):
        return jax.ShapeDtypeStruct(lead + (n, SUBLANES, LANES), F32)

    def dir_row_tile():
        return pl.BlockSpec((2, 1, tm, SUBLANES, LANES), lambda i, j: (0, i, j, 0, 0))

    out_shape = [row_tiles((b,)), jax.ShapeDtypeStruct((b, n, I_W // 2), F32), row_tiles((2, b)), row_tiles((2, b)),
                 jax.ShapeDtypeStruct((2, b, n, J_W), F32)]
    row_tile = pl.BlockSpec((1, tm, SUBLANES, LANES), lambda i, j: (i, j, 0, 0))
    return pl.pallas_call(
        _rwkv_prep_kernel,
        out_shape=out_shape,
        grid=(b, n // tm),
        in_specs=[tile(J_W), halo_prev(J_W), halo_next(J_W),
                  tile(J_W), halo_prev(J_W), halo_next(J_W),
                  tile(I_W // 2), halo_prev(I_W // 2), halo_next(I_W // 2),
                  tile(4 * DECAY_LORA),
                  _const_spec((CONV_W, J_W)), _const_spec((CONV_W, J_W)), _const_spec((CONV_W, I_W // 2)),
                  _const_spec((2, J_W)), _const_spec((2, 2, 2 * DECAY_LORA, J_W)),
                  _const_spec((2, J_W)), _const_spec((2, 2, 2 * ICLR_LORA, J_W)),
                  _const_spec((3, J_W)), _const_spec((LANES, LANES)), _const_spec((LANES, LANES)),
                  _const_spec((2, tm, tm))],
        out_specs=[row_tile, tile(I_W // 2), dir_row_tile(), dir_row_tile(), dir_tile(J_W)],
        compiler_params=_cparams(("arbitrary", "arbitrary")),
        name="rwkv_prep",
    )(r, r, r, k, k, k, v, v, v, lora, cwr, cwk, cwv, w0, w2, a0, a2, vec, sjj, sji, tri)


N_ROWVEC = 4
N_PAIRS = N_ROWVEC // 2
N_DIR_REFS = N_PAIRS + 2
SCAN_PASSES = 1
DECAY_PASSES = 3
N_SG = 3
SG_PARTS = (((0, 0, 6, 0), (1, 0, 2, 6)),
            ((1, 2, 4, 0), (2, 0, 4, 4)),
            ((2, 4, 2, 0), (3, 0, 6, 2)))
SG_MIXED = 1


def _sg_expand_matrices():
    m = np.zeros((N_SG, 2 * LANES, E_W), np.float32)
    for sg, parts in enumerate(SG_PARTS):
        for half, (_, h0, nh, slot0) in enumerate(parts):
            for jl in range(4):
                for part in range(J_PARTS):
                    for h in range(h0, h0 + nh):
                        for rep in range(I_REP):
                            m[sg, half * LANES + jl * 32 + part * 8 + h,
                              jl * 128 + part * 32 + (slot0 + h - h0) * I_REP + rep] = 1.0
    return m


def _scan_kernel(*refs, t_chunk, n_batch):
    ins = refs[:2 * N_DIR_REFS]
    r2_ref, s0_ref = refs[2 * N_DIR_REFS:2 * N_DIR_REFS + 2]
    y_refs = refs[2 * N_DIR_REFS + 2:2 * N_DIR_REFS + 4]
    sout_ref = refs[2 * N_DIR_REFS + 4]
    e_ref, s_ref, ybuf_ref, yrev_ref = refs[2 * N_DIR_REFS + 5:]
    step_id = pl.program_id(0)
    assert n_batch == 2
    last = t_chunk - 1

    @pl.when(step_id == 0)
    def _():
        s_ref[...] = s0_ref[...]

    def src(x, s):
        return ins[(s // n_batch) * N_DIR_REFS + x]

    def expand(sg, first, second, passes):
        lhs = jnp.concatenate([first, second], axis=2)
        lhs = lhs.reshape(lhs.shape[0] * SUBLANES, 2 * LANES)
        return _split_dot(lhs, r2_ref[sg], passes).reshape(first.shape[0], SUBLANES, E_W)

    for sg, parts in enumerate(SG_PARTS):
        for pair in range(N_PAIRS):
            halves = []
            for s, _, _, _ in parts:
                ref = src(pair, s)
                if sg == SG_MIXED and s >= n_batch:
                    halves.append(jnp.concatenate([ref[0, s % n_batch, t:t + 1] for t in reversed(range(t_chunk))],
                                                  axis=0))
                else:
                    halves.append(ref[0, s % n_batch])
            e_ref[sg, pair] = expand(sg, halves[0], halves[1], SCAN_PASSES)

    lane = lax.broadcasted_iota(jnp.int32, (SUBLANES, LANES), 1)
    q_even = (lane // PART_LANES) % 2 == 0
    slot = (lane // I_REP) % HEAD_SLOTS

    def shifted(x, slots):
        return x if slots == 0 else pltpu.roll(x, (slots * I_REP) % LANES, 1)

    def allparts(p):
        assert J_PARTS == 4
        return ((p + pltpu.roll(p, 2 * PART_LANES, 1))
                + (pltpu.roll(p, PART_LANES, 1) + pltpu.roll(p, 3 * PART_LANES, 1)))

    def step(s, sg):
        (sa_src, ha, _, slot_a), (sb_src, hb, _, slot_b) = SG_PARTS[sg]
        ta = s if sa_src < n_batch else last - s
        tb = s if sb_src < n_batch else last - s
        te = s if sg == SG_MIXED else ta
        vp = jnp.where(slot < slot_b, shifted(src(N_PAIRS + 1, sa_src)[sa_src % n_batch, ta], slot_a - ha),
                       shifted(src(N_PAIRS + 1, sb_src)[sb_src % n_batch, tb], slot_b - hb))
        vr = pltpu.roll(vp, PART_LANES, 1)
        v = (jnp.where(q_even, vp, vr), jnp.where(q_even, vr, vp))

        def row(x, jj):
            r0 = (x % 2) * 4 + jj // 4
            return e_ref[sg, x // 2, te, r0:r0 + 1, (jj % 4) * LANES:(jj % 4 + 1) * LANES]

        def add(acc, ih, jj, term):
            k = (ih, jj % 4)
            acc[k] = term if k not in acc else acc[k] + term

        def total(acc, ih):
            return allparts((acc[ih, 0] + acc[ih, 1]) + (acc[ih, 2] + acc[ih, 3]))

        acc = {}
        for jj in range(JJ):
            a = row(0, jj)
            for ih in range(I_HI):
                add(acc, ih, jj, s_ref[sg, ih, jj] * a)
        sa = [total(acc, ih) for ih in range(I_HI)]
        acc = {}
        for jj in range(JJ):
            b, k, rr = row(1, jj), row(2, jj), row(3, jj)
            for ih in range(I_HI):
                new = (s_ref[sg, ih, jj] + v[ih] * k) + sa[ih] * b
                s_ref[sg, ih, jj] = new
                add(acc, ih, jj, new * rr)
        y = [total(acc, ih) for ih in range(I_HI)]
        packed = jnp.where(q_even, y[0], y[1])
        ybuf_ref[sg, te] = packed
        if sg == SG_MIXED:
            yrev_ref[last - s] = packed

    def one_token(s, carry):
        for sg in range(N_SG):
            step(s, sg)
        return carry

    lax.fori_loop(0, t_chunk, one_token, 0, unroll=SCAN_UNROLL)

    for s in range(2 * n_batch):
        out = jnp.zeros((t_chunk, SUBLANES, LANES), F32)
        for sg, parts in enumerate(SG_PARTS):
            for s2, h0, nh, slot0 in parts:
                if s2 == s:
                    buf = yrev_ref[...] if (sg == SG_MIXED and s >= n_batch) else ybuf_ref[sg]
                    if h0 != slot0:
                        buf = pltpu.roll(buf, ((h0 - slot0) * I_REP) % LANES, 2)
                    out = jnp.where(jnp.logical_and(slot >= h0, slot < h0 + nh)[None], buf, out)
        y_refs[s // n_batch][s % n_batch] = out

    for sg, parts in enumerate(SG_PARTS):
        halves = []
        toks = []
        for s, _, _, _ in parts:
            tile0, tok = (t_chunk - SUBLANES, SUBLANES - 1) if s < n_batch else (0, 0)
            g8 = src(N_PAIRS, s)[0, s % n_batch, tile0:tile0 + SUBLANES, :]
            halves.append(_lane_blocks_to_rows(jnp.concatenate([g8, g8], axis=1)))
            toks.append(tok)
        res = expand(sg, halves[0], halves[1], DECAY_PASSES)
        first_rows, second_rows = res[toks[0]], res[toks[1]]
        for jj in range(JJ):
            r0 = jj // 4
            blk = slice((jj % 4) * LANES, (jj % 4 + 1) * LANES)
            scale = jnp.where(slot[0:1] < parts[1][3], first_rows[r0:r0 + 1, blk], second_rows[r0:r0 + 1, blk])
            for ih in range(I_HI):
                s_ref[sg, ih, jj] = s_ref[sg, ih, jj] * scale

    @pl.when(step_id == pl.num_programs(0) - 1)
    def _():
        sout_ref[...] = s_ref[...]


def _rwkv_scan(vt, ab, kr, gm, state0, r2):
    b, n = vt.shape[:2]
    t = SCAN_T
    nc = n // t
    in_specs = []
    args = []
    out_specs = []
    for d in range(2):
        def chunk(i, d=d):
            return i if d == 0 else nc - 1 - i

        pair_spec = pl.BlockSpec((1, b, t, SUBLANES, LANES), lambda i, d=d, chunk=chunk: (d, 0, chunk(i), 0, 0))
        gm_spec = pl.BlockSpec((1, b, t, J_W), lambda i, d=d, chunk=chunk: (d, 0, chunk(i), 0))
        tile_spec = pl.BlockSpec((b, t, SUBLANES, LANES), lambda i, chunk=chunk: (0, chunk(i), 0, 0))
        in_specs += [pair_spec, pair_spec, gm_spec, tile_spec]
        args += [ab, kr, gm, vt]
        out_specs.append(tile_spec)
    state_shape = (N_SG, I_HI, JJ, SUBLANES, LANES)
    in_specs += [_const_spec((N_SG, 2 * LANES, E_W)), _const_spec(state_shape)]
    args += [r2, state0]
    out_specs.append(_const_spec(state_shape))
    y_shape = jax.ShapeDtypeStruct((b, n, SUBLANES, LANES), F32)
    return pl.pallas_call(
        functools.partial(_scan_kernel, t_chunk=t, n_batch=b),
        out_shape=[y_shape, y_shape, jax.ShapeDtypeStruct(state_shape, F32)],
        grid=(nc,),
        in_specs=in_specs,
        out_specs=out_specs,
        scratch_shapes=[pltpu.VMEM((N_SG, N_PAIRS, t, SUBLANES, E_W), F32),
                        pltpu.VMEM(state_shape, F32),
                        pltpu.VMEM((N_SG, t, SUBLANES, LANES), F32),
                        pltpu.VMEM((t, SUBLANES, LANES), F32)],
        compiler_params=_cparams(("arbitrary",)),
        name="rwkv_scan",
    )(*args)


def _mix_out_kernel(x_ref, m_ref, g_ref, f_ref, a_ref, yf_ref, yb_ref, bonus_ref, zg_ref,
                    g2_ref, ln_ref, sii_ref, wf_ref, wa_ref, wr_ref, o_ref):
    x = x_ref[0]
    yy = pltpu.einshape("tkl->t(kl)", yf_ref[0] + yb_ref[0])
    low = lax.broadcasted_iota(jnp.int32, (1, LANES), 1) < LANES // 2
    y = jnp.concatenate([jnp.where(low, yy[:, 2 * c * LANES:(2 * c + 1) * LANES],
                                   pltpu.roll(yy[:, (2 * c + 1) * LANES:(2 * c + 2) * LANES], LANES // 2, 1))
                         for c in range(I_W // (2 * LANES))], axis=1)
    sii = sii_ref[...]
    nblk = I_W // (2 * LANES)
    cnt = float(HEAD_DIM)

    def fold(z):
        acc = z[:, 0:LANES]
        for c in range(1, nblk):
            acc = acc + z[:, c * LANES:(c + 1) * LANES]
        return acc

    def spread(z):
        return jnp.concatenate([z] * nblk, axis=1)

    mu = _split_dot(fold(y), sii, 2) * (1.0 / cnt)
    dlt = y - spread(mu)
    var = _split_dot(fold(dlt * dlt), sii, 2) * (1.0 / cnt)
    ln = ln_ref[...]
    yn = dlt * spread(lax.rsqrt(var + GN_EPS)) * ln[0:1] + ln[1:2]
    gate = _dot(_sigmoid(zg_ref[0]).astype(BF16), g2_ref[...])
    rw = ((yn + bonus_ref[0]) * gate).astype(BF16)
    o = _dot(f_ref[0].astype(BF16), wf_ref[...]) + _dot(a_ref[0].astype(BF16), wa_ref[...]) + _dot(rw, wr_ref[...])
    o_ref[0] = x + m_ref[0] * _rms(o, g_ref[...])


def _mix_out(x, gate_row, g3, fo, ao, y_f, y_b, bonus, zg, g2p, ln, sii, wf, wa, wr, tm):
    b, n, d = x.shape

    def tile(w):
        return pl.BlockSpec((1, tm, w), lambda i, j: (i, j, 0))

    return pl.pallas_call(
        _mix_out_kernel,
        out_shape=jax.ShapeDtypeStruct(x.shape, F32),
        grid=(b, n // tm),
        in_specs=[tile(d), pl.BlockSpec((1, 1, d), lambda i, j: (i, 0, 0)), _const_spec((1, d)),
                  tile(FOURIER_W), tile(ATTN_W),
                  pl.BlockSpec((1, tm, SUBLANES, LANES), lambda i, j: (i, j, 0, 0)),
                  pl.BlockSpec((1, tm, SUBLANES, LANES), lambda i, j: (i, j, 0, 0)),
                  tile(I_W // 2), tile(GATE_LORA),
                  _const_spec((GATE_LORA, I_W // 2)), _const_spec((2, I_W // 2)), _const_spec((LANES, LANES)),
                  _const_spec((FOURIER_W, d)), _const_spec((ATTN_W, d)), _const_spec((I_W // 2, d))],
        out_specs=tile(d),
        compiler_params=_cparams(("arbitrary", "arbitrary")),
        name="mix_out",
    )(x, gate_row, g3, fo, ao, y_f, y_b, bonus, zg, g2p, ln, sii, wf, wa, wr)


def _rope_tables(n_tokens):
    rows_n = n_tokens // GRID_W
    row = jnp.repeat(jnp.arange(rows_n), GRID_W).astype(F32)
    colp = jnp.tile(jnp.arange(GRID_W), rows_n).astype(F32)
    n_freq = HEAD_DIM // 4
    inv = ROPE_BASE ** (-jnp.arange(n_freq, dtype=F32) / n_freq)
    ang = jnp.concatenate([row[:, None] * inv, colp[:, None] * inv], axis=-1)
    cos = jnp.tile(jnp.cos(ang), (1, 4))
    sin = jnp.tile(jnp.sin(ang), (1, 4))
    return cos, sin


def _rot_cols(w):
    d, n = w.shape
    w4 = w.reshape(d, n // HEAD_DIM, 2, HEAD_DIM // 2)
    return jnp.stack([-w4[:, :, 1], w4[:, :, 0]], axis=2).reshape(d, n)


def _layer_weights(li, p):
    f = p["ffn1_wo"].shape[1]
    fp = -(-f // (2 * LANES)) * (2 * LANES)
    out = {}
    for name in ("ffn1", "ffn2"):
        wi = p[name + "_wi"][li]
        wo = p[name + "_wo"][li]
        gap = jnp.zeros((wi.shape[0], fp - f), BF16)
        out[name] = (jnp.concatenate([wi[:, :f].astype(BF16), gap, wi[:, f:].astype(BF16), gap], axis=1),
                     jnp.pad(wo, ((0, fp - f), (0, 0))).astype(BF16))
    w_in = p["mix_w_in"][li]
    offs = np.cumsum([0, FOURIER_W, ATTN_W, KV_W, KV_W, RWKV_W, RWKV_W, RWKV_W,
                      DECAY_LORA, DECAY_LORA, ICLR_LORA, ICLR_LORA, GATE_LORA])
    part = [w_in[:, offs[i]:offs[i + 1]] for i in range(12)]
    wf, wq, wk, wv, wrr, wrk, wrv = part[:7]
    cols = {"f": wf, "q": wq, "qr": _rot_cols(wq), "k": wk, "kr": _rot_cols(wk), "v": wv,
            "r": _take_cols(wrr, J_IDX), "rk": _take_cols(wrk, J_IDX), "rv": _take_cols(wrv, I_IDX_COMPACT),
            "lora": jnp.concatenate(part[7:11], axis=1), "g": part[11]}
    out["w_all"] = jnp.concatenate([cols[n] for n, _ in MIX_COLS], axis=1).astype(BF16)
    conv = p["rwkv_conv"][li]
    out["cwr"] = _take_cols(conv[:, :RWKV_W], J_IDX)
    out["cwk"] = _take_cols(conv[:, RWKV_W:2 * RWKV_W], J_IDX)
    out["cwv"] = _take_cols(conv[:, 2 * RWKV_W:], I_IDX_COMPACT)
    def per_direction(w2):
        w2 = _take_cols(w2, J_IDX)
        z = jnp.zeros_like(w2[0])
        w2 = jnp.stack([jnp.concatenate([w2[0], z], axis=0), jnp.concatenate([z, w2[1]], axis=0)])
        hi = w2.astype(BF16)
        return jnp.stack([hi, (w2 - hi.astype(F32)).astype(BF16)])

    out["w0"] = _take_cols(p["rwkv_w0"][li], J_IDX)
    out["w2"] = per_direction(p["rwkv_w2"][li])
    out["a0"] = _take_cols(p["rwkv_a0"][li], J_IDX)
    out["a2"] = per_direction(p["rwkv_a2"][li])
    out["vec"] = _take_cols(jnp.stack([p["rwkv_k_k"][li], p["rwkv_k_a"][li], p["rwkv_r_k"][li].reshape(-1)]), J_IDX)
    out["g2"] = _take_cols(p["rwkv_g2"][li], I_IDX_COMPACT).astype(BF16)
    out["ln"] = _take_cols(jnp.stack([p["rwkv_ln_g"][li], p["rwkv_ln_b"][li]]), I_IDX_COMPACT)
    w_out = p["mix_w_out"][li]
    out["wo_f"] = w_out[:FOURIER_W].astype(BF16)
    out["wo_a"] = w_out[FOURIER_W:FOURIER_W + ATTN_W].astype(BF16)
    out["wo_r"] = _take_rows(w_out[FOURIER_W + ATTN_W:], I_IDX_COMPACT).astype(BF16)
    sink = p["attn_sink"][li]
    out["sink"] = jnp.concatenate([sink, jnp.zeros((8 - ATTN_HEADS,), F32)])
    return out


def kernel(x, c, ctx, c_ctx, mod_w, mod_b, norm_g, ffn1_wi, ffn1_wo, mix_w_in, mix_w_out, attn_sink,
           rwkv_conv, rwkv_w0, rwkv_w2, rwkv_a0, rwkv_a2, rwkv_g2, rwkv_k_k, rwkv_k_a, rwkv_r_k,
           rwkv_ln_g, rwkv_ln_b, ffn2_wi, ffn2_wo):
    p = dict(ffn1_wi=ffn1_wi, ffn1_wo=ffn1_wo, ffn2_wi=ffn2_wi, ffn2_wo=ffn2_wo, mix_w_in=mix_w_in,
             mix_w_out=mix_w_out, attn_sink=attn_sink, rwkv_conv=rwkv_conv, rwkv_w0=rwkv_w0, rwkv_w2=rwkv_w2,
             rwkv_a0=rwkv_a0, rwkv_a2=rwkv_a2, rwkv_g2=rwkv_g2, rwkv_k_k=rwkv_k_k, rwkv_k_a=rwkv_k_a,
             rwkv_r_k=rwkv_r_k, rwkv_ln_g=rwkv_ln_g, rwkv_ln_b=rwkv_ln_b)
    b, s, d = x.shape
    n_c = ctx.shape[1]
    depth = mod_w.shape[0]
    assert b + 1 <= 8 and s % WINDOW == 0 and n_c % WINDOW == 0 and WINDOW % SCAN_T == 0
    tm_l = 256
    tm_ffn = 512 if s % 512 == 0 else tm_l
    tm_c = min(256, n_c)

    cond8 = jnp.zeros((8, d), F32).at[:b].set(c).at[b].set(c_ctx)
    mod = _ada_mod(cond8, mod_w, mod_b).reshape(depth, 8, N_MOD, d)
    cos_l, sin_l = _rope_tables(s)
    cos_c = jnp.ones((n_c, LANES), F32)
    sin_c = jnp.zeros((n_c, LANES), F32)
    r2 = jnp.asarray(_sg_expand_matrices(), BF16)
    sjj, sji, sii = (jnp.asarray(m, BF16) for m in _seg_matrices())
    zero_state = jnp.zeros((N_SG, I_HI, JJ, SUBLANES, LANES), F32)

    xl, xc = x, ctx
    for li in range(depth):
        need_ctx_out = li < depth - 1
        w = _layer_weights(li, p)
        ml = mod[li, :b]
        mc = jnp.broadcast_to(mod[li, b:b + 1], (b, N_MOD, d))
        g = norm_g[li]
        xl = _ffn_half(xl, ml[:, 0:3], g[0:2], *w["ffn1"], tm_ffn)
        xc = _ffn_half(xc, mc[:, 0:3], g[0:2], *w["ffn1"], tm_c)

        def mixer_in(xx, mm, cos, sin, tm):
            return _mix_in(xx, mm[:, 3:6], g[2:3], w["w_all"], cos, sin, tm)

        fl, ql, kl, vl, rl, rkl, rvl, loral, zgl = mixer_in(xl, ml, cos_l, sin_l, tm_l)
        fc, qc, kc, vc, rc_, rkc, rvc, lorac, zgc = mixer_in(xc, mc, cos_c, sin_c, tm_c)

        attn_l = _attention(ql, kl, vl, kc, vc, w["sink"], True)

        def prep(r_, k_, v_, lora_, tm):
            return _rwkv_prep(r_, k_, v_, lora_, w["cwr"], w["cwk"], w["cwv"], w["w0"], w["w2"],
                              w["a0"], w["a2"], w["vec"], sjj, sji, tm)

        pc = prep(rc_, rkc, rvc, lorac, tm_c)
        plat = prep(rl, rkl, rvl, loral, tm_l)
        ycf, ycb, state_c = _rwkv_scan(pc[0], *pc[2:], zero_state, r2)
        ylf, ylb, _ = _rwkv_scan(plat[0], *plat[2:], state_c, r2)

        def mixer_out(xx, mm, fo, ao, y_f, y_b, bonus, zg, tm):
            return _mix_out(xx, mm[:, 5:6], g[3:4], fo, ao, y_f, y_b, bonus, zg, w["g2"], w["ln"], sii,
                            w["wo_f"], w["wo_a"], w["wo_r"], tm)

        xl = mixer_out(xl, ml, _fourier_mix(fl), attn_l, ylf, ylb, plat[1], zgl, tm_l)
        if need_ctx_out:
            attn_c = _attention(qc, None, None, kc, vc, w["sink"], False)
            xc = mixer_out(xc, mc, _fourier_mix(fc), attn_c, ycf, ycb, pc[1], zgc, tm_c)
            xc = _ffn_half(xc, mc[:, 6:9], g[4:6], *w["ffn2"], tm_c)
        xl = _ffn_half(xl, ml[:, 6:9], g[4:6], *w["ffn2"], tm_ffn)
    return xl
```

```python
import functools

import numpy as np
import jax
import jax.numpy as jnp
from jax import lax
from jax.experimental import pallas as pl
from jax.experimental.pallas import tpu as pltpu

F32 = jnp.float32
BF16 = jnp.bfloat16

HEAD_DIM = 64
GRID_W = 64
FOURIER_W = 256
FOURIER_GROUPS = 4
ATTN_HEADS = 6
ATTN_KV_HEADS = 2
ATTN_GROUP = ATTN_HEADS // ATTN_KV_HEADS
ATTN_W = ATTN_HEADS * HEAD_DIM
KV_W = ATTN_KV_HEADS * HEAD_DIM
RWKV_HEADS = 6
RWKV_W = RWKV_HEADS * HEAD_DIM
WINDOW = 128
ROPE_BASE = 10000.0
DECAY_LORA = 64
ICLR_LORA = 64
GATE_LORA = 128
CONV_W = 3
N_MOD = 9
NORM_EPS = 1e-6
GN_EPS = 64e-5

LANES = 128
SUBLANES = 8
VMEM_LIMIT = 56 * 1024 * 1024

HEAD_SLOTS = 8
J_PARTS = 4
I_REP = 4
JJ = HEAD_DIM // J_PARTS
I_HI = HEAD_DIM // (SUBLANES * I_REP)
PART_LANES = LANES // J_PARTS
J_W = 4 * LANES
E_W = 4 * LANES
I_W = SUBLANES * LANES
SCAN_T = 64
SCAN_UNROLL = 32
FOURIER_N1 = 64
FOURIER_DIRECT_MAX = 256
FOURIER_K1_BLOCK = SUBLANES


def _j_layout():
    idx = np.full((J_W,), -1, np.int64)
    for jh in range(4):
        for jl in range(4):
            for part in range(J_PARTS):
                for h in range(RWKV_HEADS):
                    idx[jh * 128 + jl * 32 + part * 8 + h] = h * HEAD_DIM + part * JJ + jh * 4 + jl
    return idx


def _i_layout_compact():
    idx = np.full((I_W // 2,), -1, np.int64)
    for isub in range(8):
        for q in range(2):
            for h in range(RWKV_HEADS):
                for il in range(I_REP):
                    idx[isub * 64 + q * 32 + h * 4 + il] = h * HEAD_DIM + q * 32 + isub * 4 + il
    return idx


J_IDX = _j_layout()
I_IDX_COMPACT = _i_layout_compact()


def _take_cols(w, idx):
    g = jnp.take(w, jnp.asarray(np.maximum(idx, 0)), axis=-1)
    return g * jnp.asarray((idx >= 0).astype(np.float32))


def _take_rows(w, idx):
    g = jnp.take(w, jnp.asarray(np.maximum(idx, 0)), axis=0)
    return g * jnp.asarray((idx >= 0).astype(np.float32))[:, None]


def _seg_matrices():
    hj = np.arange(LANES) % 8
    hi = (np.arange(LANES) // I_REP) % 8
    jj = (hj[:, None] == hj[None, :]).astype(np.float32)
    ji = (hj[:, None] == hi[None, :]).astype(np.float32)
    ii = (hi[:, None] == hi[None, :]).astype(np.float32)
    return jj, ji, ii


def _cparams(sem, vmem=VMEM_LIMIT):
    return pltpu.CompilerParams(dimension_semantics=sem, vmem_limit_bytes=vmem)


def _const_spec(shape):
    nd = len(shape)
    return pl.BlockSpec(shape, lambda *_: (0,) * nd)


def _resident_spec(shape):
    nd = len(shape)
    return pl.BlockSpec(shape, lambda *_: (0,) * nd, pipeline_mode=pl.Buffered(1))


def _sigmoid(x):
    return 1.0 / (1.0 + jnp.exp(-x))


def _rms(x, g):
    return x * lax.rsqrt(jnp.mean(x * x, axis=-1, keepdims=True) + NORM_EPS) * g


def _dot(a, b):
    return jnp.dot(a, b, preferred_element_type=F32)


def _split_dot(x, m, passes):
    acc = None
    rem = x
    for p in range(passes):
        piece = rem.astype(BF16)
        d = _dot(piece, m)
        acc = d if acc is None else acc + d
        if p + 1 < passes:
            rem = rem - piece.astype(F32)
    return acc


def _ada_kernel(c_ref, w_ref, b_ref, o_ref):
    c = c_ref[...]
    s = c * _sigmoid(c)
    o_ref[0] = _dot(s.astype(BF16), w_ref[0].astype(BF16)) + b_ref[0]


def _ada_mod(cond8, mod_w, mod_b):
    nl, d, nw = mod_w.shape
    tn = nw // 8
    return pl.pallas_call(
        _ada_kernel,
        out_shape=jax.ShapeDtypeStruct((nl, 8, nw), F32),
        grid=(nl, nw // tn),
        in_specs=[pl.BlockSpec((8, d), lambda l, j: (0, 0)),
                  pl.BlockSpec((1, d, tn), lambda l, j: (l, 0, j)),
                  pl.BlockSpec((1, 1, tn), lambda l, j: (l, 0, j))],
        out_specs=pl.BlockSpec((1, 8, tn), lambda l, j: (l, 0, j)),
        compiler_params=_cparams(("arbitrary", "arbitrary")),
        name="ada_mod",
    )(cond8, mod_w, mod_b.reshape(nl, 1, nw))


def _ffn_kernel(x_ref, m_ref, g_ref, wi_ref, wo_ref, o_ref):
    x = x_ref[0]
    m = m_ref[0]
    g = g_ref[...]
    h = _rms(x, g[0:1]) * (1.0 + m[1:2]) + m[0:1]
    hb = h.astype(BF16)
    fp = wo_ref.shape[0]
    gate = _dot(hb, wi_ref[:, :fp])
    up = _dot(hb, wi_ref[:, fp:])
    act = (gate * _sigmoid(gate) * up).astype(BF16)
    y = _dot(act, wo_ref[...])
    o_ref[0] = x + 0.5 * m[2:3] * _rms(y, g[1:2])


def _ffn_half(x, m3, g2, wi, wo, tm):
    b, n, d = x.shape
    fp = wo.shape[0]
    return pl.pallas_call(
        _ffn_kernel,
        out_shape=jax.ShapeDtypeStruct(x.shape, F32),
        grid=(b, n // tm),
        in_specs=[pl.BlockSpec((1, tm, d), lambda i, j: (i, j, 0)),
                  pl.BlockSpec((1, 3, d), lambda i, j: (i, 0, 0)),
                  _const_spec((2, d)),
                  _resident_spec((d, 2 * fp)), _resident_spec((fp, d))],
        out_specs=pl.BlockSpec((1, tm, d), lambda i, j: (i, j, 0)),
        compiler_params=_cparams(("arbitrary", "arbitrary")),
        name="ffn_half",
    )(x, m3, g2, wi, wo)


MIX_COLS = (("f", FOURIER_W), ("q", ATTN_W), ("qr", ATTN_W), ("k", KV_W), ("kr", KV_W),
            ("v", KV_W), ("r", J_W), ("rk", J_W), ("rv", I_W // 2), ("lora", 4 * DECAY_LORA), ("g", GATE_LORA))
MIX_OFF = {}
_o = 0
for _n, _w in MIX_COLS:
    MIX_OFF[_n] = (_o, _o + _w)
    _o += _w
MIX_TOTAL = _o


def _mix_in_kernel(x_ref, m_ref, g_ref, w_ref, cos_ref, sin_ref,
                   f_ref, q_ref, k_ref, v_ref, r_ref, rk_ref, rv_ref, lora_ref, zg_ref):
    x = x_ref[0]
    m = m_ref[0]
    h = _rms(x, g_ref[...]) * (1.0 + m[1:2]) + m[0:1]
    z = _dot(h.astype(BF16), w_ref[...])

    def col(name):
        lo, hi = MIX_OFF[name]
        return z[:, lo:hi]

    cos = cos_ref[...]
    sin = sin_ref[...]
    cos3 = jnp.concatenate([cos, cos, cos], axis=1)
    sin3 = jnp.concatenate([sin, sin, sin], axis=1)

    def with_swapped(x):
        return jnp.concatenate([x, pltpu.roll(x, HEAD_DIM, 1)], axis=1).astype(BF16)

    f_ref[0] = col("f")
    q_ref[0] = ((col("q") * cos3 + col("qr") * sin3) * (HEAD_DIM ** -0.5)).astype(BF16)
    k_ref[0] = with_swapped(col("k") * cos + col("kr") * sin)
    v_ref[0] = with_swapped(col("v"))
    r_ref[0] = col("r")
    rk_ref[0] = col("rk")
    rv_ref[0] = col("rv")
    lora_ref[0] = col("lora")
    zg_ref[0] = col("g")


def _mix_in(x, m3, g1, w_all, cos, sin, tm):
    b, n, d = x.shape
    widths = (FOURIER_W, ATTN_W, 2 * KV_W, 2 * KV_W, J_W, J_W, I_W // 2, 4 * DECAY_LORA, GATE_LORA)
    dtypes = (F32, BF16, BF16, BF16, F32, F32, F32, F32, F32)
    return pl.pallas_call(
        _mix_in_kernel,
        out_shape=[jax.ShapeDtypeStruct((b, n, w), dt) for w, dt in zip(widths, dtypes)],
        grid=(b, n // tm),
        in_specs=[pl.BlockSpec((1, tm, d), lambda i, j: (i, j, 0)),
                  pl.BlockSpec((1, 3, d), lambda i, j: (i, 0, 0)),
                  _const_spec((1, d)),
                  _const_spec((d, MIX_TOTAL)),
                  pl.BlockSpec((tm, LANES), lambda i, j: (j, 0)),
                  pl.BlockSpec((tm, LANES), lambda i, j: (j, 0))],
        out_specs=[pl.BlockSpec((1, tm, w), lambda i, j: (i, j, 0)) for w in widths],
        compiler_params=_cparams(("arbitrary", "arbitrary")),
        name="mix_in",
    )(x, m3, g1, w_all, cos, sin)


def _fourier_stage1_kernel(l_ref, x_ref, o_ref):
    n1, rows, w = x_ref.shape[1:]
    res = _dot(l_ref[...], x_ref[0].reshape(n1 * rows, w).astype(BF16))
    o_ref[0] = res.reshape(2, n1, rows, w)


def _fourier_stage2_kernel(a_ref, mc_ref, ms_ref, cc_ref, sc_ref, o_ref, *, scale):
    w = cc_ref.shape[0]
    outs = []
    for j in range(a_ref.shape[2]):
        ar = a_ref[0, 0, j].astype(BF16)
        ai = a_ref[0, 1, j].astype(BF16)
        mc = mc_ref[j]
        ms = ms_ref[j]
        gr = _dot(mc, ar) + _dot(ms, ai)
        gi = _dot(mc, ai) - _dot(ms, ar)
        outs.append((_dot(gr.astype(BF16), cc_ref[...]) + _dot(gi.astype(BF16), sc_ref[...])) * scale)
    o_ref[0] = pltpu.einshape("k(jc)->kjc", jnp.concatenate(outs, axis=1), c=w)


def _fourier_small_kernel(z_ref, cn_ref, sn_ref, cc_ref, sc_ref, o_ref, *, scale):
    z = z_ref[0].astype(BF16)
    t1 = _dot(z, cc_ref[...]).astype(BF16)
    t2 = _dot(z, sc_ref[...]).astype(BF16)
    o_ref[0] = (_dot(cn_ref[...], t1) - _dot(sn_ref[...], t2)) * scale


def _channel_dft():
    gw = FOURIER_W // FOURIER_GROUPS
    c = np.arange(FOURIER_W)
    same = (c[:, None] // gw) == (c[None, :] // gw)
    ang = 2.0 * np.pi * ((c[:, None] % gw) * (c[None, :] % gw) % gw) / gw
    return (np.cos(ang) * same).astype(np.float32), (np.sin(ang) * same).astype(np.float32)


def _fourier_mix(z):
    b, n, w = z.shape
    gw = w // FOURIER_GROUPS
    scale = float(1.0 / np.sqrt(n * gw))
    cc, sc = _channel_dft()
    cc = jnp.asarray(cc, BF16)
    sc = jnp.asarray(sc, BF16)
    if n <= FOURIER_DIRECT_MAX:
        p = np.arange(n)
        ang = 2.0 * np.pi * ((p[:, None] * p[None, :]) % n) / n
        return pl.pallas_call(
            functools.partial(_fourier_small_kernel, scale=scale),
            out_shape=jax.ShapeDtypeStruct((b, n, w), F32),
            grid=(b,),
            in_specs=[pl.BlockSpec((1, n, w), lambda i: (i, 0, 0)),
                      _const_spec((n, n)), _const_spec((n, n)), _const_spec((w, w)), _const_spec((w, w))],
            out_specs=pl.BlockSpec((1, n, w), lambda i: (i, 0, 0)),
            compiler_params=_cparams(("arbitrary",)),
            name="fourier_small",
        )(z, jnp.asarray(np.cos(ang), BF16), jnp.asarray(np.sin(ang), BF16), cc, sc)
    n1 = FOURIER_N1
    n2 = n // n1
    k1 = np.arange(n1)
    ang1 = 2.0 * np.pi * ((k1[:, None] * k1[None, :]) % n1) / n1
    lhs1 = np.concatenate([np.cos(ang1), -np.sin(ang1)], axis=0)
    lhs1 = jnp.asarray(np.kron(lhs1, np.eye(SUBLANES)), BF16)
    a = pl.pallas_call(
        _fourier_stage1_kernel,
        out_shape=jax.ShapeDtypeStruct((b, 2, n1, n2, w), F32),
        grid=(b, n2 // SUBLANES),
        in_specs=[_const_spec((2 * n1 * SUBLANES, n1 * SUBLANES)),
                  pl.BlockSpec((1, n1, SUBLANES, w), lambda i, j: (i, 0, j, 0))],
        out_specs=pl.BlockSpec((1, 2, n1, SUBLANES, w), lambda i, j: (i, 0, 0, j, 0)),
        compiler_params=_cparams(("arbitrary", "arbitrary")),
        name="fourier_stage1",
    )(lhs1, z.reshape(b, n1, n2, w))
    k2 = np.arange(n2)
    freq = (k1[:, None, None] + n1 * k2[None, :, None]) * k2[None, None, :]
    ang2 = 2.0 * np.pi * (freq % n) / n
    out = pl.pallas_call(
        functools.partial(_fourier_stage2_kernel, scale=scale),
        out_shape=jax.ShapeDtypeStruct((b, n2, n1, w), F32),
        grid=(b, n1 // FOURIER_K1_BLOCK),
        in_specs=[pl.BlockSpec((1, 2, FOURIER_K1_BLOCK, n2, w), lambda i, j: (i, 0, j, 0, 0)),
                  pl.BlockSpec((FOURIER_K1_BLOCK, n2, n2), lambda i, j: (j, 0, 0)),
                  pl.BlockSpec((FOURIER_K1_BLOCK, n2, n2), lambda i, j: (j, 0, 0)),
                  _const_spec((w, w)), _const_spec((w, w))],
        out_specs=pl.BlockSpec((1, n2, FOURIER_K1_BLOCK, w), lambda i, j: (i, 0, j, 0)),
        compiler_params=_cparams(("arbitrary", "arbitrary")),
        name="fourier_stage2",
    )(a, jnp.asarray(np.cos(ang2), BF16), jnp.asarray(np.sin(ang2), BF16), cc, sc)
    return out.reshape(b, n, w)


NEG = -1e30


def _attn_kernel(*refs, window):
    if window:
        (sink_ref, q_ref, kp_ref, kc_ref, kn_ref, vp_ref, vc_ref, vn_ref, kx_ref, vx_ref, o_ref, s_ref) = refs
    else:
        (sink_ref, q_ref, kx_ref, vx_ref, o_ref, s_ref) = refs
    i = pl.program_id(1)
    nb = pl.num_programs(1)
    tq = q_ref.shape[1]
    lane = lax.broadcasted_iota(jnp.int32, (1, KV_W), 1)
    low = lane < HEAD_DIM
    if window:
        k_all = jnp.concatenate([kp_ref[0], kc_ref[0], kn_ref[0], kx_ref[0]], axis=0)
        v_all = jnp.concatenate([vp_ref[0], vc_ref[0], vn_ref[0], vx_ref[0]], axis=0)
        row = lax.broadcasted_iota(jnp.int32, (tq, WINDOW), 0)
        col = lax.broadcasted_iota(jnp.int32, (tq, WINDOW), 1)
        prev_ok = jnp.logical_and(col >= row, i > 0)
        next_ok = jnp.logical_and(col <= row, i < nb - 1)
        mask = jnp.concatenate([jnp.where(prev_ok, 0.0, NEG), jnp.zeros((tq, WINDOW), F32),
                                jnp.where(next_ok, 0.0, NEG), jnp.zeros((tq, kx_ref.shape[1]), F32)], axis=1)
    else:
        k_all = kx_ref[0]
        v_all = vx_ref[0]
        mask = None
    zero = jnp.zeros((), BF16)
    one = jnp.ones((), BF16)
    def operands(h):
        slot = h % 2
        swap = slot != h // ATTN_GROUP
        keep = (lane >= HEAD_DIM) if slot else low
        kh = jnp.where(keep, k_all[:, KV_W:] if swap else k_all[:, :KV_W], zero)
        vh = jnp.where(keep, v_all[:, KV_W:] if swap else v_all[:, :KV_W], one)
        return kh, vh

    def scores(h):
        qp = q_ref[0, :, (h // 2) * LANES:(h // 2 + 1) * LANES]
        s = lax.dot_general(qp, operands(h)[0], (((1,), (1,)), ((), ())), preferred_element_type=F32)
        return s if mask is None else s + mask

    for h in range(ATTN_HEADS):
        s_ref[h] = scores(h)
    for pair in range(ATTN_HEADS // 2):
        outs = []
        sinks = []
        for slot in range(2):
            h = 2 * pair + slot
            s = s_ref[h]
            sk = sink_ref[h]
            mx = jnp.maximum(s.max(axis=-1, keepdims=True), sk)
            outs.append(_dot(jnp.exp(s - mx).astype(BF16), operands(h)[1]))
            sinks.append(jnp.exp(sk - mx))
        num = jnp.where(low, outs[0], outs[1])
        den = pltpu.roll(jnp.where(low, outs[1], outs[0]), HEAD_DIM, 1) + jnp.where(low, sinks[0], sinks[1])
        o_ref[0, :, pair * LANES:(pair + 1) * LANES] = num / den


def _attention(q, k, v, kx, vx, sink8, window):
    b, n, _ = q.shape
    c = kx.shape[1]
    tq = WINDOW
    nb = n // tq
    smem = pl.BlockSpec(memory_space=pltpu.SMEM)
    qspec = pl.BlockSpec((1, tq, ATTN_W), lambda i, j: (i, j, 0))
    xspec = pl.BlockSpec((1, c, 2 * KV_W), lambda i, j: (i, 0, 0))
    if window:
        prev = pl.BlockSpec((1, tq, 2 * KV_W), lambda i, j: (i, jnp.maximum(j - 1, 0), 0))
        cur = pl.BlockSpec((1, tq, 2 * KV_W), lambda i, j: (i, j, 0))
        nxt = pl.BlockSpec((1, tq, 2 * KV_W), lambda i, j: (i, jnp.minimum(j + 1, nb - 1), 0))
        in_specs = [smem, qspec, prev, cur, nxt, prev, cur, nxt, xspec, xspec]
        args = (sink8, q, k, k, k, v, v, v, kx, vx)
    else:
        in_specs = [smem, qspec, xspec, xspec]
        args = (sink8, q, kx, vx)
    return pl.pallas_call(
        functools.partial(_attn_kernel, window=window),
        out_shape=jax.ShapeDtypeStruct((b, n, ATTN_W), F32),
        grid=(b, nb),
        in_specs=in_specs,
        out_specs=pl.BlockSpec((1, tq, ATTN_W), lambda i, j: (i, j, 0)),
        scratch_shapes=[pltpu.VMEM((ATTN_HEADS, tq, (3 * tq if window else 0) + c), F32)],
        compiler_params=_cparams(("arbitrary", "arbitrary")),
        name="window_attention" if window else "context_attention",
    )(*args)


def _lane_blocks_to_rows(x):
    return pltpu.einshape("t(kl)->tkl", x, l=LANES)


def _shift_rows(x, prev_row, next_row):
    n = x.shape[0]
    row = lax.broadcasted_iota(jnp.int32, (n, 1), 0)
    xp = jnp.where(row == 0, prev_row, pltpu.roll(x, 1, 0))
    xn = jnp.where(row == n - 1, next_row, pltpu.roll(x, n - 1, 0))
    return xp, xn


def _rwkv_prep_kernel(r_ref, rp_ref, rn_ref, k_ref, kp_ref, kn_ref, v_ref, vp_ref, vn_ref, lora_ref,
                      cwr_ref, cwk_ref, cwv_ref, w0_ref, w2_ref, a0_ref, a2_ref, vec_ref, sjj_ref, sji_ref, tri_ref,
                      vt_ref, bonus_ref, ab_ref, kr_ref, gm_ref):
    i = pl.program_id(1)
    first = (i > 0).astype(F32)
    last = (i < pl.num_programs(1) - 1).astype(F32)

    def conv(x_ref, p_ref, n_ref, cw_ref):
        x = x_ref[0]
        xp, xn = _shift_rows(x, p_ref[0, SUBLANES - 1:SUBLANES, :] * first, n_ref[0, 0:1, :] * last)
        cw = cw_ref[...]
        return xp * cw[0:1] + x * cw[1:2] + xn * cw[2:3]

    r = conv(r_ref, rp_ref, rn_ref, cwr_ref)
    k = conv(k_ref, kp_ref, kn_ref, cwk_ref)
    vh = conv(v_ref, vp_ref, vn_ref, cwv_ref)
    low = lax.broadcasted_iota(jnp.int32, (1, LANES), 1) < LANES // 2
    blocks = []
    for c in range(vh.shape[1] // LANES):
        xc = vh[:, c * LANES:(c + 1) * LANES]
        xr = pltpu.roll(xc, LANES // 2, 1)
        blocks += [jnp.where(low, xc, xr), jnp.where(low, xr, xc)]
    v = jnp.concatenate(blocks, axis=1)
    vec = vec_ref[...]
    sjj = sjj_ref[...]
    sji = sji_ref[...]

    def fold(x):
        return (x[:, 0:LANES] + x[:, LANES:2 * LANES]) + (x[:, 2 * LANES:3 * LANES] + x[:, 3 * LANES:4 * LANES])

    kk = k * vec[0:1]
    ss = _split_dot(fold(kk * kk), sjj, 2)
    inv = lax.rsqrt(jnp.maximum(ss, 1e-24))
    kk = kk * jnp.concatenate([inv, inv, inv, inv], axis=1)
    rk = _split_dot(fold(r * k * vec[2:3]), sji, 2)
    vt_ref[0] = _lane_blocks_to_rows(v)
    bonus_ref[0] = vh * jnp.concatenate([rk] * (vh.shape[1] // LANES), axis=1)
    lora = lora_ref[0]
    zw = jnp.tanh(lora[:, :2 * DECAY_LORA])
    za = lora[:, 2 * DECAY_LORA:]
    zw_hi = zw.astype(BF16)
    zw_lo = (zw - zw_hi.astype(F32)).astype(BF16)
    za_hi = za.astype(BF16)
    za_lo = (za - za_hi.astype(F32)).astype(BF16)

    def lora(x_hi, x_lo, w_ref, d):
        return _dot(x_hi, w_ref[0, d]) + (_dot(x_lo, w_ref[0, d]) + _dot(x_hi, w_ref[1, d]))

    for d in range(2):
        wpre = w0_ref[d:d + 1, :] + lora(zw_hi, zw_lo, w2_ref, d)
        apre = a0_ref[d:d + 1, :] + lora(za_hi, za_lo, a2_ref, d)
        a = _sigmoid(apre)
        log_decay = -np.float32(np.exp(-0.5)) * _sigmoid(wpre)
        tri = tri_ref[d]
        hi = log_decay.astype(BF16)
        rest = log_decay - hi.astype(F32)
        mid = rest.astype(BF16)
        low = (rest - mid.astype(F32)).astype(BF16)
        log_g = _dot(tri, hi) + _dot(tri, mid) + _dot(tri, low)
        g = jnp.exp(log_g)
        g_inv = jnp.exp(-log_g)
        a_t = -kk * jnp.exp(log_g - log_decay)
        b_t = kk * a * g_inv
        k_t = k * (1.0 + (a - 1.0) * vec[1:2]) * g_inv
        r_t = r * g
        ab_ref[d, 0] = _lane_blocks_to_rows(jnp.concatenate([a_t, b_t], axis=1))
        kr_ref[d, 0] = _lane_blocks_to_rows(jnp.concatenate([k_t, r_t], axis=1))
        gm_ref[d, 0] = g


def _rwkv_prep(r, k, v, lora, cwr, cwk, cwv, w0, w2, a0, a2, vec, sjj, sji, tm):
    b, n, _ = r.shape
    nh = n // SUBLANES

    def tile(w):
        return pl.BlockSpec((1, tm, w), lambda i, j: (i, j, 0))

    def halo_prev(w):
        return pl.BlockSpec((1, SUBLANES, w), lambda i, j: (i, jnp.maximum(j * (tm // SUBLANES) - 1, 0), 0))

    def halo_next(w):
        return pl.BlockSpec((1, SUBLANES, w), lambda i, j: (i, jnp.minimum((j + 1) * (tm // SUBLANES), nh - 1), 0))

    def dir_tile(w):
        return pl.BlockSpec((2, 1, tm, w), lambda i, j: (0, i, j, 0))

    t = np.arange(tm)
    same = (t[:, None] // SCAN_T) == (t[None, :] // SCAN_T)
    tri = jnp.asarray(np.stack([same & (t[None, :] <= t[:, None]), same & (t[None, :] >= t[:, None])]), BF16)
    def row_tiles(lead):
        return jax.ShapeDtypeStruct(lead + (n, SUBLANES, LANES), F32)

    def dir_row_tile():
        return pl.BlockSpec((2, 1, tm, SUBLANES, LANES), lambda i, j: (0, i, j, 0, 0))

    out_shape = [row_tiles((b,)), jax.ShapeDtypeStruct((b, n, I_W // 2), F32), row_tiles((2, b)), row_tiles((2, b)),
                 jax.ShapeDtypeStruct((2, b, n, J_W), F32)]
    row_tile = pl.BlockSpec((1, tm, SUBLANES, LANES), lambda i, j: (i, j, 0, 0))
    return pl.pallas_call(
        _rwkv_prep_kernel,
        out_shape=out_shape,
        grid=(b, n // tm),
        in_specs=[tile(J_W), halo_prev(J_W), halo_next(J_W),
                  tile(J_W), halo_prev(J_W), halo_next(J_W),
                  tile(I_W // 2), halo_prev(I_W // 2), halo_next(I_W // 2),
                  tile(4 * DECAY_LORA),
                  _const_spec((CONV_W, J_W)), _const_spec((CONV_W, J_W)), _const_spec((CONV_W, I_W // 2)),
                  _const_spec((2, J_W)), _const_spec((2, 2, 2 * DECAY_LORA, J_W)),
                  _const_spec((2, J_W)), _const_spec((2, 2, 2 * ICLR_LORA, J_W)),
                  _const_spec((3, J_W)), _const_spec((LANES, LANES)), _const_spec((LANES, LANES)),
                  _const_spec((2, tm, tm))],
        out_specs=[row_tile, tile(I_W // 2), dir_row_tile(), dir_row_tile(), dir_tile(J_W)],
        compiler_params=_cparams(("arbitrary", "arbitrary")),
        name="rwkv_prep",
    )(r, r, r, k, k, k, v, v, v, lora, cwr, cwk, cwv, w0, w2, a0, a2, vec, sjj, sji, tri)


N_ROWVEC = 4
N_PAIRS = N_ROWVEC // 2
N_DIR_REFS = N_PAIRS + 2
SCAN_PASSES = 1
DECAY_PASSES = 3
N_SG = 3
SG_PARTS = (((0, 0, 6, 0), (1, 0, 2, 6)),
            ((1, 2, 4, 0), (2, 0, 4, 4)),
            ((2, 4, 2, 0), (3, 0, 6, 2)))
SG_MIXED = 1


def _sg_expand_matrices():
    m = np.zeros((N_SG, 2 * LANES, E_W), np.float32)
    for sg, parts in enumerate(SG_PARTS):
        for half, (_, h0, nh, slot0) in enumerate(parts):
            for jl in range(4):
                for part in range(J_PARTS):
                    for h in range(h0, h0 + nh):
                        for rep in range(I_REP):
                            m[sg, half * LANES + jl * 32 + part * 8 + h,
                              jl * 128 + part * 32 + (slot0 + h - h0) * I_REP + rep] = 1.0
    return m


def _scan_kernel(*refs, t_chunk, n_batch):
    ins = refs[:2 * N_DIR_REFS]
    r2_ref, s0_ref = refs[2 * N_DIR_REFS:2 * N_DIR_REFS + 2]
    y_refs = refs[2 * N_DIR_REFS + 2:2 * N_DIR_REFS + 4]
    sout_ref = refs[2 * N_DIR_REFS + 4]
    e_ref, s_ref, ybuf_ref, yrev_ref = refs[2 * N_DIR_REFS + 5:]
    step_id = pl.program_id(0)
    assert n_batch == 2
    last = t_chunk - 1

    @pl.when(step_id == 0)
    def _():
        s_ref[...] = s0_ref[...]

    def src(x, s):
        return ins[(s // n_batch) * N_DIR_REFS + x]

    def expand(sg, first, second, passes):
        lhs = jnp.concatenate([first, second], axis=2)
        lhs = lhs.reshape(lhs.shape[0] * SUBLANES, 2 * LANES)
        return _split_dot(lhs, r2_ref[sg], passes).reshape(first.shape[0], SUBLANES, E_W)

    for sg, parts in enumerate(SG_PARTS):
        for pair in range(N_PAIRS):
            halves = []
            for s, _, _, _ in parts:
                ref = src(pair, s)
                if sg == SG_MIXED and s >= n_batch:
                    halves.append(jnp.concatenate([ref[0, s % n_batch, t:t + 1] for t in reversed(range(t_chunk))],
                                                  axis=0))
                else:
                    halves.append(ref[0, s % n_batch])
            e_ref[sg, pair] = expand(sg, halves[0], halves[1], SCAN_PASSES)

    lane = lax.broadcasted_iota(jnp.int32, (SUBLANES, LANES), 1)
    q_even = (lane // PART_LANES) % 2 == 0
    slot = (lane // I_REP) % HEAD_SLOTS

    def shifted(x, slots):
        return x if slots == 0 else pltpu.roll(x, (slots * I_REP) % LANES, 1)

    def allparts(p):
        assert J_PARTS == 4
        return ((p + pltpu.roll(p, 2 * PART_LANES, 1))
                + (pltpu.roll(p, PART_LANES, 1) + pltpu.roll(p, 3 * PART_LANES, 1)))

    def step(s, sg):
        (sa_src, ha, _, slot_a), (sb_src, hb, _, slot_b) = SG_PARTS[sg]
        ta = s if sa_src < n_batch else last - s
        tb = s if sb_src < n_batch else last - s
        te = s if sg == SG_MIXED else ta
        vp = jnp.where(slot < slot_b, shifted(src(N_PAIRS + 1, sa_src)[sa_src % n_batch, ta], slot_a - ha),
                       shifted(src(N_PAIRS + 1, sb_src)[sb_src % n_batch, tb], slot_b - hb))
        vr = pltpu.roll(vp, PART_LANES, 1)
        v = (jnp.where(q_even, vp, vr), jnp.where(q_even, vr, vp))

        def row(x, jj):
            r0 = (x % 2) * 4 + jj // 4
            return e_ref[sg, x // 2, te, r0:r0 + 1, (jj % 4) * LANES:(jj % 4 + 1) * LANES]

        def add(acc, ih, jj, term):
            k = (ih, jj % 4)
            acc[k] = term if k not in acc else acc[k] + term

        def total(acc, ih):
            return allparts((acc[ih, 0] + acc[ih, 1]) + (acc[ih, 2] + acc[ih, 3]))

        acc = {}
        for jj in range(JJ):
            a = row(0, jj)
            for ih in range(I_HI):
                add(acc, ih, jj, s_ref[sg, ih, jj] * a)
        sa = [total(acc, ih) for ih in range(I_HI)]
        acc = {}
        for jj in range(JJ):
            b, k, rr = row(1, jj), row(2, jj), row(3, jj)
            for ih in range(I_HI):
                new = (s_ref[sg, ih, jj] + v[ih] * k) + sa[ih] * b
                s_ref[sg, ih, jj] = new
                add(acc, ih, jj, new * rr)
        y = [total(acc, ih) for ih in range(I_HI)]
        packed = jnp.where(q_even, y[0], y[1])
        ybuf_ref[sg, te] = packed
        if sg == SG_MIXED:
            yrev_ref[last - s] = packed

    def one_token(s, carry):
        for sg in range(N_SG):
            step(s, sg)
        return carry

    lax.fori_loop(0, t_chunk, one_token, 0, unroll=SCAN_UNROLL)

    for s in range(2 * n_batch):
        out = jnp.zeros((t_chunk, SUBLANES, LANES), F32)
        for sg, parts in enumerate(SG_PARTS):
            for s2, h0, nh, slot0 in parts:
                if s2 == s:
                    buf = yrev_ref[...] if (sg == SG_MIXED and s >= n_batch) else ybuf_ref[sg]
                    if h0 != slot0:
                        buf = pltpu.roll(buf, ((h0 - slot0) * I_REP) % LANES, 2)
                    out = jnp.where(jnp.logical_and(slot >= h0, slot < h0 + nh)[None], buf, out)
        y_refs[s // n_batch][s % n_batch] = out

    for sg, parts in enumerate(SG_PARTS):
        halves = []
        toks = []
        for s, _, _, _ in parts:
            tile0, tok = (t_chunk - SUBLANES, SUBLANES - 1) if s < n_batch else (0, 0)
            g8 = src(N_PAIRS, s)[0, s % n_batch, tile0:tile0 + SUBLANES, :]
            halves.append(_lane_blocks_to_rows(jnp.concatenate([g8, g8], axis=1)))
            toks.append(tok)
        res = expand(sg, halves[0], halves[1], DECAY_PASSES)
        first_rows, second_rows = res[toks[0]], res[toks[1]]
        for jj in range(JJ):
            r0 = jj // 4
            blk = slice((jj % 4) * LANES, (jj % 4 + 1) * LANES)
            scale = jnp.where(slot[0:1] < parts[1][3], first_rows[r0:r0 + 1, blk], second_rows[r0:r0 + 1, blk])
            for ih in range(I_HI):
                s_ref[sg, ih, jj] = s_ref[sg, ih, jj] * scale

    @pl.when(step_id == pl.num_programs(0) - 1)
    def _():
        sout_ref[...] = s_ref[...]


def _rwkv_scan(vt, ab, kr, gm, state0, r2):
    b, n = vt.shape[:2]
    t = SCAN_T
    nc = n // t
    in_specs = []
    args = []
    out_specs = []
    for d in range(2):
        def chunk(i, d=d):
            return i if d == 0 else nc - 1 - i

        pair_spec = pl.BlockSpec((1, b, t, SUBLANES, LANES), lambda i, d=d, chunk=chunk: (d, 0, chunk(i), 0, 0))
        gm_spec = pl.BlockSpec((1, b, t, J_W), lambda i, d=d, chunk=chunk: (d, 0, chunk(i), 0))
        tile_spec = pl.BlockSpec((b, t, SUBLANES, LANES), lambda i, chunk=chunk: (0, chunk(i), 0, 0))
        in_specs += [pair_spec, pair_spec, gm_spec, tile_spec]
        args += [ab, kr, gm, vt]
        out_specs.append(tile_spec)
    state_shape = (N_SG, I_HI, JJ, SUBLANES, LANES)
    in_specs += [_const_spec((N_SG, 2 * LANES, E_W)), _const_spec(state_shape)]
    args += [r2, state0]
    out_specs.append(_const_spec(state_shape))
    y_shape = jax.ShapeDtypeStruct((b, n, SUBLANES, LANES), F32)
    return pl.pallas_call(
        functools.partial(_scan_kernel, t_chunk=t, n_batch=b),
        out_shape=[y_shape, y_shape, jax.ShapeDtypeStruct(state_shape, F32)],
        grid=(nc,),
        in_specs=in_specs,
        out_specs=out_specs,
        scratch_shapes=[pltpu.VMEM((N_SG, N_PAIRS, t, SUBLANES, E_W), F32),
                        pltpu.VMEM(state_shape, F32),
                        pltpu.VMEM((N_SG, t, SUBLANES, LANES), F32),
                        pltpu.VMEM((t, SUBLANES, LANES), F32)],
        compiler_params=_cparams(("arbitrary",)),
        name="rwkv_scan",
    )(*args)


def _mix_out_kernel(x_ref, m_ref, g_ref, f_ref, a_ref, yf_ref, yb_ref, bonus_ref, zg_ref,
                    g2_ref, ln_ref, sii_ref, wf_ref, wa_ref, wr_ref, o_ref):
    x = x_ref[0]
    yy = pltpu.einshape("tkl->t(kl)", yf_ref[0] + yb_ref[0])
    low = lax.broadcasted_iota(jnp.int32, (1, LANES), 1) < LANES // 2
    y = jnp.concatenate([jnp.where(low, yy[:, 2 * c * LANES:(2 * c + 1) * LANES],
                                   pltpu.roll(yy[:, (2 * c + 1) * LANES:(2 * c + 2) * LANES], LANES // 2, 1))
                         for c in range(I_W // (2 * LANES))], axis=1)
    sii = sii_ref[...]
    nblk = I_W // (2 * LANES)
    cnt = float(HEAD_DIM)

    def fold(z):
        acc = z[:, 0:LANES]
        for c in range(1, nblk):
            acc = acc + z[:, c * LANES:(c + 1) * LANES]
        return acc

    def spread(z):
        return jnp.concatenate([z] * nblk, axis=1)

    mu = _split_dot(fold(y), sii, 2) * (1.0 / cnt)
    dlt = y - spread(mu)
    var = _split_dot(fold(dlt * dlt), sii, 2) * (1.0 / cnt)
    ln = ln_ref[...]
    yn = dlt * spread(lax.rsqrt(var + GN_EPS)) * ln[0:1] + ln[1:2]
    gate = _dot(_sigmoid(zg_ref[0]).astype(BF16), g2_ref[...])
    rw = ((yn + bonus_ref[0]) * gate).astype(BF16)
    o = _dot(f_ref[0].astype(BF16), wf_ref[...]) + _dot(a_ref[0].astype(BF16), wa_ref[...]) + _dot(rw, wr_ref[...])
    o_ref[0] = x + m_ref[0] * _rms(o, g_ref[...])


def _mix_out(x, gate_row, g3, fo, ao, y_f, y_b, bonus, zg, g2p, ln, sii, wf, wa, wr, tm):
    b, n, d = x.shape

    def tile(w):
        return pl.BlockSpec((1, tm, w), lambda i, j: (i, j, 0))

    return pl.pallas_call(
        _mix_out_kernel,
        out_shape=jax.ShapeDtypeStruct(x.shape, F32),
        grid=(b, n // tm),
        in_specs=[tile(d), pl.BlockSpec((1, 1, d), lambda i, j: (i, 0, 0)), _const_spec((1, d)),
                  tile(FOURIER_W), tile(ATTN_W),
                  pl.BlockSpec((1, tm, SUBLANES, LANES), lambda i, j: (i, j, 0, 0)),
                  pl.BlockSpec((1, tm, SUBLANES, LANES), lambda i, j: (i, j, 0, 0)),
                  tile(I_W // 2), tile(GATE_LORA),
                  _const_spec((GATE_LORA, I_W // 2)), _const_spec((2, I_W // 2)), _const_spec((LANES, LANES)),
                  _const_spec((FOURIER_W, d)), _const_spec((ATTN_W, d)), _const_spec((I_W // 2, d))],
        out_specs=tile(d),
        compiler_params=_cparams(("arbitrary", "arbitrary")),
        name="mix_out",
    )(x, gate_row, g3, fo, ao, y_f, y_b, bonus, zg, g2p, ln, sii, wf, wa, wr)


def _rope_tables(n_tokens):
    rows_n = n_tokens // GRID_W
    row = jnp.repeat(jnp.arange(rows_n), GRID_W).astype(F32)
    colp = jnp.tile(jnp.arange(GRID_W), rows_n).astype(F32)
    n_freq = HEAD_DIM // 4
    inv = ROPE_BASE ** (-jnp.arange(n_freq, dtype=F32) / n_freq)
    ang = jnp.concatenate([row[:, None] * inv, colp[:, None] * inv], axis=-1)
    cos = jnp.tile(jnp.cos(ang), (1, 4))
    sin = jnp.tile(jnp.sin(ang), (1, 4))
    return cos, sin


def _rot_cols(w):
    d, n = w.shape
    w4 = w.reshape(d, n // HEAD_DIM, 2, HEAD_DIM // 2)
    return jnp.stack([-w4[:, :, 1], w4[:, :, 0]], axis=2).reshape(d, n)


def _layer_weights(li, p):
    f = p["ffn1_wo"].shape[1]
    fp = -(-f // (2 * LANES)) * (2 * LANES)
    out = {}
    for name in ("ffn1", "ffn2"):
        wi = p[name + "_wi"][li]
        wo = p[name + "_wo"][li]
        gap = jnp.zeros((wi.shape[0], fp - f), BF16)
        out[name] = (jnp.concatenate([wi[:, :f].astype(BF16), gap, wi[:, f:].astype(BF16), gap], axis=1),
                     jnp.pad(wo, ((0, fp - f), (0, 0))).astype(BF16))
    w_in = p["mix_w_in"][li]
    offs = np.cumsum([0, FOURIER_W, ATTN_W, KV_W, KV_W, RWKV_W, RWKV_W, RWKV_W,
                      DECAY_LORA, DECAY_LORA, ICLR_LORA, ICLR_LORA, GATE_LORA])
    part = [w_in[:, offs[i]:offs[i + 1]] for i in range(12)]
    wf, wq, wk, wv, wrr, wrk, wrv = part[:7]
    cols = {"f": wf, "q": wq, "qr": _rot_cols(wq), "k": wk, "kr": _rot_cols(wk), "v": wv,
            "r": _take_cols(wrr, J_IDX), "rk": _take_cols(wrk, J_IDX), "rv": _take_cols(wrv, I_IDX_COMPACT),
            "lora": jnp.concatenate(part[7:11], axis=1), "g": part[11]}
    out["w_all"] = jnp.concatenate([cols[n] for n, _ in MIX_COLS], axis=1).astype(BF16)
    conv = p["rwkv_conv"][li]
    out["cwr"] = _take_cols(conv[:, :RWKV_W], J_IDX)
    out["cwk"] = _take_cols(conv[:, RWKV_W:2 * RWKV_W], J_IDX)
    out["cwv"] = _take_cols(conv[:, 2 * RWKV_W:], I_IDX_COMPACT)
    def per_direction(w2):
        w2 = _take_cols(w2, J_IDX)
        z = jnp.zeros_like(w2[0])
        w2 = jnp.stack([jnp.concatenate([w2[0], z], axis=0), jnp.concatenate([z, w2[1]], axis=0)])
        hi = w2.astype(BF16)
        return jnp.stack([hi, (w2 - hi.astype(F32)).astype(BF16)])

    out["w0"] = _take_cols(p["rwkv_w0"][li], J_IDX)
    out["w2"] = per_direction(p["rwkv_w2"][li])
    out["a0"] = _take_cols(p["rwkv_a0"][li], J_IDX)
    out["a2"] = per_direction(p["rwkv_a2"][li])
    out["vec"] = _take_cols(jnp.stack([p["rwkv_k_k"][li], p["rwkv_k_a"][li], p["rwkv_r_k"][li].reshape(-1)]), J_IDX)
    out["g2"] = _take_cols(p["rwkv_g2"][li], I_IDX_COMPACT).astype(BF16)
    out["ln"] = _take_cols(jnp.stack([p["rwkv_ln_g"][li], p["rwkv_ln_b"][li]]), I_IDX_COMPACT)
    w_out = p["mix_w_out"][li]
    out["wo_f"] = w_out[:FOURIER_W].astype(BF16)
    out["wo_a"] = w_out[FOURIER_W:FOURIER_W + ATTN_W].astype(BF16)
    out["wo_r"] = _take_rows(w_out[FOURIER_W + ATTN_W:], I_IDX_COMPACT).astype(BF16)
    sink = p["attn_sink"][li]
    out["sink"] = jnp.concatenate([sink, jnp.zeros((8 - ATTN_HEADS,), F32)])
    return out


def kernel(x, c, ctx, c_ctx, mod_w, mod_b, norm_g, ffn1_wi, ffn1_wo, mix_w_in, mix_w_out, attn_sink,
           rwkv_conv, rwkv_w0, rwkv_w2, rwkv_a0, rwkv_a2, rwkv_g2, rwkv_k_k, rwkv_k_a, rwkv_r_k,
           rwkv_ln_g, rwkv_ln_b, ffn2_wi, ffn2_wo):
    p = dict(ffn1_wi=ffn1_wi, ffn1_wo=ffn1_wo, ffn2_wi=ffn2_wi, ffn2_wo=ffn2_wo, mix_w_in=mix_w_in,
             mix_w_out=mix_w_out, attn_sink=attn_sink, rwkv_conv=rwkv_conv, rwkv_w0=rwkv_w0, rwkv_w2=rwkv_w2,
             rwkv_a0=rwkv_a0, rwkv_a2=rwkv_a2, rwkv_g2=rwkv_g2, rwkv_k_k=rwkv_k_k, rwkv_k_a=rwkv_k_a,
             rwkv_r_k=rwkv_r_k, rwkv_ln_g=rwkv_ln_g, rwkv_ln_b=rwkv_ln_b)
    b, s, d = x.shape
    n_c = ctx.shape[1]
    depth = mod_w.shape[0]
    assert b + 1 <= 8 and s % WINDOW == 0 and n_c % WINDOW == 0 and WINDOW % SCAN_T == 0
    tm_l = 256
    tm_ffn = 512 if s % 512 == 0 else tm_l
    tm_c = min(256, n_c)

    cond8 = jnp.zeros((8, d), F32).at[:b].set(c).at[b].set(c_ctx)
    mod = _ada_mod(cond8, mod_w, mod_b).reshape(depth, 8, N_MOD, d)
    cos_l, sin_l = _rope_tables(s)
    cos_c = jnp.ones((n_c, LANES), F32)
    sin_c = jnp.zeros((n_c, LANES), F32)
    r2 = jnp.asarray(_sg_expand_matrices(), BF16)
    sjj, sji, sii = (jnp.asarray(m, BF16) for m in _seg_matrices())
    zero_state = jnp.zeros((N_SG, I_HI, JJ, SUBLANES, LANES), F32)

    xl, xc = x, ctx
    for li in range(depth):
        need_ctx_out = li < depth - 1
        w = _layer_weights(li, p)
        ml = mod[li, :b]
        mc = jnp.broadcast_to(mod[li, b:b + 1], (b, N_MOD, d))
        g = norm_g[li]
        xl = _ffn_half(xl, ml[:, 0:3], g[0:2], *w["ffn1"], tm_ffn)
        xc = _ffn_half(xc, mc[:, 0:3], g[0:2], *w["ffn1"], tm_c)

        def mixer_in(xx, mm, cos, sin, tm):
            return _mix_in(xx, mm[:, 3:6], g[2:3], w["w_all"], cos, sin, tm)

        fl, ql, kl, vl, rl, rkl, rvl, loral, zgl = mixer_in(xl, ml, cos_l, sin_l, tm_l)
        fc, qc, kc, vc, rc_, rkc, rvc, lorac, zgc = mixer_in(xc, mc, cos_c, sin_c, tm_c)

        attn_l = _attention(ql, kl, vl, kc, vc, w["sink"], True)

        def prep(r_, k_, v_, lora_, tm):
            return _rwkv_prep(r_, k_, v_, lora_, w["cwr"], w["cwk"], w["cwv"], w["w0"], w["w2"],
                              w["a0"], w["a2"], w["vec"], sjj, sji, tm)

        pc = prep(rc_, rkc, rvc, lorac, tm_c)
        plat = prep(rl, rkl, rvl, loral, tm_l)
        ycf, ycb, state_c = _rwkv_scan(pc[0], *pc[2:], zero_state, r2)
        ylf, ylb, _ = _rwkv_scan(plat[0], *plat[2:], state_c, r2)

        def mixer_out(xx, mm, fo, ao, y_f, y_b, bonus, zg, tm):
            return _mix_out(xx, mm[:, 5:6], g[3:4], fo, ao, y_f, y_b, bonus, zg, w["g2"], w["ln"], sii,
                            w["wo_f"], w["wo_a"], w["wo_r"], tm)

        xl = mixer_out(xl, ml, _fourier_mix(fl), attn_l, ylf, ylb, plat[1], zgl, tm_l)
        if need_ctx_out:
            attn_c = _attention(qc, None, None, kc, vc, w["sink"], False)
            xc = mixer_out(xc, mc, _fourier_mix(fc), attn_c, ycf, ycb, pc[1], zgc, tm_c)
            xc = _ffn_half(xc, mc[:, 6:9], g[4:6], *w["ffn2"], tm_c)
        xl = _ffn_half(xl, ml[:, 6:9], g[4:6], *w["ffn2"], tm_ffn)
    return xl
```

```python
import functools

import numpy as np
import jax
import jax.numpy as jnp
from jax import lax
from jax.experimental import pallas as pl
from jax.experimental.pallas import tpu as pltpu

F32 = jnp.float32
BF16 = jnp.bfloat16

HEAD_DIM = 64
GRID_W = 64
FOURIER_W = 256
FOURIER_GROUPS = 4
ATTN_HEADS = 6
ATTN_KV_HEADS = 2
ATTN_GROUP = ATTN_HEADS // ATTN_KV_HEADS
ATTN_W = ATTN_HEADS * HEAD_DIM
KV_W = ATTN_KV_HEADS * HEAD_DIM
RWKV_HEADS = 6
RWKV_W = RWKV_HEADS * HEAD_DIM
WINDOW = 128
ROPE_BASE = 10000.0
DECAY_LORA = 64
ICLR_LORA = 64
GATE_LORA = 128
CONV_W = 3
N_MOD = 9
NORM_EPS = 1e-6
GN_EPS = 64e-5

LANES = 128
SUBLANES = 8
VMEM_LIMIT = 56 * 1024 * 1024

HEAD_SLOTS = 8
J_PARTS = 4
I_REP = 4
JJ = HEAD_DIM // J_PARTS
I_HI = HEAD_DIM // (SUBLANES * I_REP)
PART_LANES = LANES // J_PARTS
J_W = 4 * LANES
E_W = 4 * LANES
I_W = SUBLANES * LANES
SCAN_T = 64
SCAN_UNROLL = 32
FOURIER_N1 = 64
FOURIER_DIRECT_MAX = 256
FOURIER_K1_BLOCK = SUBLANES


def _j_layout():
    idx = np.full((J_W,), -1, np.int64)
    for jh in range(4):
        for jl in range(4):
            for part in range(J_PARTS):
                for h in range(RWKV_HEADS):
                    idx[jh * 128 + jl * 32 + part * 8 + h] = h * HEAD_DIM + part * JJ + jh * 4 + jl
    return idx


def _i_layout_compact():
    idx = np.full((I_W // 2,), -1, np.int64)
    for isub in range(8):
        for q in range(2):
            for h in range(RWKV_HEADS):
                for il in range(I_REP):
                    idx[isub * 64 + q * 32 + h * 4 + il] = h * HEAD_DIM + q * 32 + isub * 4 + il
    return idx


J_IDX = _j_layout()
I_IDX_COMPACT = _i_layout_compact()


def _take_cols(w, idx):
    g = jnp.take(w, jnp.asarray(np.maximum(idx, 0)), axis=-1)
    return g * jnp.asarray((idx >= 0).astype(np.float32))


def _take_rows(w, idx):
    g = jnp.take(w, jnp.asarray(np.maximum(idx, 0)), axis=0)
    return g * jnp.asarray((idx >= 0).astype(np.float32))[:, None]


def _seg_matrices():
    hj = np.arange(LANES) % 8
    hi = (np.arange(LANES) // I_REP) % 8
    jj = (hj[:, None] == hj[None, :]).astype(np.float32)
    ji = (hj[:, None] == hi[None, :]).astype(np.float32)
    ii = (hi[:, None] == hi[None, :]).astype(np.float32)
    return jj, ji, ii


def _cparams(sem, vmem=VMEM_LIMIT):
    return pltpu.CompilerParams(dimension_semantics=sem, vmem_limit_bytes=vmem)


def _const_spec(shape):
    nd = len(shape)
    return pl.BlockSpec(shape, lambda *_: (0,) * nd)


def _resident_spec(shape):
    nd = len(shape)
    return pl.BlockSpec(shape, lambda *_: (0,) * nd, pipeline_mode=pl.Buffered(1))


def _sigmoid(x):
    return 1.0 / (1.0 + jnp.exp(-x))


def _rms(x, g):
    return x * lax.rsqrt(jnp.mean(x * x, axis=-1, keepdims=True) + NORM_EPS) * g


def _dot(a, b):
    return jnp.dot(a, b, preferred_element_type=F32)


def _split_dot(x, m, passes):
    acc = None
    rem = x
    for p in range(passes):
        piece = rem.astype(BF16)
        d = _dot(piece, m)
        acc = d if acc is None else acc + d
        if p + 1 < passes:
            rem = rem - piece.astype(F32)
    return acc


def _ada_kernel(c_ref, w_ref, b_ref, o_ref):
    c = c_ref[...]
    s = c * _sigmoid(c)
    o_ref[0] = _dot(s.astype(BF16), w_ref[0].astype(BF16)) + b_ref[0]


def _ada_mod(cond8, mod_w, mod_b):
    nl, d, nw = mod_w.shape
    tn = nw // 8
    return pl.pallas_call(
        _ada_kernel,
        out_shape=jax.ShapeDtypeStruct((nl, 8, nw), F32),
        grid=(nl, nw // tn),
        in_specs=[pl.BlockSpec((8, d), lambda l, j: (0, 0)),
                  pl.BlockSpec((1, d, tn), lambda l, j: (l, 0, j)),
                  pl.BlockSpec((1, 1, tn), lambda l, j: (l, 0, j))],
        out_specs=pl.BlockSpec((1, 8, tn), lambda l, j: (l, 0, j)),
        compiler_params=_cparams(("arbitrary", "arbitrary")),
        name="ada_mod",
    )(cond8, mod_w, mod_b.reshape(nl, 1, nw))


def _ffn_kernel(x_ref, m_ref, g_ref, wi_ref, wo_ref, o_ref):
    x = x_ref[0]
    m = m_ref[0]
    g = g_ref[...]
    h = _rms(x, g[0:1]) * (1.0 + m[1:2]) + m[0:1]
    hb = h.astype(BF16)
    fp = wo_ref.shape[0]
    gate = _dot(hb, wi_ref[:, :fp])
    up = _dot(hb, wi_ref[:, fp:])
    act = (gate * _sigmoid(gate) * up).astype(BF16)
    y = _dot(act, wo_ref[...])
    o_ref[0] = x + 0.5 * m[2:3] * _rms(y, g[1:2])


def _ffn_half(x, m3, g2, wi, wo, tm):
    b, n, d = x.shape
    fp = wo.shape[0]
    return pl.pallas_call(
        _ffn_kernel,
        out_shape=jax.ShapeDtypeStruct(x.shape, F32),
        grid=(b, n // tm),
        in_specs=[pl.BlockSpec((1, tm, d), lambda i, j: (i, j, 0)),
                  pl.BlockSpec((1, 3, d), lambda i, j: (i, 0, 0)),
                  _const_spec((2, d)),
                  _resident_spec((d, 2 * fp)), _resident_spec((fp, d))],
        out_specs=pl.BlockSpec((1, tm, d), lambda i, j: (i, j, 0)),
        compiler_params=_cparams(("arbitrary", "arbitrary")),
        name="ffn_half",
    )(x, m3, g2, wi, wo)


MIX_COLS = (("f", FOURIER_W), ("q", ATTN_W), ("k", KV_W),
            ("v", KV_W), ("r", J_W), ("rk", J_W), ("rv", I_W // 2), ("lora", 4 * DECAY_LORA), ("g", GATE_LORA))
MIX_OFF = {}
_o = 0
for _n, _w in MIX_COLS:
    MIX_OFF[_n] = (_o, _o + _w)
    _o += _w
MIX_TOTAL = _o


def _mix_in_kernel(x_ref, m_ref, g_ref, w_ref, cos_ref, sin_ref,
                   f_ref, q_ref, k_ref, v_ref, r_ref, rk_ref, rv_ref, lora_ref, zg_ref):
    x = x_ref[0]
    m = m_ref[0]
    h = _rms(x, g_ref[...]) * (1.0 + m[1:2]) + m[0:1]
    z = _dot(h.astype(BF16), w_ref[...])

    def col(name):
        lo, hi = MIX_OFF[name]
        return z[:, lo:hi]

    cos = cos_ref[...]
    sin = sin_ref[...]
    cos3 = jnp.concatenate([cos, cos, cos], axis=1)
    sin3 = jnp.concatenate([sin, sin, sin], axis=1)

    def with_swapped(x):
        return jnp.concatenate([x, pltpu.roll(x, HEAD_DIM, 1)], axis=1).astype(BF16)

    half = HEAD_DIM // 2
    first_half = (lax.broadcasted_iota(jnp.int32, (1, LANES), 1) // half) % 2 == 0

    def rotate_half(x):
        blocks = []
        for c in range(x.shape[1] // LANES):
            xc = x[:, c * LANES:(c + 1) * LANES]
            blocks.append(jnp.where(first_half, -pltpu.roll(xc, LANES - half, 1), pltpu.roll(xc, half, 1)))
        return jnp.concatenate(blocks, axis=1)

    q = col("q")
    k = col("k")
    f_ref[0] = col("f")
    q_ref[0] = ((q * cos3 + rotate_half(q) * sin3) * (HEAD_DIM ** -0.5)).astype(BF16)
    k_ref[0] = with_swapped(k * cos + rotate_half(k) * sin)
    v_ref[0] = with_swapped(col("v"))
    r_ref[0] = col("r")
    rk_ref[0] = col("rk")
    rv_ref[0] = col("rv")
    lora_ref[0] = col("lora")
    zg_ref[0] = col("g")


def _mix_in(x, m3, g1, w_all, cos, sin, tm):
    b, n, d = x.shape
    widths = (FOURIER_W, ATTN_W, 2 * KV_W, 2 * KV_W, J_W, J_W, I_W // 2, 4 * DECAY_LORA, GATE_LORA)
    dtypes = (F32, BF16, BF16, BF16, F32, F32, F32, F32, F32)
    return pl.pallas_call(
        _mix_in_kernel,
        out_shape=[jax.ShapeDtypeStruct((b, n, w), dt) for w, dt in zip(widths, dtypes)],
        grid=(b, n // tm),
        in_specs=[pl.BlockSpec((1, tm, d), lambda i, j: (i, j, 0)),
                  pl.BlockSpec((1, 3, d), lambda i, j: (i, 0, 0)),
                  _const_spec((1, d)),
                  _const_spec((d, MIX_TOTAL)),
                  pl.BlockSpec((tm, LANES), lambda i, j: (j, 0)),
                  pl.BlockSpec((tm, LANES), lambda i, j: (j, 0))],
        out_specs=[pl.BlockSpec((1, tm, w), lambda i, j: (i, j, 0)) for w in widths],
        compiler_params=_cparams(("arbitrary", "arbitrary")),
        name="mix_in",
    )(x, m3, g1, w_all, cos, sin)


def _fourier_stage1_kernel(l_ref, x_ref, o_ref):
    n1, rows, w = x_ref.shape[1:]
    res = _dot(l_ref[...], x_ref[0].reshape(n1 * rows, w).astype(BF16))
    o_ref[0] = res.reshape(2, n1, rows, w)


def _fourier_stage2_kernel(a_ref, mc_ref, ms_ref, cc_ref, sc_ref, o_ref, *, scale):
    w = cc_ref.shape[0]
    outs = []
    for j in range(a_ref.shape[2]):
        ar = a_ref[0, 0, j].astype(BF16)
        ai = a_ref[0, 1, j].astype(BF16)
        mc = mc_ref[j]
        ms = ms_ref[j]
        gr = _dot(mc, ar) + _dot(ms, ai)
        gi = _dot(mc, ai) - _dot(ms, ar)
        outs.append((_dot(gr.astype(BF16), cc_ref[...]) + _dot(gi.astype(BF16), sc_ref[...])) * scale)
    o_ref[0] = pltpu.einshape("k(jc)->kjc", jnp.concatenate(outs, axis=1), c=w)


def _fourier_small_kernel(z_ref, cn_ref, sn_ref, cc_ref, sc_ref, o_ref, *, scale):
    z = z_ref[0].astype(BF16)
    t1 = _dot(z, cc_ref[...]).astype(BF16)
    t2 = _dot(z, sc_ref[...]).astype(BF16)
    o_ref[0] = (_dot(cn_ref[...], t1) - _dot(sn_ref[...], t2)) * scale


def _channel_dft():
    gw = FOURIER_W // FOURIER_GROUPS
    c = np.arange(FOURIER_W)
    same = (c[:, None] // gw) == (c[None, :] // gw)
    ang = 2.0 * np.pi * ((c[:, None] % gw) * (c[None, :] % gw) % gw) / gw
    return (np.cos(ang) * same).astype(np.float32), (np.sin(ang) * same).astype(np.float32)


def _fourier_mix(z):
    b, n, w = z.shape
    gw = w // FOURIER_GROUPS
    scale = float(1.0 / np.sqrt(n * gw))
    cc, sc = _channel_dft()
    cc = jnp.asarray(cc, BF16)
    sc = jnp.asarray(sc, BF16)
    if n <= FOURIER_DIRECT_MAX:
        p = np.arange(n)
        ang = 2.0 * np.pi * ((p[:, None] * p[None, :]) % n) / n
        return pl.pallas_call(
            functools.partial(_fourier_small_kernel, scale=scale),
            out_shape=jax.ShapeDtypeStruct((b, n, w), F32),
            grid=(b,),
            in_specs=[pl.BlockSpec((1, n, w), lambda i: (i, 0, 0)),
                      _const_spec((n, n)), _const_spec((n, n)), _const_spec((w, w)), _const_spec((w, w))],
            out_specs=pl.BlockSpec((1, n, w), lambda i: (i, 0, 0)),
            compiler_params=_cparams(("arbitrary",)),
            name="fourier_small",
        )(z, jnp.asarray(np.cos(ang), BF16), jnp.asarray(np.sin(ang), BF16), cc, sc)
    n1 = FOURIER_N1
    n2 = n // n1
    k1 = np.arange(n1)
    ang1 = 2.0 * np.pi * ((k1[:, None] * k1[None, :]) % n1) / n1
    lhs1 = np.concatenate([np.cos(ang1), -np.sin(ang1)], axis=0)
    lhs1 = jnp.asarray(np.kron(lhs1, np.eye(SUBLANES)), BF16)
    a = pl.pallas_call(
        _fourier_stage1_kernel,
        out_shape=jax.ShapeDtypeStruct((b, 2, n1, n2, w), F32),
        grid=(b, n2 // SUBLANES),
        in_specs=[_const_spec((2 * n1 * SUBLANES, n1 * SUBLANES)),
                  pl.BlockSpec((1, n1, SUBLANES, w), lambda i, j: (i, 0, j, 0))],
        out_specs=pl.BlockSpec((1, 2, n1, SUBLANES, w), lambda i, j: (i, 0, 0, j, 0)),
        compiler_params=_cparams(("arbitrary", "arbitrary")),
        name="fourier_stage1",
    )(lhs1, z.reshape(b, n1, n2, w))
    k2 = np.arange(n2)
    freq = (k1[:, None, None] + n1 * k2[None, :, None]) * k2[None, None, :]
    ang2 = 2.0 * np.pi * (freq % n) / n
    out = pl.pallas_call(
        functools.partial(_fourier_stage2_kernel, scale=scale),
        out_shape=jax.ShapeDtypeStruct((b, n2, n1, w), F32),
        grid=(b, n1 // FOURIER_K1_BLOCK),
        in_specs=[pl.BlockSpec((1, 2, FOURIER_K1_BLOCK, n2, w), lambda i, j: (i, 0, j, 0, 0)),
                  pl.BlockSpec((FOURIER_K1_BLOCK, n2, n2), lambda i, j: (j, 0, 0)),
                  pl.BlockSpec((FOURIER_K1_BLOCK, n2, n2), lambda i, j: (j, 0, 0)),
                  _const_spec((w, w)), _const_spec((w, w))],
        out_specs=pl.BlockSpec((1, n2, FOURIER_K1_BLOCK, w), lambda i, j: (i, 0, j, 0)),
        compiler_params=_cparams(("arbitrary", "arbitrary")),
        name="fourier_stage2",
    )(a, jnp.asarray(np.cos(ang2), BF16), jnp.asarray(np.sin(ang2), BF16), cc, sc)
    return out.reshape(b, n, w)


NEG = -1e30


def _attn_kernel(*refs, window):
    if window:
        (sink_ref, q_ref, kp_ref, kc_ref, kn_ref, vp_ref, vc_ref, vn_ref, kx_ref, vx_ref, o_ref, s_ref) = refs
    else:
        (sink_ref, q_ref, kx_ref, vx_ref, o_ref, s_ref) = refs
    i = pl.program_id(1)
    nb = pl.num_programs(1)
    tq = q_ref.shape[1]
    lane = lax.broadcasted_iota(jnp.int32, (1, KV_W), 1)
    low = lane < HEAD_DIM
    if window:
        k_all = jnp.concatenate([kp_ref[0], kc_ref[0], kn_ref[0], kx_ref[0]], axis=0)
        v_all = jnp.concatenate([vp_ref[0], vc_ref[0], vn_ref[0], vx_ref[0]], axis=0)
        row = lax.broadcasted_iota(jnp.int32, (tq, WINDOW), 0)
        col = lax.broadcasted_iota(jnp.int32, (tq, WINDOW), 1)
        prev_ok = jnp.logical_and(col >= row, i > 0)
        next_ok = jnp.logical_and(col <= row, i < nb - 1)
        mask = jnp.concatenate([jnp.where(prev_ok, 0.0, NEG), jnp.zeros((tq, WINDOW), F32),
                                jnp.where(next_ok, 0.0, NEG), jnp.zeros((tq, kx_ref.shape[1]), F32)], axis=1)
    else:
        k_all = kx_ref[0]
        v_all = vx_ref[0]
        mask = None
    zero = jnp.zeros((), BF16)
    one = jnp.ones((), BF16)
    def operands(h):
        slot = h % 2
        swap = slot != h // ATTN_GROUP
        keep = (lane >= HEAD_DIM) if slot else low
        kh = jnp.where(keep, k_all[:, KV_W:] if swap else k_all[:, :KV_W], zero)
        vh = jnp.where(keep, v_all[:, KV_W:] if swap else v_all[:, :KV_W], one)
        return kh, vh

    def scores(h):
        qp = q_ref[0, :, (h // 2) * LANES:(h // 2 + 1) * LANES]
        s = lax.dot_general(qp, operands(h)[0], (((1,), (1,)), ((), ())), preferred_element_type=F32)
        return s if mask is None else s + mask

    for h in range(ATTN_HEADS):
        s_ref[h] = scores(h)
    for pair in range(ATTN_HEADS // 2):
        outs = []
        sinks = []
        for slot in range(2):
            h = 2 * pair + slot
            s = s_ref[h]
            sk = sink_ref[h]
            mx = jnp.maximum(s.max(axis=-1, keepdims=True), sk)
            outs.append(_dot(jnp.exp(s - mx).astype(BF16), operands(h)[1]))
            sinks.append(jnp.exp(sk - mx))
        num = jnp.where(low, outs[0], outs[1])
        den = pltpu.roll(jnp.where(low, outs[1], outs[0]), HEAD_DIM, 1) + jnp.where(low, sinks[0], sinks[1])
        o_ref[0, :, pair * LANES:(pair + 1) * LANES] = num / den


def _attention(q, k, v, kx, vx, sink8, window):
    b, n, _ = q.shape
    c = kx.shape[1]
    tq = WINDOW
    nb = n // tq
    smem = pl.BlockSpec(memory_space=pltpu.SMEM)
    qspec = pl.BlockSpec((1, tq, ATTN_W), lambda i, j: (i, j, 0))
    xspec = pl.BlockSpec((1, c, 2 * KV_W), lambda i, j: (i, 0, 0))
    if window:
        prev = pl.BlockSpec((1, tq, 2 * KV_W), lambda i, j: (i, jnp.maximum(j - 1, 0), 0))
        cur = pl.BlockSpec((1, tq, 2 * KV_W), lambda i, j: (i, j, 0))
        nxt = pl.BlockSpec((1, tq, 2 * KV_W), lambda i, j: (i, jnp.minimum(j + 1, nb - 1), 0))
        in_specs = [smem, qspec, prev, cur, nxt, prev, cur, nxt, xspec, xspec]
        args = (sink8, q, k, k, k, v, v, v, kx, vx)
    else:
        in_specs = [smem, qspec, xspec, xspec]
        args = (sink8, q, kx, vx)
    return pl.pallas_call(
        functools.partial(_attn_kernel, window=window),
        out_shape=jax.ShapeDtypeStruct((b, n, ATTN_W), F32),
        grid=(b, nb),
        in_specs=in_specs,
        out_specs=pl.BlockSpec((1, tq, ATTN_W), lambda i, j: (i, j, 0)),
        scratch_shapes=[pltpu.VMEM((ATTN_HEADS, tq, (3 * tq if window else 0) + c), F32)],
        compiler_params=_cparams(("arbitrary", "arbitrary")),
        name="window_attention" if window else "context_attention",
    )(*args)


def _lane_blocks_to_rows(x):
    return pltpu.einshape("t(kl)->tkl", x, l=LANES)


def _shift_rows(x, prev_row, next_row):
    n = x.shape[0]
    row = lax.broadcasted_iota(jnp.int32, (n, 1), 0)
    xp = jnp.where(row == 0, prev_row, pltpu.roll(x, 1, 0))
    xn = jnp.where(row == n - 1, next_row, pltpu.roll(x, n - 1, 0))
    return xp, xn


def _rwkv_prep_kernel(r_ref, rp_ref, rn_ref, k_ref, kp_ref, kn_ref, v_ref, vp_ref, vn_ref, lora_ref,
                      cwr_ref, cwk_ref, cwv_ref, w0_ref, w2_ref, a0_ref, a2_ref, vec_ref, sjj_ref, sji_ref, tri_ref,
                      vt_ref, bonus_ref, ab_ref, kr_ref, gm_ref):
    i = pl.program_id(1)
    first = (i > 0).astype(F32)
    last = (i < pl.num_programs(1) - 1).astype(F32)

    def conv(x_ref, p_ref, n_ref, cw_ref):
        x = x_ref[0]
        xp, xn = _shift_rows(x, p_ref[0, SUBLANES - 1:SUBLANES, :] * first, n_ref[0, 0:1, :] * last)
        cw = cw_ref[...]
        return xp * cw[0:1] + x * cw[1:2] + xn * cw[2:3]

    r = conv(r_ref, rp_ref, rn_ref, cwr_ref)
    k = conv(k_ref, kp_ref, kn_ref, cwk_ref)
    vh = conv(v_ref, vp_ref, vn_ref, cwv_ref)
    low = lax.broadcasted_iota(jnp.int32, (1, LANES), 1) < LANES // 2
    blocks = []
    for c in range(vh.shape[1] // LANES):
        xc = vh[:, c * LANES:(c + 1) * LANES]
        xr = pltpu.roll(xc, LANES // 2, 1)
        blocks += [jnp.where(low, xc, xr), jnp.where(low, xr, xc)]
    v = jnp.concatenate(blocks, axis=1)
    vec = vec_ref[...]
    sjj = sjj_ref[...]
    sji = sji_ref[...]

    def fold(x):
        return (x[:, 0:LANES] + x[:, LANES:2 * LANES]) + (x[:, 2 * LANES:3 * LANES] + x[:, 3 * LANES:4 * LANES])

    kk = k * vec[0:1]
    ss = _split_dot(fold(kk * kk), sjj, 2)
    inv = lax.rsqrt(jnp.maximum(ss, 1e-24))
    kk = kk * jnp.concatenate([inv, inv, inv, inv], axis=1)
    rk = _split_dot(fold(r * k * vec[2:3]), sji, 2)
    vt_ref[0] = _lane_blocks_to_rows(v)
    bonus_ref[0] = vh * jnp.concatenate([rk] * (vh.shape[1] // LANES), axis=1)
    lora = lora_ref[0]
    zw = jnp.tanh(lora[:, :2 * DECAY_LORA])
    za = lora[:, 2 * DECAY_LORA:]
    zw_hi = zw.astype(BF16)
    zw_lo = (zw - zw_hi.astype(F32)).astype(BF16)
    za_hi = za.astype(BF16)
    za_lo = (za - za_hi.astype(F32)).astype(BF16)

    def lora(x_hi, x_lo, w_ref, d):
        return _dot(x_hi, w_ref[0, d]) + (_dot(x_lo, w_ref[0, d]) + _dot(x_hi, w_ref[1, d]))

    for d in range(2):
        wpre = w0_ref[d:d + 1, :] + lora(zw_hi, zw_lo, w2_ref, d)
        apre = a0_ref[d:d + 1, :] + lora(za_hi, za_lo, a2_ref, d)
        a = _sigmoid(apre)
        log_decay = -np.float32(np.exp(-0.5)) * _sigmoid(wpre)
        tri = tri_ref[d]
        hi = log_decay.astype(BF16)
        rest = log_decay - hi.astype(F32)
        mid = rest.astype(BF16)
        low = (rest - mid.astype(F32)).astype(BF16)
        log_g = _dot(tri, hi) + _dot(tri, mid) + _dot(tri, low)
        g = jnp.exp(log_g)
        g_inv = jnp.exp(-log_g)
        a_t = -kk * jnp.exp(log_g - log_decay)
        b_t = kk * a * g_inv
        k_t = k * (1.0 + (a - 1.0) * vec[1:2]) * g_inv
        r_t = r * g
        ab_ref[d, 0] = _lane_blocks_to_rows(jnp.concatenate([a_t, b_t], axis=1))
        kr_ref[d, 0] = _lane_blocks_to_rows(jnp.concatenate([k_t, r_t], axis=1))
        gm_ref[d, 0] = g


def _rwkv_prep(r, k, v, lora, cwr, cwk, cwv, w0, w2, a0, a2, vec, sjj, sji, tm):
    b, n, _ = r.shape
    nh = n // SUBLANES

    def tile(w):
        return pl.BlockSpec((1, tm, w), lambda i, j: (i, j, 0))

    def halo_prev(w):
        return pl.BlockSpec((1, SUBLANES, w), lambda i, j: (i, jnp.maximum(j * (tm // SUBLANES) - 1, 0), 0))

    def halo_next(w):
        return pl.BlockSpec((1, SUBLANES, w), lambda i, j: (i, jnp.minimum((j + 1) * (tm // SUBLANES), nh - 1), 0))

    def dir_tile(w):
        return pl.BlockSpec((2, 1, tm, w), lambda i, j: (0, i, j, 0))

    t = np.arange(tm)
    same = (t[:, None] // SCAN_T) == (t[None, :] // SCAN_T)
    tri = jnp.asarray(np.stack([same & (t[None, :] <= t[:, None]), same & (t[None, :] >= t[:, None])]), BF16)
    def row_tiles(lead):
        return jax.ShapeDtypeStruct(lead + (n, SUBLANES, LANES), F32)

    def dir_row_tile():
        return pl.BlockSpec((2, 1, tm, SUBLANES, LANES), lambda i, j: (0, i, j, 0, 0))

    out_shape = [row_tiles((b,)), jax.ShapeDtypeStruct((b, n, I_W // 2), F32), row_tiles((2, b)), row_tiles((2, b)),
                 jax.ShapeDtypeStruct((2, b, n, J_W), F32)]
    row_tile = pl.BlockSpec((1, tm, SUBLANES, LANES), lambda i, j: (i, j, 0, 0))
    return pl.pallas_call(
        _rwkv_prep_kernel,
        out_shape=out_shape,
        grid=(b, n // tm),
        in_specs=[tile(J_W), halo_prev(J_W), halo_next(J_W),
                  tile(J_W), halo_prev(J_W), halo_next(J_W),
                  tile(I_W // 2), halo_prev(I_W // 2), halo_next(I_W // 2),
                  tile(4 * DECAY_LORA),
                  _const_spec((CONV_W, J_W)), _const_spec((CONV_W, J_W)), _const_spec((CONV_W, I_W // 2)),
                  _const_spec((2, J_W)), _const_spec((2, 2, 2 * DECAY_LORA, J_W)),
                  _const_spec((2, J_W)), _const_spec((2, 2, 2 * ICLR_LORA, J_W)),
                  _const_spec((3, J_W)), _const_spec((LANES, LANES)), _const_spec((LANES, LANES)),
                  _const_spec((2, tm, tm))],
        out_specs=[row_tile, tile(I_W // 2), dir_row_tile(), dir_row_tile(), dir_tile(J_W)],
        compiler_params=_cparams(("arbitrary", "arbitrary")),
        name="rwkv_prep",
    )(r, r, r, k, k, k, v, v, v, lora, cwr, cwk, cwv, w0, w2, a0, a2, vec, sjj, sji, tri)


N_ROWVEC = 4
N_PAIRS = N_ROWVEC // 2
N_DIR_REFS = N_PAIRS + 2
SCAN_PASSES = 1
DECAY_PASSES = 3
N_SG = 3
SG_PARTS = (((0, 0, 6, 0), (1, 0, 2, 6)),
            ((1, 2, 4, 0), (2, 0, 4, 4)),
            ((2, 4, 2, 0), (3, 0, 6, 2)))
SG_MIXED = 1


def _sg_expand_matrices():
    m = np.zeros((N_SG, 2 * LANES, E_W), np.float32)
    for sg, parts in enumerate(SG_PARTS):
        for half, (_, h0, nh, slot0) in enumerate(parts):
            for jl in range(4):
                for part in range(J_PARTS):
                    for h in range(h0, h0 + nh):
                        for rep in range(I_REP):
                            m[sg, half * LANES + jl * 32 + part * 8 + h,
                              jl * 128 + part * 32 + (slot0 + h - h0) * I_REP + rep] = 1.0
    return m


def _scan_kernel(*refs, t_chunk, n_batch):
    ins = refs[:2 * N_DIR_REFS]
    r2_ref, s0_ref = refs[2 * N_DIR_REFS:2 * N_DIR_REFS + 2]
    y_refs = refs[2 * N_DIR_REFS + 2:2 * N_DIR_REFS + 4]
    sout_ref = refs[2 * N_DIR_REFS + 4]
    e_ref, s_ref, ybuf_ref, yrev_ref = refs[2 * N_DIR_REFS + 5:]
    step_id = pl.program_id(0)
    assert n_batch == 2
    last = t_chunk - 1

    @pl.when(step_id == 0)
    def _():
        s_ref[...] = s0_ref[...]

    def src(x, s):
        return ins[(s // n_batch) * N_DIR_REFS + x]

    def expand(sg, first, second, passes):
        lhs = jnp.concatenate([first, second], axis=2)
        lhs = lhs.reshape(lhs.shape[0] * SUBLANES, 2 * LANES)
        return _split_dot(lhs, r2_ref[sg], passes).reshape(first.shape[0], SUBLANES, E_W)

    for sg, parts in enumerate(SG_PARTS):
        for pair in range(N_PAIRS):
            halves = []
            for s, _, _, _ in parts:
                ref = src(pair, s)
                if sg == SG_MIXED and s >= n_batch:
                    halves.append(jnp.concatenate([ref[0, s % n_batch, t:t + 1] for t in reversed(range(t_chunk))],
                                                  axis=0))
                else:
                    halves.append(ref[0, s % n_batch])
            e_ref[sg, pair] = expand(sg, halves[0], halves[1], SCAN_PASSES)

    lane = lax.broadcasted_iota(jnp.int32, (SUBLANES, LANES), 1)
    q_even = (lane // PART_LANES) % 2 == 0
    slot = (lane // I_REP) % HEAD_SLOTS

    def shifted(x, slots):
        return x if slots == 0 else pltpu.roll(x, (slots * I_REP) % LANES, 1)

    def allparts(p):
        assert J_PARTS == 4
        return ((p + pltpu.roll(p, 2 * PART_LANES, 1))
                + (pltpu.roll(p, PART_LANES, 1) + pltpu.roll(p, 3 * PART_LANES, 1)))

    def step(s, sg):
        (sa_src, ha, _, slot_a), (sb_src, hb, _, slot_b) = SG_PARTS[sg]
        ta = s if sa_src < n_batch else last - s
        tb = s if sb_src < n_batch else last - s
        te = s if sg == SG_MIXED else ta
        vp = jnp.where(slot < slot_b, shifted(src(N_PAIRS + 1, sa_src)[sa_src % n_batch, ta], slot_a - ha),
                       shifted(src(N_PAIRS + 1, sb_src)[sb_src % n_batch, tb], slot_b - hb))
        vr = pltpu.roll(vp, PART_LANES, 1)
        v = (jnp.where(q_even, vp, vr), jnp.where(q_even, vr, vp))

        def row(x, jj):
            r0 = (x % 2) * 4 + jj // 4
            return e_ref[sg, x // 2, te, r0:r0 + 1, (jj % 4) * LANES:(jj % 4 + 1) * LANES]

        def add(acc, ih, jj, term):
            k = (ih, jj % 4)
            acc[k] = term if k not in acc else acc[k] + term

        def total(acc, ih):
            return allparts((acc[ih, 0] + acc[ih, 1]) + (acc[ih, 2] + acc[ih, 3]))

        acc = {}
        for jj in range(JJ):
            a = row(0, jj)
            for ih in range(I_HI):
                add(acc, ih, jj, s_ref[sg, ih, jj] * a)
        sa = [total(acc, ih) for ih in range(I_HI)]
        acc = {}
        for jj in range(JJ):
            b, k, rr = row(1, jj), row(2, jj), row(3, jj)
            for ih in range(I_HI):
                new = (s_ref[sg, ih, jj] + v[ih] * k) + sa[ih] * b
                s_ref[sg, ih, jj] = new
                add(acc, ih, jj, new * rr)
        y = [total(acc, ih) for ih in range(I_HI)]
        packed = jnp.where(q_even, y[0], y[1])
        ybuf_ref[sg, te] = packed
        if sg == SG_MIXED:
            yrev_ref[last - s] = packed

    def one_token(s, carry):
        for sg in range(N_SG):
            step(s, sg)
        return carry

    lax.fori_loop(0, t_chunk, one_token, 0, unroll=SCAN_UNROLL)

    for s in range(2 * n_batch):
        out = jnp.zeros((t_chunk, SUBLANES, LANES), F32)
        for sg, parts in enumerate(SG_PARTS):
            for s2, h0, nh, slot0 in parts:
                if s2 == s:
                    buf = yrev_ref[...] if (sg == SG_MIXED and s >= n_batch) else ybuf_ref[sg]
                    if h0 != slot0:
                        buf = pltpu.roll(buf, ((h0 - slot0) * I_REP) % LANES, 2)
                    out = jnp.where(jnp.logical_and(slot >= h0, slot < h0 + nh)[None], buf, out)
        y_refs[s // n_batch][s % n_batch] = out

    for sg, parts in enumerate(SG_PARTS):
        halves = []
        toks = []
        for s, _, _, _ in parts:
            tile0, tok = (t_chunk - SUBLANES, SUBLANES - 1) if s < n_batch else (0, 0)
            g8 = src(N_PAIRS, s)[0, s % n_batch, tile0:tile0 + SUBLANES, :]
            halves.append(_lane_blocks_to_rows(jnp.concatenate([g8, g8], axis=1)))
            toks.append(tok)
        res = expand(sg, halves[0], halves[1], DECAY_PASSES)
        first_rows, second_rows = res[toks[0]], res[toks[1]]
        for jj in range(JJ):
            r0 = jj // 4
            blk = slice((jj % 4) * LANES, (jj % 4 + 1) * LANES)
            scale = jnp.where(slot[0:1] < parts[1][3], first_rows[r0:r0 + 1, blk], second_rows[r0:r0 + 1, blk])
            for ih in range(I_HI):
                s_ref[sg, ih, jj] = s_ref[sg, ih, jj] * scale

    @pl.when(step_id == pl.num_programs(0) - 1)
    def _():
        sout_ref[...] = s_ref[...]


def _rwkv_scan(vt, ab, kr, gm, state0, r2):
    b, n = vt.shape[:2]
    t = SCAN_T
    nc = n // t
    in_specs = []
    args = []
    out_specs = []
    for d in range(2):
        def chunk(i, d=d):
            return i if d == 0 else nc - 1 - i

        pair_spec = pl.BlockSpec((1, b, t, SUBLANES, LANES), lambda i, d=d, chunk=chunk: (d, 0, chunk(i), 0, 0))
        gm_spec = pl.BlockSpec((1, b, t, J_W), lambda i, d=d, chunk=chunk: (d, 0, chunk(i), 0))
        tile_spec = pl.BlockSpec((b, t, SUBLANES, LANES), lambda i, chunk=chunk: (0, chunk(i), 0, 0))
        in_specs += [pair_spec, pair_spec, gm_spec, tile_spec]
        args += [ab, kr, gm, vt]
        out_specs.append(tile_spec)
    state_shape = (N_SG, I_HI, JJ, SUBLANES, LANES)
    in_specs += [_const_spec((N_SG, 2 * LANES, E_W)), _const_spec(state_shape)]
    args += [r2, state0]
    out_specs.append(_const_spec(state_shape))
    y_shape = jax.ShapeDtypeStruct((b, n, SUBLANES, LANES), F32)
    return pl.pallas_call(
        functools.partial(_scan_kernel, t_chunk=t, n_batch=b),
        out_shape=[y_shape, y_shape, jax.ShapeDtypeStruct(state_shape, F32)],
        grid=(nc,),
        in_specs=in_specs,
        out_specs=out_specs,
        scratch_shapes=[pltpu.VMEM((N_SG, N_PAIRS, t, SUBLANES, E_W), F32),
                        pltpu.VMEM(state_shape, F32),
                        pltpu.VMEM((N_SG, t, SUBLANES, LANES), F32),
                        pltpu.VMEM((t, SUBLANES, LANES), F32)],
        compiler_params=_cparams(("arbitrary",)),
        name="rwkv_scan",
    )(*args)


def _mix_out_kernel(x_ref, m_ref, g_ref, f_ref, a_ref, yf_ref, yb_ref, bonus_ref, zg_ref,
                    g2_ref, ln_ref, sii_ref, wf_ref, wa_ref, wr_ref, o_ref):
    x = x_ref[0]
    yy = pltpu.einshape("tkl->t(kl)", yf_ref[0] + yb_ref[0])
    low = lax.broadcasted_iota(jnp.int32, (1, LANES), 1) < LANES // 2
    y = jnp.concatenate([jnp.where(low, yy[:, 2 * c * LANES:(2 * c + 1) * LANES],
                                   pltpu.roll(yy[:, (2 * c + 1) * LANES:(2 * c + 2) * LANES], LANES // 2, 1))
                         for c in range(I_W // (2 * LANES))], axis=1)
    sii = sii_ref[...]
    nblk = I_W // (2 * LANES)
    cnt = float(HEAD_DIM)

    def fold(z):
        acc = z[:, 0:LANES]
        for c in range(1, nblk):
            acc = acc + z[:, c * LANES:(c + 1) * LANES]
        return acc

    def spread(z):
        return jnp.concatenate([z] * nblk, axis=1)

    mu = _split_dot(fold(y), sii, 2) * (1.0 / cnt)
    dlt = y - spread(mu)
    var = _split_dot(fold(dlt * dlt), sii, 2) * (1.0 / cnt)
    ln = ln_ref[...]
    yn = dlt * spread(lax.rsqrt(var + GN_EPS)) * ln[0:1] + ln[1:2]
    gate = _dot(_sigmoid(zg_ref[0]).astype(BF16), g2_ref[...])
    rw = ((yn + bonus_ref[0]) * gate).astype(BF16)
    o = _dot(f_ref[0].astype(BF16), wf_ref[...]) + _dot(a_ref[0].astype(BF16), wa_ref[...]) + _dot(rw, wr_ref[...])
    o_ref[0] = x + m_ref[0] * _rms(o, g_ref[...])


def _mix_out(x, gate_row, g3, fo, ao, y_f, y_b, bonus, zg, g2p, ln, sii, wf, wa, wr, tm):
    b, n, d = x.shape

    def tile(w):
        return pl.BlockSpec((1, tm, w), lambda i, j: (i, j, 0))

    return pl.pallas_call(
        _mix_out_kernel,
        out_shape=jax.ShapeDtypeStruct(x.shape, F32),
        grid=(b, n // tm),
        in_specs=[tile(d), pl.BlockSpec((1, 1, d), lambda i, j: (i, 0, 0)), _const_spec((1, d)),
                  tile(FOURIER_W), tile(ATTN_W),
                  pl.BlockSpec((1, tm, SUBLANES, LANES), lambda i, j: (i, j, 0, 0)),
                  pl.BlockSpec((1, tm, SUBLANES, LANES), lambda i, j: (i, j, 0, 0)),
                  tile(I_W // 2), tile(GATE_LORA),
                  _const_spec((GATE_LORA, I_W // 2)), _const_spec((2, I_W // 2)), _const_spec((LANES, LANES)),
                  _const_spec((FOURIER_W, d)), _const_spec((ATTN_W, d)), _const_spec((I_W // 2, d))],
        out_specs=tile(d),
        compiler_params=_cparams(("arbitrary", "arbitrary")),
        name="mix_out",
    )(x, gate_row, g3, fo, ao, y_f, y_b, bonus, zg, g2p, ln, sii, wf, wa, wr)


def _rope_tables(n_tokens):
    rows_n = n_tokens // GRID_W
    row = jnp.repeat(jnp.arange(rows_n), GRID_W).astype(F32)
    colp = jnp.tile(jnp.arange(GRID_W), rows_n).astype(F32)
    n_freq = HEAD_DIM // 4
    inv = ROPE_BASE ** (-jnp.arange(n_freq, dtype=F32) / n_freq)
    ang = jnp.concatenate([row[:, None] * inv, colp[:, None] * inv], axis=-1)
    cos = jnp.tile(jnp.cos(ang), (1, 4))
    sin = jnp.tile(jnp.sin(ang), (1, 4))
    return cos, sin


def _layer_weights(li, p):
    f = p["ffn1_wo"].shape[1]
    fp = -(-f // (2 * LANES)) * (2 * LANES)
    out = {}
    for name in ("ffn1", "ffn2"):
        wi = p[name + "_wi"][li]
        wo = p[name + "_wo"][li]
        gap = jnp.zeros((wi.shape[0], fp - f), BF16)
        out[name] = (jnp.concatenate([wi[:, :f].astype(BF16), gap, wi[:, f:].astype(BF16), gap], axis=1),
                     jnp.pad(wo, ((0, fp - f), (0, 0))).astype(BF16))
    w_in = p["mix_w_in"][li]
    offs = np.cumsum([0, FOURIER_W, ATTN_W, KV_W, KV_W, RWKV_W, RWKV_W, RWKV_W,
                      DECAY_LORA, DECAY_LORA, ICLR_LORA, ICLR_LORA, GATE_LORA])
    part = [w_in[:, offs[i]:offs[i + 1]] for i in range(12)]
    wf, wq, wk, wv, wrr, wrk, wrv = part[:7]
    cols = {"f": wf, "q": wq, "k": wk, "v": wv,
            "r": _take_cols(wrr, J_IDX), "rk": _take_cols(wrk, J_IDX), "rv": _take_cols(wrv, I_IDX_COMPACT),
            "lora": jnp.concatenate(part[7:11], axis=1), "g": part[11]}
    out["w_all"] = jnp.concatenate([cols[n] for n, _ in MIX_COLS], axis=1).astype(BF16)
    conv = p["rwkv_conv"][li]
    out["cwr"] = _take_cols(conv[:, :RWKV_W], J_IDX)
    out["cwk"] = _take_cols(conv[:, RWKV_W:2 * RWKV_W], J_IDX)
    out["cwv"] = _take_cols(conv[:, 2 * RWKV_W:], I_IDX_COMPACT)
    def per_direction(w2):
        w2 = _take_cols(w2, J_IDX)
        z = jnp.zeros_like(w2[0])
        w2 = jnp.stack([jnp.concatenate([w2[0], z], axis=0), jnp.concatenate([z, w2[1]], axis=0)])
        hi = w2.astype(BF16)
        return jnp.stack([hi, (w2 - hi.astype(F32)).astype(BF16)])

    out["w0"] = _take_cols(p["rwkv_w0"][li], J_IDX)
    out["w2"] = per_direction(p["rwkv_w2"][li])
    out["a0"] = _take_cols(p["rwkv_a0"][li], J_IDX)
    out["a2"] = per_direction(p["rwkv_a2"][li])
    out["vec"] = _take_cols(jnp.stack([p["rwkv_k_k"][li], p["rwkv_k_a"][li], p["rwkv_r_k"][li].reshape(-1)]), J_IDX)
    out["g2"] = _take_cols(p["rwkv_g2"][li], I_IDX_COMPACT).astype(BF16)
    out["ln"] = _take_cols(jnp.stack([p["rwkv_ln_g"][li], p["rwkv_ln_b"][li]]), I_IDX_COMPACT)
    w_out = p["mix_w_out"][li]
    out["wo_f"] = w_out[:FOURIER_W].astype(BF16)
    out["wo_a"] = w_out[FOURIER_W:FOURIER_W + ATTN_W].astype(BF16)
    out["wo_r"] = _take_rows(w_out[FOURIER_W + ATTN_W:], I_IDX_COMPACT).astype(BF16)
    sink = p["attn_sink"][li]
    out["sink"] = jnp.concatenate([sink, jnp.zeros((8 - ATTN_HEADS,), F32)])
    return out


def kernel(x, c, ctx, c_ctx, mod_w, mod_b, norm_g, ffn1_wi, ffn1_wo, mix_w_in, mix_w_out, attn_sink,
           rwkv_conv, rwkv_w0, rwkv_w2, rwkv_a0, rwkv_a2, rwkv_g2, rwkv_k_k, rwkv_k_a, rwkv_r_k,
           rwkv_ln_g, rwkv_ln_b, ffn2_wi, ffn2_wo):
    p = dict(ffn1_wi=ffn1_wi, ffn1_wo=ffn1_wo, ffn2_wi=ffn2_wi, ffn2_wo=ffn2_wo, mix_w_in=mix_w_in,
             mix_w_out=mix_w_out, attn_sink=attn_sink, rwkv_conv=rwkv_conv, rwkv_w0=rwkv_w0, rwkv_w2=rwkv_w2,
             rwkv_a0=rwkv_a0, rwkv_a2=rwkv_a2, rwkv_g2=rwkv_g2, rwkv_k_k=rwkv_k_k, rwkv_k_a=rwkv_k_a,
             rwkv_r_k=rwkv_r_k, rwkv_ln_g=rwkv_ln_g, rwkv_ln_b=rwkv_ln_b)
    b, s, d = x.shape
    n_c = ctx.shape[1]
    depth = mod_w.shape[0]
    assert b + 1 <= 8 and s % WINDOW == 0 and n_c % WINDOW == 0 and WINDOW % SCAN_T == 0
    tm_l = 256
    tm_ffn = 512 if s % 512 == 0 else tm_l
    tm_c = min(256, n_c)

    cond8 = jnp.zeros((8, d), F32).at[:b].set(c).at[b].set(c_ctx)
    mod = _ada_mod(cond8, mod_w, mod_b).reshape(depth, 8, N_MOD, d)
    cos_l, sin_l = _rope_tables(s)
    cos_c = jnp.ones((n_c, LANES), F32)
    sin_c = jnp.zeros((n_c, LANES), F32)
    r2 = jnp.asarray(_sg_expand_matrices(), BF16)
    sjj, sji, sii = (jnp.asarray(m, BF16) for m in _seg_matrices())
    zero_state = jnp.zeros((N_SG, I_HI, JJ, SUBLANES, LANES), F32)

    xl, xc = x, ctx
    for li in range(depth):
        need_ctx_out = li < depth - 1
        w = _layer_weights(li, p)
        ml = mod[li, :b]
        mc = jnp.broadcast_to(mod[li, b:b + 1], (b, N_MOD, d))
        g = norm_g[li]
        xl = _ffn_half(xl, ml[:, 0:3], g[0:2], *w["ffn1"], tm_ffn)
        xc = _ffn_half(xc, mc[:, 0:3], g[0:2], *w["ffn1"], tm_c)

        def mixer_in(xx, mm, cos, sin, tm):
            return _mix_in(xx, mm[:, 3:6], g[2:3], w["w_all"], cos, sin, tm)

        fl, ql, kl, vl, rl, rkl, rvl, loral, zgl = mixer_in(xl, ml, cos_l, sin_l, tm_l)
        fc, qc, kc, vc, rc_, rkc, rvc, lorac, zgc = mixer_in(xc, mc, cos_c, sin_c, tm_c)

        attn_l = _attention(ql, kl, vl, kc, vc, w["sink"], True)

        def prep(r_, k_, v_, lora_, tm):
            return _rwkv_prep(r_, k_, v_, lora_, w["cwr"], w["cwk"], w["cwv"], w["w0"], w["w2"],
                              w["a0"], w["a2"], w["vec"], sjj, sji, tm)

        pc = prep(rc_, rkc, rvc, lorac, tm_c)
        plat = prep(rl, rkl, rvl, loral, tm_l)
        ycf, ycb, state_c = _rwkv_scan(pc[0], *pc[2:], zero_state, r2)
        ylf, ylb, _ = _rwkv_scan(plat[0], *plat[2:], state_c, r2)

        def mixer_out(xx, mm, fo, ao, y_f, y_b, bonus, zg, tm):
            return _mix_out(xx, mm[:, 5:6], g[3:4], fo, ao, y_f, y_b, bonus, zg, w["g2"], w["ln"], sii,
                            w["wo_f"], w["wo_a"], w["wo_r"], tm)

        xl = mixer_out(xl, ml, _fourier_mix(fl), attn_l, ylf, ylb, plat[1], zgl, tm_l)
        if need_ctx_out:
            attn_c = _attention(qc, None, None, kc, vc, w["sink"], False)
            xc = mixer_out(xc, mc, _fourier_mix(fc), attn_c, ycf, ycb, pc[1], zgc, tm_c)
            xc = _ffn_half(xc, mc[:, 6:9], g[4:6], *w["ffn2"], tm_c)
        xl = _ffn_half(xl, ml[:, 6:9], g[4:6], *w["ffn2"], tm_ffn)
    return xl
```
